```python
import jax, jax.numpy as jnp
from jax import lax
import numpy as np

D_MODEL = 1024
BATCH = 1
SEQ = 16384
DEPTH = 1

N_HEADS = 16
HEAD_DIM = 64
N_KV_GROUPS = 2
HEADS_PER_GROUP = N_HEADS // N_KV_GROUPS
CMP_BLOCK = 32
CMP_STRIDE = 16
CMP_HIDDEN = 256
SLC_BLOCK = 64
N_SELECT = 16
WINDOW = 512
Q_BLOCK = 128
CONV_WIDTH = 1024
CONV_K = 3
D_FF = 2816
EPS = 1e-6
NEG_INF = -1e30
FORCE_SCORE = 1e9

Q_COLS = N_HEADS * HEAD_DIM
KV_COLS = N_KV_GROUPS * HEAD_DIM
IN_WIDTHS = (Q_COLS, KV_COLS, KV_COLS, KV_COLS, KV_COLS, KV_COLS, KV_COLS, N_HEADS * 3,
             CONV_WIDTH, CONV_WIDTH, CONV_WIDTH, D_MODEL, D_MODEL)
IN_TOTAL = sum(IN_WIDTHS)

kernel_name = "hybrid_nsa_shortconv_macaron"


def rms_norm(x, g):
    xf = x.astype(jnp.float32)
    y = xf * lax.rsqrt(jnp.mean(xf * xf, axis=-1, keepdims=True) + EPS)
    return (y * g.astype(jnp.float32)).astype(x.dtype)


def swiglu(x, w_gate, w_up, w_down):
    return (jax.nn.silu(x @ w_gate) * (x @ w_up)) @ w_down


def compress_blocks(kv, pe, w1, w2):
    b, s, g, d = kv.shape
    chunks = kv.reshape(b, s // CMP_STRIDE, CMP_STRIDE, g, d)
    blocks = jnp.concatenate([chunks[:, :-1], chunks[:, 1:]], axis=2)
    blocks = blocks + pe[None, None, :, None, :]
    n_cmp = blocks.shape[1]
    flat = blocks.transpose(0, 1, 3, 2, 4).reshape(b, n_cmp, g, CMP_BLOCK * d)
    return jax.nn.gelu(flat @ w1) @ w2


def nsa_attention(q, kc, vc, k_slc, v_slc, k_win, v_win):
    b, s, h, d = q.shape
    g = N_KV_GROUPS
    scale = d ** -0.5
    n_cmp = kc.shape[1]
    n_slc = s // SLC_BLOCK
    n_sel = min(N_SELECT, n_slc)
    cmp_start = jnp.arange(n_cmp) * CMP_STRIDE
    cmp_end = cmp_start + CMP_BLOCK - 1
    slc_start = jnp.arange(n_slc) * SLC_BLOCK
    overlap = ((cmp_start[:, None] < slc_start[None, :] + SLC_BLOCK)
               & (cmp_start[:, None] + CMP_BLOCK > slc_start[None, :])).astype(jnp.float32)
    ks_blocks = k_slc.reshape(b, n_slc, SLC_BLOCK, g, d).transpose(0, 3, 1, 2, 4)
    vs_blocks = v_slc.reshape(b, n_slc, SLC_BLOCK, g, d).transpose(0, 3, 1, 2, 4)
    kw = jnp.pad(k_win, ((0, 0), (WINDOW, 0), (0, 0), (0, 0)))
    vw = jnp.pad(v_win, ((0, 0), (WINDOW, 0), (0, 0), (0, 0)))
    qg = q.reshape(b, s, g, HEADS_PER_GROUP, d)
    b_idx = jnp.arange(b)[:, None, None, None]
    g_idx = jnp.arange(g)[None, :, None, None]
    blk = jnp.arange(n_slc)

    def query_block(i):
        q0 = i * Q_BLOCK
        qb = lax.dynamic_slice_in_dim(qg, q0, Q_BLOCK, axis=1)
        t = q0 + jnp.arange(Q_BLOCK)
        sc = jnp.einsum('bqghd,bcgd->bghqc', qb, kc).astype(jnp.float32) * scale
        valid_c = cmp_end[None, :] <= t[:, None]
        pc = jax.nn.softmax(jnp.where(valid_c, sc, NEG_INF), axis=-1)
        pc = jnp.where(valid_c, pc, 0.0)
        o_cmp = jnp.einsum('bghqc,bcgd->bqghd', pc.astype(vc.dtype), vc)
        ps = jnp.einsum('bghqc,cn->bgqn', pc, overlap)
        cur = t // SLC_BLOCK
        forced = (blk[None, :] == 0) | (blk[None, :] == cur[:, None]) | (blk[None, :] == cur[:, None] - 1)
        causal_blk = blk[None, :] * SLC_BLOCK <= t[:, None]
        ps = jnp.where(forced, FORCE_SCORE, jnp.where(causal_blk, ps, -1.0))
        _, idx = lax.top_k(ps, n_sel)
        ksel = ks_blocks[b_idx, g_idx, idx].reshape(b, g, Q_BLOCK, n_sel * SLC_BLOCK, d)
        vsel = vs_blocks[b_idx, g_idx, idx].reshape(b, g, Q_BLOCK, n_sel * SLC_BLOCK, d)
        pos = (idx[..., None] * SLC_BLOCK + jnp.arange(SLC_BLOCK)).reshape(b, g, Q_BLOCK, n_sel * SLC_BLOCK)
        valid_s = (pos <= t[:, None])[:, :, None]
        ss = jnp.einsum('bqghd,bgqkd->bghqk', qb, ksel).astype(jnp.float32) * scale
        psel = jax.nn.softmax(jnp.where(valid_s, ss, NEG_INF), axis=-1)
        o_slc = jnp.einsum('bghqk,bgqkd->bqghd', psel.astype(vsel.dtype), vsel)
        kwb = lax.dynamic_slice_in_dim(kw, q0, WINDOW + Q_BLOCK, axis=1)
        vwb = lax.dynamic_slice_in_dim(vw, q0, WINDOW + Q_BLOCK, axis=1)
        wpos = q0 - WINDOW + jnp.arange(WINDOW + Q_BLOCK)
        diff = t[:, None] - wpos[None, :]
        valid_w = (diff >= 0) & (diff < WINDOW) & (wpos[None, :] >= 0)
        sw = jnp.einsum('bqghd,bkgd->bghqk', qb, kwb).astype(jnp.float32) * scale
        pw = jax.nn.softmax(jnp.where(valid_w, sw, NEG_INF), axis=-1)
        o_win = jnp.einsum('bghqk,bkgd->bqghd', pw.astype(vwb.dtype), vwb)
        return o_cmp, o_slc, o_win

    o_cmp, o_slc, o_win = lax.map(query_block, jnp.arange(s // Q_BLOCK))

    def unblock(o):
        return o.transpose(1, 0, 2, 3, 4, 5).reshape(b, s, h, d)

    return unblock(o_cmp), unblock(o_slc), unblock(o_win)


def short_gated_conv(b_gate, c_gate, x_in, conv_w):
    u = c_gate * x_in
    conv = lax.conv_general_dilated(u, conv_w[:, None, :], window_strides=(1,),
                                    padding=[(CONV_K - 1, 0)],
                                    dimension_numbers=('NWC', 'WIO', 'NWC'),
                                    feature_group_count=CONV_WIDTH)
    return b_gate * conv


def setup_inputs(seed: int = 0) -> dict:
    key = jax.random.key(seed)
    ks = jax.random.split(key, 24)
    L = DEPTH

    def w(k, shape, fan_in):
        return jax.random.normal(k, shape, jnp.float32) * fan_in ** -0.5

    def gain(k):
        return 1.0 + 0.01 * jax.random.normal(k, (L, D_MODEL), jnp.float32)

    flat_cmp = CMP_BLOCK * HEAD_DIM
    return {
        "x": jax.random.normal(ks[0], (BATCH, SEQ, D_MODEL), jnp.float32),
        "ffn1_norm": gain(ks[1]),
        "ffn1_w_gate": w(ks[2], (L, D_MODEL, D_FF), D_MODEL),
        "ffn1_w_up": w(ks[3], (L, D_MODEL, D_FF), D_MODEL),
        "ffn1_w_down": w(ks[4], (L, D_FF, D_MODEL), D_FF),
        "mix_norm": gain(ks[5]),
        "w_in": w(ks[6], (L, D_MODEL, IN_TOTAL), D_MODEL),
        "cmp_pe_k": 0.02 * jax.random.normal(ks[7], (L, CMP_BLOCK, HEAD_DIM), jnp.float32),
        "cmp_pe_v": 0.02 * jax.random.normal(ks[8], (L, CMP_BLOCK, HEAD_DIM), jnp.float32),
        "cmp_k_w1": w(ks[9], (L, flat_cmp, CMP_HIDDEN), flat_cmp),
        "cmp_k_w2": w(ks[10], (L, CMP_HIDDEN, HEAD_DIM), CMP_HIDDEN),
        "cmp_v_w1": w(ks[11], (L, flat_cmp, CMP_HIDDEN), flat_cmp),
        "cmp_v_w2": w(ks[12], (L, CMP_HIDDEN, HEAD_DIM), CMP_HIDDEN),
        "conv_w": w(ks[13], (L, CONV_K, CONV_WIDTH), CONV_K),
        "w_nsa_out": w(ks[14], (L, N_HEADS * HEAD_DIM, D_MODEL), N_HEADS * HEAD_DIM),
        "w_conv_out": w(ks[15], (L, CONV_WIDTH, D_MODEL), CONV_WIDTH),
        "w_out": w(ks[16], (L, D_MODEL, D_MODEL), D_MODEL),
        "ffn2_norm": gain(ks[17]),
        "ffn2_w_gate": w(ks[18], (L, D_MODEL, D_FF), D_MODEL),
        "ffn2_w_up": w(ks[19], (L, D_MODEL, D_FF), D_MODEL),
        "ffn2_w_down": w(ks[20], (L, D_FF, D_MODEL), D_FF),
        "final_norm": 1.0 + 0.01 * jax.random.normal(ks[21], (D_MODEL,), jnp.float32),
    }


def reference(x, ffn1_norm, ffn1_w_gate, ffn1_w_up, ffn1_w_down, mix_norm, w_in,
              cmp_pe_k, cmp_pe_v, cmp_k_w1, cmp_k_w2, cmp_v_w1, cmp_v_w2, conv_w,
              w_nsa_out, w_conv_out, w_out, ffn2_norm, ffn2_w_gate, ffn2_w_up,
              ffn2_w_down, final_norm):
    b, s, _ = x.shape
    split_at = [int(v) for v in np.cumsum(IN_WIDTHS)[:-1]]
    for l in range(DEPTH):
        x = x + 0.5 * swiglu(rms_norm(x, ffn1_norm[l]), ffn1_w_gate[l], ffn1_w_up[l], ffn1_w_down[l])
        h = rms_norm(x, mix_norm[l])
        proj = h @ w_in[l]
        (q, k_c, v_c, k_s, v_s, k_w, v_w, nsa_g,
         conv_b, conv_c, conv_x, gate_a, gate_b) = jnp.split(proj, split_at, axis=-1)
        q = q.reshape(b, s, N_HEADS, HEAD_DIM)
        kv_shape = (b, s, N_KV_GROUPS, HEAD_DIM)
        kc = compress_blocks(k_c.reshape(kv_shape), cmp_pe_k[l], cmp_k_w1[l], cmp_k_w2[l])
        vc = compress_blocks(v_c.reshape(kv_shape), cmp_pe_v[l], cmp_v_w1[l], cmp_v_w2[l])
        o_cmp, o_slc, o_win = nsa_attention(q, kc, vc, k_s.reshape(kv_shape), v_s.reshape(kv_shape),
                                            k_w.reshape(kv_shape), v_w.reshape(kv_shape))
        g3 = jax.nn.sigmoid(nsa_g.reshape(b, s, N_HEADS, 3))
        o_nsa = g3[..., 0:1] * o_cmp + g3[..., 1:2] * o_slc + g3[..., 2:3] * o_win
        y_a = o_nsa.reshape(b, s, N_HEADS * HEAD_DIM) @ w_nsa_out[l]
        y_b = short_gated_conv(conv_b, conv_c, conv_x, conv_w[l]) @ w_conv_out[l]
        merged = jax.nn.sigmoid(gate_a) * y_a + jax.nn.sigmoid(gate_b) * y_b
        x = x + merged @ w_out[l]
        x = x + 0.5 * swiglu(rms_norm(x, ffn2_norm[l]), ffn2_w_gate[l], ffn2_w_up[l], ffn2_w_down[l])
    return rms_norm(x, final_norm)
```

```python
import functools

import jax
import jax.numpy as jnp
import numpy as np
from jax import lax
from jax.experimental import pallas as pl
from jax.experimental.pallas import tpu as pltpu

D_MODEL = 1024
N_HEADS = 16
HEAD_DIM = 64
N_GROUPS = 2
HEADS_PER_GROUP = N_HEADS // N_GROUPS
N_PAIRS = N_HEADS // 2
PAIRS_PER_GROUP = N_PAIRS // N_GROUPS
CMP_BLOCK = 32
CMP_STRIDE = 16
CMP_HIDDEN = 256
SLC_BLOCK = 64
N_SELECT = 16
WINDOW = 512
CONV_WIDTH = 1024
CONV_K = 3
D_FF = 2816
EPS = 1e-6
NEG = -1e30
FORCE_SCORE = 1e9
LANES = 128
SUBLANES = 8

P_Q, P_CB, P_CC, P_CX, P_GA, P_GB = 0, 8, 16, 24, 32, 40
P_KC, P_VC, P_KS, P_VS, P_KW, P_VW, P_G3 = 48, 49, 50, 51, 52, 53, 54
P_BLOCKS = 55
P_COLS = P_BLOCKS * LANES

F32 = jnp.float32
BF16 = jnp.bfloat16

VMEM_LIMIT = 52 * 1024 * 1024


def _dot(a, b):
    return jnp.dot(a, b, preferred_element_type=F32)


def _dot_nt(a, b):
    return lax.dot_general(a, b, (((1,), (1,)), ((), ())), preferred_element_type=F32)


def _sigmoid(x):
    return 1.0 / (1.0 + jnp.exp(-x))


def _rms(x, g):
    return x * lax.rsqrt(jnp.mean(x * x, axis=-1, keepdims=True) + EPS) * g


def _params(sem, limit=VMEM_LIMIT):
    return pltpu.CompilerParams(dimension_semantics=sem, vmem_limit_bytes=limit)


FFN_TM = 512
FFN_TF = 1408


def _ffn_kernel(x_ref, g_ref, wg_ref, wu_ref, wd_ref, fg_ref, o_ref, h_ref, acc_ref, *, final_norm):
    f = pl.program_id(1)

    @pl.when(f == 0)
    def _():
        h_ref[...] = _rms(x_ref[...], g_ref[...]).astype(BF16)
        acc_ref[...] = jnp.zeros_like(acc_ref)

    h = h_ref[...]
    gate = _dot(h, wg_ref[...])
    up = _dot(h, wu_ref[...])
    a = (gate * _sigmoid(gate) * up).astype(BF16)
    acc_ref[...] += _dot(a, wd_ref[...])

    @pl.when(f == pl.num_programs(1) - 1)
    def _():
        y = x_ref[...] + 0.5 * acc_ref[...]
        if final_norm:
            y = _rms(y, fg_ref[...])
        o_ref[...] = y


def _ffn(x, norm_g, wg, wu, wd, final_g, final_norm):
    s = x.shape[0]
    grid = (s // FFN_TM, D_FF // FFN_TF)
    return pl.pallas_call(
        functools.partial(_ffn_kernel, final_norm=final_norm),
        grid=grid,
        in_specs=[
            pl.BlockSpec((FFN_TM, D_MODEL), lambda i, f: (i, 0)),
            pl.BlockSpec((1, D_MODEL), lambda i, f: (0, 0)),
            pl.BlockSpec((D_MODEL, FFN_TF), lambda i, f: (0, f)),
            pl.BlockSpec((D_MODEL, FFN_TF), lambda i, f: (0, f)),
            pl.BlockSpec((FFN_TF, D_MODEL), lambda i, f: (f, 0)),
            pl.BlockSpec((1, D_MODEL), lambda i, f: (0, 0)),
        ],
        out_specs=pl.BlockSpec((FFN_TM, D_MODEL), lambda i, f: (i, 0)),
        out_shape=jax.ShapeDtypeStruct((s, D_MODEL), F32),
        scratch_shapes=[pltpu.VMEM((FFN_TM, D_MODEL), BF16), pltpu.VMEM((FFN_TM, D_MODEL), F32)],
        compiler_params=_params(("parallel", "arbitrary")),
        name="ffn_final" if final_norm else "ffn",
    )(x, norm_g, wg, wu, wd, final_g)


PROJ_TM = 1024
PROJ_TN = 640


def _proj_kernel(x_ref, g_ref, w_ref, o_ref, h_ref):
    @pl.when(pl.program_id(1) == 0)
    def _():
        h_ref[...] = _rms(x_ref[...], g_ref[...]).astype(BF16)

    o_ref[...] = _dot(h_ref[...], w_ref[...]).astype(BF16)


def _proj(x, norm_g, w):
    s = x.shape[0]
    grid = (s // PROJ_TM, P_COLS // PROJ_TN)
    return pl.pallas_call(
        _proj_kernel,
        grid=grid,
        in_specs=[
            pl.BlockSpec((PROJ_TM, D_MODEL), lambda i, j: (i, 0)),
            pl.BlockSpec((1, D_MODEL), lambda i, j: (0, 0)),
            pl.BlockSpec((D_MODEL, PROJ_TN), lambda i, j: (0, j)),
        ],
        out_specs=pl.BlockSpec((PROJ_TM, PROJ_TN), lambda i, j: (i, j)),
        out_shape=jax.ShapeDtypeStruct((s, P_COLS), BF16),
        scratch_shapes=[pltpu.VMEM((PROJ_TM, D_MODEL), BF16)],
        compiler_params=_params(("parallel", "arbitrary")),
        name="proj",
    )(x, norm_g, w)


def _proj_weight(w_in):
    q, kc, vc, ks, vs, kw, vw, g3, cb, cc, cx, ga, gb = jnp.split(
        w_in, np.cumsum([1024, 128, 128, 128, 128, 128, 128, 48, 1024, 1024, 1024, 1024])[:].tolist(), axis=1)
    g3 = jnp.pad(g3, ((0, 0), (0, LANES - g3.shape[1])))
    w = jnp.concatenate([q * (HEAD_DIM ** -0.5), cb, cc, cx, ga, gb, kc, vc, ks, vs, kw, vw, g3], axis=1)
    return w.astype(BF16)


def _gelu_tanh(x):
    return 0.5 * x * (1.0 + jnp.tanh(np.sqrt(2.0 / np.pi).astype(np.float32) * (x + 0.044715 * (x * x * x))))


def _compress_kernel(cf_ref, pe_ref, w1_ref, w2_ref, o_ref):
    cf = cf_ref[0].astype(F32)
    top = (cf + pe_ref[0, 0]).astype(BF16)
    bot = (cf + pe_ref[0, 1]).astype(BF16)
    a = _dot(top, w1_ref[0, 0, 0])
    b = _dot(bot, w1_ref[0, 0, 1])
    n = a.shape[0]
    pre = a + pltpu.roll(b, n - 1, 0)
    h = _gelu_tanh(pre).astype(BF16)
    o_ref[0, 0, 0] = _dot(h, w2_ref[0, 0]).astype(BF16)
    o_ref[0, 0, 1] = _dot(h, w2_ref[0, 1]).astype(BF16)


def _compress(cf, pe, w1, w2):
    n = cf.shape[1]
    width = CMP_STRIDE * LANES
    return pl.pallas_call(
        _compress_kernel,
        grid=(2, N_GROUPS),
        in_specs=[
            pl.BlockSpec((1, n, width), lambda t, g: (t, 0, 0)),
            pl.BlockSpec((1, 2, 1, width), lambda t, g: (t, 0, 0, 0)),
            pl.BlockSpec((1, 1, 2, width, CMP_HIDDEN), lambda t, g: (t, g, 0, 0, 0)),
            pl.BlockSpec((1, 2, CMP_HIDDEN, LANES), lambda t, g: (t, 0, 0, 0)),
        ],
        out_specs=pl.BlockSpec((1, 1, 2, n, LANES), lambda t, g: (t, g, 0, 0, 0)),
        out_shape=jax.ShapeDtypeStruct((2, N_GROUPS, 2, n, LANES), BF16),
        compiler_params=_params(("parallel", "parallel")),
        name="compress",
    )(cf, pe, w1, w2)


def _compress_weights(pe, w1, w2):
    half = CMP_STRIDE * HEAD_DIM
    w1h = w1.reshape(2, CMP_STRIDE, HEAD_DIM, CMP_HIDDEN)
    zeros = jnp.zeros_like(w1h)
    w1g = jnp.stack([jnp.concatenate([w1h, zeros], axis=2), jnp.concatenate([zeros, w1h], axis=2)])
    w1g = w1g.reshape(N_GROUPS, 2, CMP_STRIDE * LANES, CMP_HIDDEN).astype(BF16)
    peh = pe.reshape(2, CMP_STRIDE, HEAD_DIM)
    pe2 = jnp.concatenate([peh, peh], axis=2).reshape(2, 1, CMP_STRIDE * LANES)
    z2 = jnp.zeros_like(w2)
    w2p = jnp.stack([jnp.concatenate([w2, z2], axis=1), jnp.concatenate([z2, w2], axis=1)]).astype(BF16)
    del half
    return pe2, w1g, w2p


CMP_TQ = 256


def _cmp_kernel(q_ref, ckv_ref, ov_ref, g3_ref, o_ref, sel_ref, *, n_slc):
    tq = q_ref.shape[0]
    n_cmp = ckv_ref.shape[3]
    q0 = pl.program_id(0) * tq
    t = q0 + lax.broadcasted_iota(jnp.int32, (tq, 1), 0)
    c_end = lax.broadcasted_iota(jnp.int32, (1, n_cmp), 1) * CMP_STRIDE + (CMP_BLOCK - 1)
    valid = c_end <= t
    gates = _sigmoid(g3_ref[...].astype(F32))

    pcsum = [None] * N_GROUPS
    for p in range(N_PAIRS):
        g = p // PAIRS_PER_GROUP
        qp = q_ref[:, p * LANES:(p + 1) * LANES]
        o_pair = None
        for e in range(2):
            h = 2 * p + e
            s = jnp.where(valid, _dot_nt(qp, ckv_ref[0, g, e]), NEG)
            m = jnp.max(s, axis=-1, keepdims=True)
            pr = jnp.where(valid, jnp.exp(s - m), 0.0)
            l = jnp.sum(pr, axis=-1, keepdims=True)
            pc = pr * (1.0 / jnp.where(l > 0.0, l, 1.0))
            pcsum[g] = pc if pcsum[g] is None else pcsum[g] + pc
            o_h = _dot(pc.astype(BF16), ckv_ref[1, g, e]) * gates[:, 3 * h:3 * h + 1]
            o_pair = o_h if o_pair is None else o_pair + o_h
        o_ref[:, p * LANES:(p + 1) * LANES] = o_pair.astype(BF16)

    blk = lax.broadcasted_iota(jnp.int32, (tq, n_slc), 1)
    blk_f = blk.astype(F32)
    cur = t // SLC_BLOCK
    forced = (blk == 0) | (blk == cur) | (blk == cur - 1)
    causal = blk * SLC_BLOCK <= t
    ov = ov_ref[...]
    for g in range(N_GROUPS):
        hi = pcsum[g].astype(BF16)
        lo_part = (pcsum[g] - hi.astype(F32)).astype(BF16)
        ps = _dot(hi, ov) + _dot(lo_part, ov)
        work = jnp.where(forced, FORCE_SCORE, jnp.where(causal, ps, -1.0))
        sel = jnp.zeros((tq, n_slc), F32)
        for _ in range(N_SELECT):
            m = jnp.max(work, axis=-1, keepdims=True)
            first = jnp.min(jnp.where(work == m, blk_f, float(n_slc)), axis=-1, keepdims=True)
            hit = blk_f == first
            sel = jnp.where(hit, 1.0, sel)
            work = jnp.where(hit, -jnp.inf, work)
        sel_ref[:, g * n_slc:(g + 1) * n_slc] = sel.astype(BF16)


def _cmp_attention(p_arr, ckv, overlap):
    s = p_arr.shape[0]
    n_cmp = ckv.shape[3]
    n_slc = s // SLC_BLOCK
    return pl.pallas_call(
        functools.partial(_cmp_kernel, n_slc=n_slc),
        grid=(s // CMP_TQ,),
        in_specs=[
            pl.BlockSpec((CMP_TQ, D_MODEL), lambda i: (i, P_Q // 8)),
            pl.BlockSpec((2, N_GROUPS, 2, n_cmp, LANES), lambda i: (0, 0, 0, 0, 0)),
            pl.BlockSpec((n_cmp, n_slc), lambda i: (0, 0)),
            pl.BlockSpec((CMP_TQ, LANES), lambda i: (i, P_G3)),
        ],
        out_specs=[
            pl.BlockSpec((CMP_TQ, D_MODEL), lambda i: (i, 0)),
            pl.BlockSpec((CMP_TQ, N_GROUPS * n_slc), lambda i: (i, 0)),
        ],
        out_shape=[
            jax.ShapeDtypeStruct((s, D_MODEL), BF16),
            jax.ShapeDtypeStruct((s, N_GROUPS * n_slc), BF16),
        ],
        compiler_params=_params(("parallel",)),
        name="cmp_attention",
    )(p_arr, ckv, overlap, p_arr)


def _overlap_matrix(n_cmp, n_slc):
    c0 = np.arange(n_cmp)[:, None] * CMP_STRIDE
    s0 = np.arange(n_slc)[None, :] * SLC_BLOCK
    ov = (c0 < s0 + SLC_BLOCK) & (c0 + CMP_BLOCK > s0)
    return jnp.asarray(ov, dtype=BF16)


FLASH_TQ = 256
SLC_TK = 512
WIN_TK = 256


def _flash_kernel(qi_ref, ki_ref, fl_ref, *refs, mode, n_slc, branch):
    if mode == "slc":
        q_ref, k_ref, v_ref, g3_ref, sel_ref, o_ref, m_ref, l_ref, acc_ref = refs
    else:
        q_ref, k_ref, v_ref, g3_ref, o_ref, m_ref, l_ref, acc_ref = refs
    step = pl.program_id(0)
    tq = q_ref.shape[0]
    tk = k_ref.shape[0]
    flags = fl_ref[step]

    @pl.when((flags & 1) != 0)
    def _():
        m_ref[...] = jnp.full_like(m_ref, NEG)
        l_ref[...] = jnp.zeros_like(l_ref)
        acc_ref[...] = jnp.zeros_like(acc_ref)

    q0 = qi_ref[step] * tq
    k0 = ki_ref[step] * tk
    t = q0 + lax.broadcasted_iota(jnp.int32, (tq, 1), 0)
    key = k0 + lax.broadcasted_iota(jnp.int32, (1, tk), 1)

    if mode == "slc":
        blk = lax.broadcasted_iota(jnp.int32, (n_slc, tk), 0)
        key_blk = (k0 + lax.broadcasted_iota(jnp.int32, (n_slc, tk), 1)) // SLC_BLOCK
        expand = (blk == key_blk).astype(BF16)
        causal = key <= t
        bias = []
        for g in range(N_GROUPS):
            member = _dot(sel_ref[:, g * n_slc:(g + 1) * n_slc], expand)
            bias.append(jnp.where((member > 0.5) & causal, 0.0, NEG))
    else:
        d = t - key
        band = jnp.where((d >= 0) & (d < WINDOW), 0.0, NEG)
        bias = [band] * N_GROUPS

    kt = k_ref[...]
    vt = v_ref[...]
    lo_k = lax.broadcasted_iota(jnp.int32, (tk, LANES), 1) < HEAD_DIM
    kr = pltpu.roll(kt, HEAD_DIM, 1)
    vr = pltpu.roll(vt, HEAD_DIM, 1)
    zero = jnp.zeros_like(kt)
    k_half = [[jnp.where(lo_k, kt, zero), jnp.where(lo_k, zero, kr)],
              [jnp.where(lo_k, kr, zero), jnp.where(lo_k, zero, kt)]]
    v_half = [[jnp.where(lo_k, vt, zero), jnp.where(lo_k, zero, vr)],
              [jnp.where(lo_k, vr, zero), jnp.where(lo_k, zero, vt)]]

    lo_q = lax.broadcasted_iota(jnp.int32, (tq, LANES), 1) < HEAD_DIM
    for p in range(N_PAIRS):
        g = p // PAIRS_PER_GROUP
        qp = q_ref[:, p * LANES:(p + 1) * LANES]
        alphas = []
        pv = None
        for e in range(2):
            h = 2 * p + e
            s = _dot_nt(qp, k_half[g][e]) + bias[g]
            m_prev = m_ref[h]
            m_new = jnp.maximum(m_prev, jnp.max(s, axis=-1, keepdims=True))
            alpha = jnp.exp(m_prev - m_new)
            pr = jnp.exp(s - m_new)
            l_ref[h] = alpha * l_ref[h] + jnp.sum(pr, axis=-1, keepdims=True)
            m_ref[h] = m_new
            pv_h = _dot(pr.astype(BF16), v_half[g][e])
            pv = pv_h if pv is None else pv + pv_h
            alphas.append(alpha)
        acc_ref[p] = acc_ref[p] * jnp.where(lo_q, alphas[0], alphas[1]) + pv

    @pl.when((flags & 2) != 0)
    def _():
        gates = _sigmoid(g3_ref[...].astype(F32))
        for p in range(N_PAIRS):
            scale = [gates[:, 3 * (2 * p + e) + branch:3 * (2 * p + e) + branch + 1] / l_ref[2 * p + e]
                     for e in range(2)]
            o_ref[:, p * LANES:(p + 1) * LANES] = (acc_ref[p] * jnp.where(lo_q, scale[0], scale[1])).astype(BF16)


def _flash_tables(s, tq, tk, mode):
    qi, ki, fl = [], [], []
    for i in range(s // tq):
        q_lo, q_hi = i * tq, i * tq + tq - 1
        first = 0 if mode == "slc" else max(0, (q_lo - (WINDOW - 1)) // tk)
        last = q_hi // tk
        for k in range(first, last + 1):
            qi.append(i)
            ki.append(k)
            fl.append((1 if k == first else 0) | (2 if k == last else 0))
    return (jnp.asarray(qi, jnp.int32), jnp.asarray(ki, jnp.int32), jnp.asarray(fl, jnp.int32))


def _flash(p_arr, sel, mode):
    s = p_arr.shape[0]
    n_slc = s // SLC_BLOCK
    tq = FLASH_TQ
    tk = SLC_TK if mode == "slc" else WIN_TK
    k_col, v_col, branch = (P_KS, P_VS, 1) if mode == "slc" else (P_KW, P_VW, 2)
    qi, ki, fl = _flash_tables(s, tq, tk, mode)
    in_specs = [
        pl.BlockSpec((tq, D_MODEL), lambda n, qi, ki, fl: (qi[n], P_Q // 8)),
        pl.BlockSpec((tk, LANES), lambda n, qi, ki, fl: (ki[n], k_col)),
        pl.BlockSpec((tk, LANES), lambda n, qi, ki, fl: (ki[n], v_col)),
        pl.BlockSpec((tq, LANES), lambda n, qi, ki, fl: (qi[n], P_G3)),
    ]
    args = [p_arr, p_arr, p_arr, p_arr]
    if mode == "slc":
        in_specs.append(pl.BlockSpec((tq, N_GROUPS * n_slc), lambda n, qi, ki, fl: (qi[n], 0)))
        args.append(sel)
    grid_spec = pltpu.PrefetchScalarGridSpec(
        num_scalar_prefetch=3,
        grid=(qi.shape[0],),
        in_specs=in_specs,
        out_specs=pl.BlockSpec((tq, D_MODEL), lambda n, qi, ki, fl: (qi[n], 0)),
        scratch_shapes=[
            pltpu.VMEM((N_HEADS, tq, 1), F32),
            pltpu.VMEM((N_HEADS, tq, 1), F32),
            pltpu.VMEM((N_PAIRS, tq, LANES), F32),
        ],
    )
    return pl.pallas_call(
        functools.partial(_flash_kernel, mode=mode, n_slc=n_slc, branch=branch),
        grid_spec=grid_spec,
        out_shape=jax.ShapeDtypeStruct((s, D_MODEL), BF16),
        compiler_params=_params(("arbitrary",)),
        name="flash_" + mode,
    )(qi, ki, fl, *args)


OUT_TM = 512
PREV_ROWS = 16


def _out_kernel(x_ref, oc_ref, os_ref, ow_ref, cb_ref, cc_ref, cx_ref, pc_ref, px_ref, ga_ref, gb_ref,
                cw_ref, wa_ref, wb_ref, wo_ref, o_ref, u_ref):
    tm = x_ref.shape[0]
    o_nsa = oc_ref[...].astype(F32) + os_ref[...].astype(F32) + ow_ref[...].astype(F32)
    y_a = _dot(o_nsa.astype(BF16), wa_ref[...])

    prev = pc_ref[...].astype(F32) * px_ref[...].astype(F32)
    u_ref[0:PREV_ROWS, :] = jnp.where(pl.program_id(0) == 0, 0.0, prev)
    u_ref[PREV_ROWS:, :] = cc_ref[...].astype(F32) * cx_ref[...].astype(F32)
    cw = cw_ref[...]
    conv = (cw[0:1, :] * u_ref[pl.ds(PREV_ROWS - 2, tm), :]
            + cw[1:2, :] * u_ref[pl.ds(PREV_ROWS - 1, tm), :]
            + cw[2:3, :] * u_ref[pl.ds(PREV_ROWS, tm), :])
    y_b = _dot((cb_ref[...].astype(F32) * conv).astype(BF16), wb_ref[...])

    merged = _sigmoid(ga_ref[...].astype(F32)) * y_a + _sigmoid(gb_ref[...].astype(F32)) * y_b
    o_ref[...] = x_ref[...] + _dot(merged.astype(BF16), wo_ref[...])


def _out(x1, o_cmp, o_slc, o_win, p_arr, conv_w, w_nsa_out, w_conv_out, w_out):
    s = x1.shape[0]
    tm = OUT_TM
    row = lambda i: (i, 0)
    prev_blocks = tm // PREV_ROWS

    def seg(col):
        return pl.BlockSpec((tm, D_MODEL), lambda i: (i, col // 8))

    def seg_prev(col):
        return pl.BlockSpec((PREV_ROWS, D_MODEL), lambda i: (jnp.maximum(i * prev_blocks - 1, 0), col // 8))

    full = lambda shape: pl.BlockSpec(shape, lambda i: (0, 0))
    return pl.pallas_call(
        _out_kernel,
        grid=(s // tm,),
        in_specs=[
            pl.BlockSpec((tm, D_MODEL), row), pl.BlockSpec((tm, D_MODEL), row),
            pl.BlockSpec((tm, D_MODEL), row), pl.BlockSpec((tm, D_MODEL), row),
            seg(P_CB), seg(P_CC), seg(P_CX), seg_prev(P_CC), seg_prev(P_CX), seg(P_GA), seg(P_GB),
            full((CONV_K, CONV_WIDTH)), full((D_MODEL, D_MODEL)), full((CONV_WIDTH, D_MODEL)),
            full((D_MODEL, D_MODEL)),
        ],
        out_specs=pl.BlockSpec((tm, D_MODEL), row),
        out_shape=jax.ShapeDtypeStruct((s, D_MODEL), F32),
        scratch_shapes=[pltpu.VMEM((tm + PREV_ROWS, CONV_WIDTH), F32)],
        compiler_params=_params(("parallel",)),
        name="merge_out",
    )(x1, o_cmp, o_slc, o_win, p_arr, p_arr, p_arr, p_arr, p_arr, p_arr, p_arr,
      conv_w, w_nsa_out, w_conv_out, w_out)


def _layer(x, ffn1_norm, ffn1_w_gate, ffn1_w_up, ffn1_w_down, mix_norm, w_in, cmp_pe_k, cmp_pe_v,
           cmp_k_w1, cmp_k_w2, cmp_v_w1, cmp_v_w2, conv_w, w_nsa_out, w_conv_out, w_out,
           ffn2_norm, ffn2_w_gate, ffn2_w_up, ffn2_w_down, final_g, final_norm):
    s = x.shape[0]
    n_cmp = s // CMP_STRIDE
    n_slc = s // SLC_BLOCK
    row = lambda v: v.reshape(1, -1).astype(F32)

    x1 = _ffn(x, row(ffn1_norm), ffn1_w_gate.astype(BF16), ffn1_w_up.astype(BF16),
              ffn1_w_down.astype(BF16), row(final_g), False)
    p_arr = _proj(x1, row(mix_norm), _proj_weight(w_in))

    cf = jnp.stack([p_arr[:, P_KC * LANES:(P_KC + 1) * LANES].reshape(n_cmp, CMP_STRIDE * LANES),
                    p_arr[:, P_VC * LANES:(P_VC + 1) * LANES].reshape(n_cmp, CMP_STRIDE * LANES)])
    pek, w1k, w2k = _compress_weights(cmp_pe_k, cmp_k_w1, cmp_k_w2)
    pev, w1v, w2v = _compress_weights(cmp_pe_v, cmp_v_w1, cmp_v_w2)
    ckv = _compress(cf, jnp.stack([pek, pev]), jnp.stack([w1k, w1v]), jnp.stack([w2k, w2v]))

    o_cmp, sel = _cmp_attention(p_arr, ckv, _overlap_matrix(n_cmp, n_slc))
    o_slc = _flash(p_arr, sel, "slc")
    o_win = _flash(p_arr, None, "win")
    x2 = _out(x1, o_cmp, o_slc, o_win, p_arr, conv_w.astype(F32), w_nsa_out.astype(BF16),
              w_conv_out.astype(BF16), w_out.astype(BF16))
    return _ffn(x2, row(ffn2_norm), ffn2_w_gate.astype(BF16), ffn2_w_up.astype(BF16),
                ffn2_w_down.astype(BF16), row(final_g), final_norm)


def kernel(x, ffn1_norm, ffn1_w_gate, ffn1_w_up, ffn1_w_down, mix_norm, w_in, cmp_pe_k, cmp_pe_v,
           cmp_k_w1, cmp_k_w2, cmp_v_w1, cmp_v_w2, conv_w, w_nsa_out, w_conv_out, w_out, ffn2_norm,
           ffn2_w_gate, ffn2_w_up, ffn2_w_down, final_norm):
    batch, _, _ = x.shape
    depth = ffn1_norm.shape[0]
    outs = []
    for b in range(batch):
        xb = x[b]
        for l in range(depth):
            xb = _layer(xb, ffn1_norm[l], ffn1_w_gate[l], ffn1_w_up[l], ffn1_w_down[l], mix_norm[l],
                        w_in[l], cmp_pe_k[l], cmp_pe_v[l], cmp_k_w1[l], cmp_k_w2[l], cmp_v_w1[l],
                        cmp_v_w2[l], conv_w[l], w_nsa_out[l], w_conv_out[l], w_out[l], ffn2_norm[l],
                        ffn2_w_gate[l], ffn2_w_up[l], ffn2_w_down[l], final_norm, l == depth - 1)
        outs.append(xb)
    return jnp.stack(outs)
```

```python
import functools

import jax
import jax.numpy as jnp
import numpy as np
from jax import lax
from jax.experimental import pallas as pl
from jax.experimental.pallas import tpu as pltpu

D_MODEL = 1024
N_HEADS = 16
HEAD_DIM = 64
N_GROUPS = 2
HEADS_PER_GROUP = N_HEADS // N_GROUPS
N_PAIRS = N_HEADS // 2
PAIRS_PER_GROUP = N_PAIRS // N_GROUPS
CMP_BLOCK = 32
CMP_STRIDE = 16
CMP_HIDDEN = 256
SLC_BLOCK = 64
N_SELECT = 16
WINDOW = 512
CONV_WIDTH = 1024
CONV_K = 3
D_FF = 2816
EPS = 1e-6
NEG = -1e30
FORCE_SCORE = 1e9
LOG2E = float(np.log2(np.e))
LANES = 128

P_Q, P_CB, P_CC, P_CX, P_GA, P_GB = 0, 8, 16, 24, 32, 40
P_KC, P_VC, P_KS, P_VS, P_KW, P_VW, P_G3 = 48, 49, 50, 51, 52, 53, 54
P_BLOCKS = 55
P_COLS = P_BLOCKS * LANES

F32 = jnp.float32
BF16 = jnp.bfloat16

VMEM_LIMIT = 52 * 1024 * 1024


def _dot(a, b):
    return jnp.dot(a, b, preferred_element_type=F32)


def _dot_nt(a, b):
    return lax.dot_general(a, b, (((1,), (1,)), ((), ())), preferred_element_type=F32)


def _sigmoid(x):
    return 1.0 / (1.0 + jnp.exp(-x))


def _rms(x, g):
    return x * lax.rsqrt(jnp.mean(x * x, axis=-1, keepdims=True) + EPS) * g


def _params(sem, limit=VMEM_LIMIT):
    return pltpu.CompilerParams(dimension_semantics=sem, vmem_limit_bytes=limit)


FFN_TM = 512
FFN_TF = 1408


def _ffn_kernel(x_ref, g_ref, wg_ref, wu_ref, wd_ref, fg_ref, o_ref, h_ref, acc_ref, *, final_norm):
    f = pl.program_id(1)

    @pl.when(f == 0)
    def _():
        h_ref[...] = _rms(x_ref[...], g_ref[...]).astype(BF16)
        acc_ref[...] = jnp.zeros_like(acc_ref)

    h = h_ref[...]
    gate = _dot(h, wg_ref[...])
    up = _dot(h, wu_ref[...])
    a = (gate * _sigmoid(gate) * up).astype(BF16)
    acc_ref[...] += _dot(a, wd_ref[...])

    @pl.when(f == pl.num_programs(1) - 1)
    def _():
        y = x_ref[...] + 0.5 * acc_ref[...]
        if final_norm:
            y = _rms(y, fg_ref[...])
        o_ref[...] = y


def _ffn(x, norm_g, wg, wu, wd, final_g, final_norm):
    s = x.shape[0]
    grid = (s // FFN_TM, D_FF // FFN_TF)
    return pl.pallas_call(
        functools.partial(_ffn_kernel, final_norm=final_norm),
        grid=grid,
        in_specs=[
            pl.BlockSpec((FFN_TM, D_MODEL), lambda i, f: (i, 0)),
            pl.BlockSpec((1, D_MODEL), lambda i, f: (0, 0)),
            pl.BlockSpec((D_MODEL, FFN_TF), lambda i, f: (0, f)),
            pl.BlockSpec((D_MODEL, FFN_TF), lambda i, f: (0, f)),
            pl.BlockSpec((FFN_TF, D_MODEL), lambda i, f: (f, 0)),
            pl.BlockSpec((1, D_MODEL), lambda i, f: (0, 0)),
        ],
        out_specs=pl.BlockSpec((FFN_TM, D_MODEL), lambda i, f: (i, 0)),
        out_shape=jax.ShapeDtypeStruct((s, D_MODEL), F32),
        scratch_shapes=[pltpu.VMEM((FFN_TM, D_MODEL), BF16), pltpu.VMEM((FFN_TM, D_MODEL), F32)],
        compiler_params=_params(("parallel", "arbitrary")),
        name="ffn_final" if final_norm else "ffn",
    )(x, norm_g, wg, wu, wd, final_g)


PROJ_TM = 1024
PROJ_TN = 640


def _proj_kernel(x_ref, g_ref, w_ref, o_ref, h_ref):
    @pl.when(pl.program_id(1) == 0)
    def _():
        h_ref[...] = _rms(x_ref[...], g_ref[...]).astype(BF16)

    o_ref[...] = _dot(h_ref[...], w_ref[...]).astype(BF16)


def _proj(x, norm_g, w):
    s = x.shape[0]
    grid = (s // PROJ_TM, P_COLS // PROJ_TN)
    return pl.pallas_call(
        _proj_kernel,
        grid=grid,
        in_specs=[
            pl.BlockSpec((PROJ_TM, D_MODEL), lambda i, j: (i, 0)),
            pl.BlockSpec((1, D_MODEL), lambda i, j: (0, 0)),
            pl.BlockSpec((D_MODEL, PROJ_TN), lambda i, j: (0, j)),
        ],
        out_specs=pl.BlockSpec((PROJ_TM, PROJ_TN), lambda i, j: (i, j)),
        out_shape=jax.ShapeDtypeStruct((s, P_COLS), BF16),
        scratch_shapes=[pltpu.VMEM((PROJ_TM, D_MODEL), BF16)],
        compiler_params=_params(("parallel", "arbitrary")),
        name="proj",
    )(x, norm_g, w)


def _proj_weight(w_in):
    q, kc, vc, ks, vs, kw, vw, g3, cb, cc, cx, ga, gb = jnp.split(
        w_in, np.cumsum([1024, 128, 128, 128, 128, 128, 128, 48, 1024, 1024, 1024, 1024]).tolist(), axis=1)
    g3 = jnp.pad(g3, ((0, 0), (0, LANES - g3.shape[1])))
    w = jnp.concatenate([q * (HEAD_DIM ** -0.5 * LOG2E), cb, cc, cx, ga, gb, kc, vc, ks, vs, kw, vw, g3], axis=1)
    return w.astype(BF16)


def _gelu_tanh(x):
    return 0.5 * x * (1.0 + jnp.tanh(np.sqrt(2.0 / np.pi).astype(np.float32) * (x + 0.044715 * (x * x * x))))


def _compress_kernel(cf_ref, pe_ref, w1_ref, w2_ref, o_ref, *, transposed):
    cf = cf_ref[...].astype(F32)
    top = (cf + pe_ref[0]).astype(BF16)
    bot = (cf + pe_ref[1]).astype(BF16)
    a = _dot(top, w1_ref[0, 0])
    b = _dot(bot, w1_ref[0, 1])
    n = a.shape[0]
    pre = a + pltpu.roll(b, n - 1, 0)
    h = _gelu_tanh(pre).astype(BF16)
    if transposed:
        o_ref[0] = _dot_nt(w2_ref[...], h).astype(BF16)
    else:
        o_ref[0, 0] = _dot(h, w2_ref[0]).astype(BF16)
        o_ref[0, 1] = _dot(h, w2_ref[1]).astype(BF16)


def _compress(cf, pe, w1, w2, transposed):
    n = cf.shape[0]
    width = CMP_STRIDE * LANES
    if transposed:
        w2_spec = pl.BlockSpec((HEAD_DIM, CMP_HIDDEN), lambda g: (0, 0))
        out_spec = pl.BlockSpec((1, HEAD_DIM, n), lambda g: (g, 0, 0))
        out_shape = jax.ShapeDtypeStruct((N_GROUPS, HEAD_DIM, n), BF16)
    else:
        w2_spec = pl.BlockSpec((2, CMP_HIDDEN, LANES), lambda g: (0, 0, 0))
        out_spec = pl.BlockSpec((1, 2, n, LANES), lambda g: (g, 0, 0, 0))
        out_shape = jax.ShapeDtypeStruct((N_GROUPS, 2, n, LANES), BF16)
    return pl.pallas_call(
        functools.partial(_compress_kernel, transposed=transposed),
        grid=(N_GROUPS,),
        in_specs=[
            pl.BlockSpec((n, width), lambda g: (0, 0)),
            pl.BlockSpec((2, 1, width), lambda g: (0, 0, 0)),
            pl.BlockSpec((1, 2, width, CMP_HIDDEN), lambda g: (g, 0, 0, 0)),
            w2_spec,
        ],
        out_specs=out_spec,
        out_shape=out_shape,
        compiler_params=_params(("parallel",)),
        name="compress_v" if transposed else "compress_k",
    )(cf, pe, w1, w2)


def _compress_weights(pe, w1, w2, transposed):
    w1h = w1.reshape(2, CMP_STRIDE, HEAD_DIM, CMP_HIDDEN)
    zeros = jnp.zeros_like(w1h)
    w1g = jnp.stack([jnp.concatenate([w1h, zeros], axis=2), jnp.concatenate([zeros, w1h], axis=2)])
    w1g = w1g.reshape(N_GROUPS, 2, CMP_STRIDE * LANES, CMP_HIDDEN).astype(BF16)
    peh = pe.reshape(2, CMP_STRIDE, HEAD_DIM)
    pe2 = jnp.concatenate([peh, peh], axis=2).reshape(2, 1, CMP_STRIDE * LANES)
    if transposed:
        w2p = w2.T.astype(BF16)
    else:
        z2 = jnp.zeros_like(w2)
        w2p = jnp.stack([jnp.concatenate([w2, z2], axis=1), jnp.concatenate([z2, w2], axis=1)]).astype(BF16)
    return pe2, w1g, w2p


def _split_heads(kt):
    lo = lax.broadcasted_iota(jnp.int32, kt.shape, 1) < HEAD_DIM
    kr = pltpu.roll(kt, HEAD_DIM, 1)
    zero = jnp.zeros_like(kt)
    return [[jnp.where(lo, kt, zero), jnp.where(lo, zero, kr)],
            [jnp.where(lo, kr, zero), jnp.where(lo, zero, kt)]]


CMP_TQ = 256


def _cmp_kernel(q_ref, kc_ref, vct_ref, ovt_ref, g3t_ref, o_ref, sel_ref, *, n_slc):
    tq = q_ref.shape[0]
    n_cmp = kc_ref.shape[2]
    q0 = pl.program_id(0) * tq
    t = q0 + lax.broadcasted_iota(jnp.int32, (1, tq), 1)
    c_end = lax.broadcasted_iota(jnp.int32, (n_cmp, 1), 0) * CMP_STRIDE + (CMP_BLOCK - 1)
    valid = c_end <= t
    gates = _sigmoid(g3t_ref[...].astype(F32))

    def scores(h):
        p, e = divmod(h, 2)
        g = p // PAIRS_PER_GROUP
        return jnp.where(valid, _dot_nt(kc_ref[g, e], q_ref[:, p * LANES:(p + 1) * LANES]), NEG)

    pcsum = [None] * N_GROUPS
    pending = [scores(h) for h in range(SCORE_LOOKAHEAD)]
    for h in range(N_HEADS):
        g = h // HEADS_PER_GROUP
        if h + SCORE_LOOKAHEAD < N_HEADS:
            pending.append(scores(h + SCORE_LOOKAHEAD))
        s = pending.pop(0)
        m = jnp.max(s, axis=0, keepdims=True)
        pr = jnp.where(valid, jnp.exp2(s - m), 0.0)
        l = jnp.sum(pr, axis=0, keepdims=True)
        pc = pr * (1.0 / jnp.where(l > 0.0, l, 1.0))
        pcsum[g] = pc if pcsum[g] is None else pcsum[g] + pc
        o_h = _dot(vct_ref[g], pc.astype(BF16)) * gates[3 * h:3 * h + 1, :]
        o_ref[h * HEAD_DIM:(h + 1) * HEAD_DIM, :] = o_h.astype(BF16)

    blk = lax.broadcasted_iota(jnp.int32, (n_slc, tq), 0)
    blk_f = blk.astype(F32)
    cur = t // SLC_BLOCK
    forced = (blk == 0) | (blk == cur) | (blk == cur - 1)
    causal = blk * SLC_BLOCK <= t
    ovt = ovt_ref[...]
    for g in range(N_GROUPS):
        hi = pcsum[g].astype(BF16)
        lo_part = (pcsum[g] - hi.astype(F32)).astype(BF16)
        ps = _dot(ovt, hi) + _dot(ovt, lo_part)
        work = jnp.where(forced, FORCE_SCORE, jnp.where(causal, ps, -1.0))
        sel = jnp.zeros((n_slc, tq), F32)
        for _ in range(N_SELECT):
            m = jnp.max(work, axis=0, keepdims=True)
            first = jnp.min(jnp.where(work == m, blk_f, float(n_slc)), axis=0, keepdims=True)
            hit = blk_f == first
            sel = jnp.where(hit, 1.0, sel)
            work = jnp.where(hit, -jnp.inf, work)
        sel_ref[g * n_slc:(g + 1) * n_slc, :] = sel.astype(BF16)


def _cmp_attention(p_arr, kc, vct, overlap_t, g3t):
    s = p_arr.shape[0]
    n_cmp = kc.shape[2]
    n_slc = s // SLC_BLOCK
    return pl.pallas_call(
        functools.partial(_cmp_kernel, n_slc=n_slc),
        grid=(s // CMP_TQ,),
        in_specs=[
            pl.BlockSpec((CMP_TQ, D_MODEL), lambda i: (i, P_Q // 8)),
            pl.BlockSpec((N_GROUPS, 2, n_cmp, LANES), lambda i: (0, 0, 0, 0)),
            pl.BlockSpec((N_GROUPS, HEAD_DIM, n_cmp), lambda i: (0, 0, 0)),
            pl.BlockSpec((n_slc, n_cmp), lambda i: (0, 0)),
            pl.BlockSpec((LANES, CMP_TQ), lambda i: (0, i)),
        ],
        out_specs=[
            pl.BlockSpec((D_MODEL, CMP_TQ), lambda i: (0, i)),
            pl.BlockSpec((N_GROUPS * n_slc, CMP_TQ), lambda i: (0, i)),
        ],
        out_shape=[
            jax.ShapeDtypeStruct((D_MODEL, s), BF16),
            jax.ShapeDtypeStruct((N_GROUPS * n_slc, s), BF16),
        ],
        compiler_params=_params(("parallel",)),
        name="cmp_attention",
    )(p_arr, kc, vct, overlap_t, g3t)


def _overlap_matrix_t(n_cmp, n_slc):
    c0 = np.arange(n_cmp)[None, :] * CMP_STRIDE
    s0 = np.arange(n_slc)[:, None] * SLC_BLOCK
    ov = (c0 < s0 + SLC_BLOCK) & (c0 + CMP_BLOCK > s0)
    return jnp.asarray(ov, dtype=BF16)


FLASH_TQ = 256
SLC_TK = 512
WIN_TK = 256
SCORE_LOOKAHEAD = 2
ACC_ROWS = HEAD_DIM + 16


def _flash_kernel(qi_ref, ki_ref, fl_ref, *refs, mode, n_slc, branch):
    if mode == "slc":
        q_ref, k_ref, vt_ref, g3t_ref, sel_ref, o_ref, m_ref, acc_ref = refs
    else:
        q_ref, k_ref, vt_ref, g3t_ref, o_ref, m_ref, acc_ref = refs
    step = pl.program_id(0)
    tq = q_ref.shape[0]
    tk = k_ref.shape[0]
    flags = fl_ref[step]

    @pl.when((flags & 1) != 0)
    def _():
        m_ref[...] = jnp.full_like(m_ref, NEG)
        acc_ref[...] = jnp.zeros_like(acc_ref)

    q0 = qi_ref[step] * tq
    k0 = ki_ref[step] * tk
    t = q0 + lax.broadcasted_iota(jnp.int32, (1, tq), 1)
    key = k0 + lax.broadcasted_iota(jnp.int32, (tk, 1), 0)

    if mode == "slc":
        key_blk = (k0 + lax.broadcasted_iota(jnp.int32, (tk, n_slc), 0)) // SLC_BLOCK
        blk = lax.broadcasted_iota(jnp.int32, (tk, n_slc), 1)
        expand = (blk == key_blk).astype(BF16)
        causal = key <= t
        bias = []
        for g in range(N_GROUPS):
            member = _dot(expand, sel_ref[g * n_slc:(g + 1) * n_slc, :])
            bias.append(jnp.where((member > 0.5) & causal, 0.0, NEG))
    else:
        d = t - key
        band = jnp.where((d >= 0) & (d < WINDOW), 0.0, NEG)
        bias = [band] * N_GROUPS

    k_half = _split_heads(k_ref[...])
    vt = vt_ref[...]
    ones = jnp.ones((ACC_ROWS - HEAD_DIM, tk), BF16)
    v_aug = [jnp.concatenate([vt[g * HEAD_DIM:(g + 1) * HEAD_DIM, :], ones], axis=0) for g in range(N_GROUPS)]

    def scores(h):
        p, e = divmod(h, 2)
        g = p // PAIRS_PER_GROUP
        return _dot_nt(k_half[g][e], q_ref[:, p * LANES:(p + 1) * LANES]) + bias[g]

    pending = [scores(h) for h in range(SCORE_LOOKAHEAD)]
    for h in range(N_HEADS):
        g = h // HEADS_PER_GROUP
        if h + SCORE_LOOKAHEAD < N_HEADS:
            pending.append(scores(h + SCORE_LOOKAHEAD))
        s = pending.pop(0)
        m_prev = m_ref[h:h + 1, :]
        m_new = jnp.maximum(m_prev, jnp.max(s, axis=0, keepdims=True))
        alpha = jnp.exp2(m_prev - m_new)
        pr = jnp.exp2(s - m_new)
        m_ref[h:h + 1, :] = m_new
        acc_ref[h] = acc_ref[h] * alpha + _dot(v_aug[g], pr.astype(BF16))

    @pl.when((flags & 2) != 0)
    def _():
        gates = _sigmoid(g3t_ref[...].astype(F32))
        for h in range(N_HEADS):
            acc = acc_ref[h]
            scale = gates[3 * h + branch:3 * h + branch + 1, :] / acc[HEAD_DIM:HEAD_DIM + 1, :]
            o_ref[h * HEAD_DIM:(h + 1) * HEAD_DIM, :] = (acc[:HEAD_DIM, :] * scale).astype(BF16)


def _flash_tables(s, tq, tk, mode):
    qi, ki, fl = [], [], []
    for i in range(s // tq):
        q_lo, q_hi = i * tq, i * tq + tq - 1
        first = 0 if mode == "slc" else max(0, (q_lo - (WINDOW - 1)) // tk)
        last = q_hi // tk
        for k in range(first, last + 1):
            qi.append(i)
            ki.append(k)
            fl.append((1 if k == first else 0) | (2 if k == last else 0))
    return (jnp.asarray(qi, jnp.int32), jnp.asarray(ki, jnp.int32), jnp.asarray(fl, jnp.int32))


def _flash(p_arr, vt, g3t, sel_t, mode):
    s = p_arr.shape[0]
    n_slc = s // SLC_BLOCK
    tq = FLASH_TQ
    tk = SLC_TK if mode == "slc" else WIN_TK
    k_col, branch = (P_KS, 1) if mode == "slc" else (P_KW, 2)
    qi, ki, fl = _flash_tables(s, tq, tk, mode)
    in_specs = [
        pl.BlockSpec((tq, D_MODEL), lambda n, qi, ki, fl: (qi[n], P_Q // 8)),
        pl.BlockSpec((tk, LANES), lambda n, qi, ki, fl: (ki[n], k_col)),
        pl.BlockSpec((LANES, tk), lambda n, qi, ki, fl: (0, ki[n])),
        pl.BlockSpec((LANES, tq), lambda n, qi, ki, fl: (0, qi[n])),
    ]
    args = [p_arr, p_arr, vt, g3t]
    if mode == "slc":
        in_specs.append(pl.BlockSpec((N_GROUPS * n_slc, tq), lambda n, qi, ki, fl: (0, qi[n])))
        args.append(sel_t)
    grid_spec = pltpu.PrefetchScalarGridSpec(
        num_scalar_prefetch=3,
        grid=(qi.shape[0],),
        in_specs=in_specs,
        out_specs=pl.BlockSpec((D_MODEL, tq), lambda n, qi, ki, fl: (0, qi[n])),
        scratch_shapes=[
            pltpu.VMEM((N_HEADS, tq), F32),
            pltpu.VMEM((N_HEADS, ACC_ROWS, tq), F32),
        ],
    )
    return pl.pallas_call(
        functools.partial(_flash_kernel, mode=mode, n_slc=n_slc, branch=branch),
        grid_spec=grid_spec,
        out_shape=jax.ShapeDtypeStruct((D_MODEL, s), BF16),
        compiler_params=_params(("arbitrary",)),
        name="flash_" + mode,
    )(qi, ki, fl, *args)


OUT_TM = 512
PREV_ROWS = 16


def _out_kernel(x_ref, oc_ref, os_ref, ow_ref, cb_ref, cc_ref, cx_ref, pc_ref, px_ref, ga_ref, gb_ref,
                cw_ref, wa_ref, wb_ref, wo_ref, o_ref, u_ref):
    tm = x_ref.shape[0]
    o_nsa_t = oc_ref[...].astype(F32) + os_ref[...].astype(F32) + ow_ref[...].astype(F32)
    y_a = _dot(o_nsa_t.T.astype(BF16), wa_ref[...])

    prev = pc_ref[...].astype(F32) * px_ref[...].astype(F32)
    u_ref[0:PREV_ROWS, :] = jnp.where(pl.program_id(0) == 0, 0.0, prev)
    u_ref[PREV_ROWS:, :] = cc_ref[...].astype(F32) * cx_ref[...].astype(F32)
    cw = cw_ref[...]
    conv = (cw[0:1, :] * u_ref[pl.ds(PREV_ROWS - 2, tm), :]
            + cw[1:2, :] * u_ref[pl.ds(PREV_ROWS - 1, tm), :]
            + cw[2:3, :] * u_ref[pl.ds(PREV_ROWS, tm), :])
    y_b = _dot((cb_ref[...].astype(F32) * conv).astype(BF16), wb_ref[...])

    merged = _sigmoid(ga_ref[...].astype(F32)) * y_a + _sigmoid(gb_ref[...].astype(F32)) * y_b
    o_ref[...] = x_ref[...] + _dot(merged.astype(BF16), wo_ref[...])


def _out(x1, o_cmp_t, o_slc_t, o_win_t, p_arr, conv_w, w_nsa_out, w_conv_out, w_out):
    s = x1.shape[0]
    tm = OUT_TM
    row = lambda i: (i, 0)
    col = lambda i: (0, i)
    prev_blocks = tm // PREV_ROWS

    def seg(c):
        return pl.BlockSpec((tm, D_MODEL), lambda i: (i, c // 8))

    def seg_prev(c):
        return pl.BlockSpec((PREV_ROWS, D_MODEL), lambda i: (jnp.maximum(i * prev_blocks - 1, 0), c // 8))

    full = lambda shape: pl.BlockSpec(shape, lambda i: (0, 0))
    return pl.pallas_call(
        _out_kernel,
        grid=(s // tm,),
        in_specs=[
            pl.BlockSpec((tm, D_MODEL), row), pl.BlockSpec((D_MODEL, tm), col),
            pl.BlockSpec((D_MODEL, tm), col), pl.BlockSpec((D_MODEL, tm), col),
            seg(P_CB), seg(P_CC), seg(P_CX), seg_prev(P_CC), seg_prev(P_CX), seg(P_GA), seg(P_GB),
            full((CONV_K, CONV_WIDTH)), full((D_MODEL, D_MODEL)), full((CONV_WIDTH, D_MODEL)),
            full((D_MODEL, D_MODEL)),
        ],
        out_specs=pl.BlockSpec((tm, D_MODEL), row),
        out_shape=jax.ShapeDtypeStruct((s, D_MODEL), F32),
        scratch_shapes=[pltpu.VMEM((tm + PREV_ROWS, CONV_WIDTH), F32)],
        compiler_params=_params(("parallel",)),
        name="merge_out",
    )(x1, o_cmp_t, o_slc_t, o_win_t, p_arr, p_arr, p_arr, p_arr, p_arr, p_arr, p_arr,
      conv_w, w_nsa_out, w_conv_out, w_out)


def _layer(x, ffn1_norm, ffn1_w_gate, ffn1_w_up, ffn1_w_down, mix_norm, w_in, cmp_pe_k, cmp_pe_v,
           cmp_k_w1, cmp_k_w2, cmp_v_w1, cmp_v_w2, conv_w, w_nsa_out, w_conv_out, w_out,
           ffn2_norm, ffn2_w_gate, ffn2_w_up, ffn2_w_down, final_g, final_norm):
    s = x.shape[0]
    n_cmp = s // CMP_STRIDE
    n_slc = s // SLC_BLOCK
    row = lambda v: v.reshape(1, -1).astype(F32)

    x1 = _ffn(x, row(ffn1_norm), ffn1_w_gate.astype(BF16), ffn1_w_up.astype(BF16),
              ffn1_w_down.astype(BF16), row(final_g), False)
    p_arr = _proj(x1, row(mix_norm), _proj_weight(w_in))

    col = lambda c: p_arr[:, c * LANES:(c + 1) * LANES]
    chunks = lambda c: col(c).reshape(n_cmp, CMP_STRIDE * LANES)
    kc = _compress(chunks(P_KC), *_compress_weights(cmp_pe_k, cmp_k_w1, cmp_k_w2, False), False)
    vct = _compress(chunks(P_VC), *_compress_weights(cmp_pe_v, cmp_v_w1, cmp_v_w2, True), True)

    g3t = col(P_G3).T
    o_cmp_t, sel_t = _cmp_attention(p_arr, kc, vct, _overlap_matrix_t(n_cmp, n_slc), g3t)
    o_slc_t = _flash(p_arr, col(P_VS).T, g3t, sel_t, "slc")
    o_win_t = _flash(p_arr, col(P_VW).T, g3t, None, "win")
    x2 = _out(x1, o_cmp_t, o_slc_t, o_win_t, p_arr, conv_w.astype(F32), w_nsa_out.astype(BF16),
              w_conv_out.astype(BF16), w_out.astype(BF16))
    return _ffn(x2, row(ffn2_norm), ffn2_w_gate.astype(BF16), ffn2_w_up.astype(BF16),
                ffn2_w_down.astype(BF16), row(final_g), final_norm)


def kernel(x, ffn1_norm, ffn1_w_gate, ffn1_w_up, ffn1_w_down, mix_norm, w_in, cmp_pe_k, cmp_pe_v,
           cmp_k_w1, cmp_k_w2, cmp_v_w1, cmp_v_w2, conv_w, w_nsa_out, w_conv_out, w_out, ffn2_norm,
           ffn2_w_gate, ffn2_w_up, ffn2_w_down, final_norm):
    batch, _, _ = x.shape
    depth = ffn1_norm.shape[0]
    outs = []
    for b in range(batch):
        xb = x[b]
        for l in range(depth):
            xb = _layer(xb, ffn1_norm[l], ffn1_w_gate[l], ffn1_w_up[l], ffn1_w_down[l], mix_norm[l],
                        w_in[l], cmp_pe_k[l], cmp_pe_v[l], cmp_k_w1[l], cmp_k_w2[l], cmp_v_w1[l],
                        cmp_v_w2[l], conv_w[l], w_nsa_out[l], w_conv_out[l], w_out[l], ffn2_norm[l],
                        ffn2_w_gate[l], ffn2_w_up[l], ffn2_w_down[l], final_norm, l == depth - 1)
        outs.append(xb)
    return jnp.stack(outs)
```

```python
import functools

import jax
import jax.numpy as jnp
import numpy as np
from jax import lax
from jax.experimental import pallas as pl
from jax.experimental.pallas import tpu as pltpu

D_MODEL = 1024
N_HEADS = 16
HEAD_DIM = 64
N_GROUPS = 2
HEADS_PER_GROUP = N_HEADS // N_GROUPS
N_PAIRS = N_HEADS // 2
PAIRS_PER_GROUP = N_PAIRS // N_GROUPS
CMP_BLOCK = 32
CMP_STRIDE = 16
CMP_HIDDEN = 256
SLC_BLOCK = 64
N_SELECT = 16
WINDOW = 512
CONV_WIDTH = 1024
CONV_K = 3
D_FF = 2816
EPS = 1e-6
NEG = -1e30
N_FORCED = 3
LOG2E = float(np.log2(np.e))
LANES = 128
BF16_ROWS = 16

P_Q, P_CB, P_CC, P_CX, P_GA, P_GB = 0, 8, 16, 24, 32, 40
P_KC, P_VC, P_KS, P_VS, P_KW, P_VW, P_G3 = 48, 49, 50, 51, 52, 53, 54
P_BLOCKS = 55
P_COLS = P_BLOCKS * LANES

F32 = jnp.float32
BF16 = jnp.bfloat16

VMEM_LIMIT = 52 * 1024 * 1024


def _dot(a, b):
    return jnp.dot(a, b, preferred_element_type=F32)


def _dot_nt(a, b):
    return lax.dot_general(a, b, (((1,), (1,)), ((), ())), preferred_element_type=F32)


def _sigmoid(x):
    return 1.0 / (1.0 + jnp.exp(-x))


def _rms(x, g):
    return x * lax.rsqrt(jnp.mean(x * x, axis=-1, keepdims=True) + EPS) * g


def _params(sem, limit=VMEM_LIMIT, flags=None):
    return pltpu.CompilerParams(dimension_semantics=sem, vmem_limit_bytes=limit, flags=flags)


FFN_TM = 512
FFN_TF = 1408


def _ffn_kernel(x_ref, g_ref, wg_ref, wu_ref, wd_ref, fg_ref, o_ref, h_ref, acc_ref, *, final_norm):
    f = pl.program_id(1)

    @pl.when(f == 0)
    def _():
        h_ref[...] = _rms(x_ref[...], g_ref[...]).astype(BF16)
        acc_ref[...] = jnp.zeros_like(acc_ref)

    h = h_ref[...]
    gate = _dot(h, wg_ref[...])
    up = _dot(h, wu_ref[...])
    a = (gate * _sigmoid(gate) * up).astype(BF16)
    acc_ref[...] += _dot(a, wd_ref[...])

    @pl.when(f == pl.num_programs(1) - 1)
    def _():
        y = x_ref[...] + 0.5 * acc_ref[...]
        if final_norm:
            y = _rms(y, fg_ref[...])
        o_ref[...] = y


def _ffn(x, norm_g, wg, wu, wd, final_g, final_norm):
    s = x.shape[0]
    grid = (s // FFN_TM, D_FF // FFN_TF)
    return pl.pallas_call(
        functools.partial(_ffn_kernel, final_norm=final_norm),
        grid=grid,
        in_specs=[
            pl.BlockSpec((FFN_TM, D_MODEL), lambda i, f: (i, 0)),
            pl.BlockSpec((1, D_MODEL), lambda i, f: (0, 0)),
            pl.BlockSpec((D_MODEL, FFN_TF), lambda i, f: (0, f)),
            pl.BlockSpec((D_MODEL, FFN_TF), lambda i, f: (0, f)),
            pl.BlockSpec((FFN_TF, D_MODEL), lambda i, f: (f, 0)),
            pl.BlockSpec((1, D_MODEL), lambda i, f: (0, 0)),
        ],
        out_specs=pl.BlockSpec((FFN_TM, D_MODEL), lambda i, f: (i, 0)),
        out_shape=jax.ShapeDtypeStruct((s, D_MODEL), F32),
        scratch_shapes=[pltpu.VMEM((FFN_TM, D_MODEL), BF16), pltpu.VMEM((FFN_TM, D_MODEL), F32)],
        compiler_params=_params(("parallel", "arbitrary")),
        name="ffn_final" if final_norm else "ffn",
    )(x, norm_g, wg, wu, wd, final_g)


PROJ_TM = 1024
PROJ_TN = 640


def _proj_kernel(x_ref, g_ref, w_ref, o_ref, h_ref):
    @pl.when(pl.program_id(1) == 0)
    def _():
        h_ref[...] = _rms(x_ref[...], g_ref[...]).astype(BF16)

    o_ref[...] = _dot(h_ref[...], w_ref[...]).astype(BF16)


def _proj(x, norm_g, w):
    s = x.shape[0]
    grid = (s // PROJ_TM, P_COLS // PROJ_TN)
    return pl.pallas_call(
        _proj_kernel,
        grid=grid,
        in_specs=[
            pl.BlockSpec((PROJ_TM, D_MODEL), lambda i, j: (i, 0)),
            pl.BlockSpec((1, D_MODEL), lambda i, j: (0, 0)),
            pl.BlockSpec((D_MODEL, PROJ_TN), lambda i, j: (0, j)),
        ],
        out_specs=pl.BlockSpec((PROJ_TM, PROJ_TN), lambda i, j: (i, j)),
        out_shape=jax.ShapeDtypeStruct((s, P_COLS), BF16),
        scratch_shapes=[pltpu.VMEM((PROJ_TM, D_MODEL), BF16)],
        compiler_params=_params(("parallel", "arbitrary")),
        name="proj",
    )(x, norm_g, w)


def _proj_weight(w_in):
    q, kc, vc, ks, vs, kw, vw, g3, cb, cc, cx, ga, gb = jnp.split(
        w_in, np.cumsum([1024, 128, 128, 128, 128, 128, 128, 48, 1024, 1024, 1024, 1024]).tolist(), axis=1)
    g3 = jnp.pad(g3, ((0, 0), (0, LANES - g3.shape[1])))
    w = jnp.concatenate([q * (HEAD_DIM ** -0.5 * LOG2E), cb, cc, cx, ga, gb, kc, vc, ks, vs, kw, vw, g3], axis=1)
    return w.astype(BF16)


def _gelu_tanh(x):
    return 0.5 * x * (1.0 + jnp.tanh(np.sqrt(2.0 / np.pi).astype(np.float32) * (x + 0.044715 * (x * x * x))))


def _compress_kernel(cf_ref, pe_ref, w1_ref, w2_ref, o_ref, *, transposed):
    cf = cf_ref[...].astype(F32)
    top = (cf + pe_ref[0]).astype(BF16)
    bot = (cf + pe_ref[1]).astype(BF16)
    a = _dot(top, w1_ref[0, 0])
    b = _dot(bot, w1_ref[0, 1])
    n = a.shape[0]
    pre = a + pltpu.roll(b, n - 1, 0)
    h = _gelu_tanh(pre).astype(BF16)
    if transposed:
        o_ref[0] = _dot_nt(w2_ref[...], h).astype(BF16)
    else:
        o_ref[0, 0] = _dot(h, w2_ref[0]).astype(BF16)
        o_ref[0, 1] = _dot(h, w2_ref[1]).astype(BF16)


def _compress(cf, pe, w1, w2, transposed):
    n = cf.shape[0]
    width = CMP_STRIDE * LANES
    if transposed:
        w2_spec = pl.BlockSpec((HEAD_DIM, CMP_HIDDEN), lambda g: (0, 0))
        out_spec = pl.BlockSpec((1, HEAD_DIM, n), lambda g: (g, 0, 0))
        out_shape = jax.ShapeDtypeStruct((N_GROUPS, HEAD_DIM, n), BF16)
    else:
        w2_spec = pl.BlockSpec((2, CMP_HIDDEN, LANES), lambda g: (0, 0, 0))
        out_spec = pl.BlockSpec((1, 2, n, LANES), lambda g: (g, 0, 0, 0))
        out_shape = jax.ShapeDtypeStruct((N_GROUPS, 2, n, LANES), BF16)
    return pl.pallas_call(
        functools.partial(_compress_kernel, transposed=transposed),
        grid=(N_GROUPS,),
        in_specs=[
            pl.BlockSpec((n, width), lambda g: (0, 0)),
            pl.BlockSpec((2, 1, width), lambda g: (0, 0, 0)),
            pl.BlockSpec((1, 2, width, CMP_HIDDEN), lambda g: (g, 0, 0, 0)),
            w2_spec,
        ],
        out_specs=out_spec,
        out_shape=out_shape,
        compiler_params=_params(("parallel",)),
        name="compress_v" if transposed else "compress_k",
    )(cf, pe, w1, w2)


def _compress_weights(pe, w1, w2, transposed):
    w1h = w1.reshape(2, CMP_STRIDE, HEAD_DIM, CMP_HIDDEN)
    zeros = jnp.zeros_like(w1h)
    w1g = jnp.stack([jnp.concatenate([w1h, zeros], axis=2), jnp.concatenate([zeros, w1h], axis=2)])
    w1g = w1g.reshape(N_GROUPS, 2, CMP_STRIDE * LANES, CMP_HIDDEN).astype(BF16)
    peh = pe.reshape(2, CMP_STRIDE, HEAD_DIM)
    pe2 = jnp.concatenate([peh, peh], axis=2).reshape(2, 1, CMP_STRIDE * LANES)
    if transposed:
        w2p = w2.T.astype(BF16)
    else:
        z2 = jnp.zeros_like(w2)
        w2p = jnp.stack([jnp.concatenate([w2, z2], axis=1), jnp.concatenate([z2, w2], axis=1)]).astype(BF16)
    return pe2, w1g, w2p


def _split_heads(kt):
    lo = lax.broadcasted_iota(jnp.int32, kt.shape, 1) < HEAD_DIM
    kr = pltpu.roll(kt, HEAD_DIM, 1)
    zero = jnp.zeros_like(kt)
    return [[jnp.where(lo, kt, zero), jnp.where(lo, zero, kr)],
            [jnp.where(lo, kr, zero), jnp.where(lo, zero, kt)]]


CMP_TQ = 256
CMP_KEY_CHUNK = 256


def _cmp_kernel(q_ref, kc_ref, vct_ref, ovt_ref, g3t_ref, o_ref, sel_ref, *, n_slc):
    tq = q_ref.shape[0]
    n_cmp = kc_ref.shape[2]
    q0 = pl.program_id(0) * tq
    t = q0 + lax.broadcasted_iota(jnp.int32, (1, tq), 1)
    any_valid = t >= CMP_BLOCK - 1
    cur = t // SLC_BLOCK

    def body(n_keys, n_blk):
        c_end = lax.broadcasted_iota(jnp.int32, (n_keys, 1), 0) * CMP_STRIDE + (CMP_BLOCK - 1)
        bias = jnp.where(c_end <= t, 0.0, NEG)
        gates = _sigmoid(g3t_ref[...].astype(F32))

        def scores(h):
            p, e = divmod(h, 2)
            g = p // PAIRS_PER_GROUP
            s = _dot_nt(kc_ref[g, e, :n_keys, :], q_ref[:, p * LANES:(p + 1) * LANES]) + bias
            return s, jnp.max(s, axis=0, keepdims=True)

        pcsum = [None] * N_GROUPS
        pending = [scores(h) for h in range(SCORE_LOOKAHEAD)]
        for h in range(N_HEADS):
            g = h // HEADS_PER_GROUP
            if h + SCORE_LOOKAHEAD < N_HEADS:
                pending.append(scores(h + SCORE_LOOKAHEAD))
            s, m = pending.pop(0)
            pr = jnp.exp2(s - m)
            l = jnp.sum(pr, axis=0, keepdims=True)
            pc = pr * jnp.where(any_valid, 1.0 / l, 0.0)
            pcsum[g] = pc if pcsum[g] is None else pcsum[g] + pc
            o_h = _dot(vct_ref[g, :, :n_keys], pc.astype(BF16)) * gates[3 * h:3 * h + 1, :]
            o_ref[h * HEAD_DIM:(h + 1) * HEAD_DIM, :] = o_h.astype(BF16)

        blk = lax.broadcasted_iota(jnp.int32, (n_blk, tq), 0)
        blk_f = blk.astype(F32)
        forced = (blk == 0) | (blk == cur) | (blk == cur - 1)
        causal = blk * SLC_BLOCK <= t
        ovt = ovt_ref[:n_blk, :n_keys]
        for g in range(N_GROUPS):
            hi = pcsum[g].astype(BF16)
            lo_part = (pcsum[g] - hi.astype(F32)).astype(BF16)
            ps = _dot(ovt, hi) + _dot(ovt, lo_part)
            work = jnp.where(causal & ~forced, ps, -1.0)
            sel = jnp.where(forced, 1.0, 0.0)
            for _ in range(N_SELECT - N_FORCED):
                m = jnp.max(work, axis=0, keepdims=True)
                first = jnp.min(jnp.where(work == m, blk_f, float(n_blk)), axis=0, keepdims=True)
                hit = blk_f == first
                sel = jnp.where(hit, 1.0, sel)
                work = jnp.where(hit, -jnp.inf, work)
            sel_ref[g * n_slc:g * n_slc + n_blk, :] = jnp.where(sel > 0.0, 0.0, NEG)
            if n_blk < n_slc:
                sel_ref[g * n_slc + n_blk:(g + 1) * n_slc, :] = jnp.full((n_slc - n_blk, tq), NEG, F32)

    chunk_tokens = CMP_KEY_CHUNK * CMP_STRIDE
    variant = (q0 + tq - 1) // chunk_tokens
    for v in range(n_cmp // CMP_KEY_CHUNK):
        pl.when(variant == v)(functools.partial(body, (v + 1) * CMP_KEY_CHUNK,
                                                (v + 1) * chunk_tokens // SLC_BLOCK))


def _cmp_attention(p_arr, kc, vct, overlap_t, g3t):
    s = p_arr.shape[0]
    n_cmp = kc.shape[2]
    n_slc = s // SLC_BLOCK
    return pl.pallas_call(
        functools.partial(_cmp_kernel, n_slc=n_slc),
        grid=(s // CMP_TQ,),
        in_specs=[
            pl.BlockSpec((CMP_TQ, D_MODEL), lambda i: (i, P_Q // 8)),
            pl.BlockSpec((N_GROUPS, 2, n_cmp, LANES), lambda i: (0, 0, 0, 0)),
            pl.BlockSpec((N_GROUPS, HEAD_DIM, n_cmp), lambda i: (0, 0, 0)),
            pl.BlockSpec((n_slc, n_cmp), lambda i: (0, 0)),
            pl.BlockSpec((LANES, CMP_TQ), lambda i: (0, i)),
        ],
        out_specs=[
            pl.BlockSpec((D_MODEL, CMP_TQ), lambda i: (0, i)),
            pl.BlockSpec((N_GROUPS * n_slc, CMP_TQ), lambda i: (0, i)),
        ],
        out_shape=[
            jax.ShapeDtypeStruct((D_MODEL, s), BF16),
            jax.ShapeDtypeStruct((N_GROUPS * n_slc, s), F32),
        ],
        compiler_params=_params(("parallel",)),
        name="cmp_attention",
    )(p_arr, kc, vct, overlap_t, g3t)


def _overlap_matrix_t(n_cmp, n_slc):
    c0 = np.arange(n_cmp)[None, :] * CMP_STRIDE
    s0 = np.arange(n_slc)[:, None] * SLC_BLOCK
    ov = (c0 < s0 + SLC_BLOCK) & (c0 + CMP_BLOCK > s0)
    return jnp.asarray(ov, dtype=BF16)


QSUB = 256
SLC_TQ = 512
SLC_TK = 512
WIN_TQ = 256
WIN_TK = 256
SCORE_LOOKAHEAD = 4
ACC_ROWS = HEAD_DIM + BF16_ROWS


def _flash_kernel(qi_ref, ki_ref, fl_ref, *refs, mode, n_slc, branch):
    if mode == "slc":
        q_ref, k_ref, vt_ref, g3t_ref, sel_ref, o_ref, m_ref, acc_ref = refs
    else:
        q_ref, k_ref, vt_ref, g3t_ref, o_ref, m_ref, acc_ref = refs
    step = pl.program_id(0)
    tq = q_ref.shape[0]
    tk = k_ref.shape[0]
    flags = fl_ref[step]

    @pl.when((flags & 1) != 0)
    def _():
        m_ref[...] = jnp.full_like(m_ref, NEG)
        acc_ref[...] = jnp.zeros_like(acc_ref)

    q0 = qi_ref[step] * tq
    k0 = ki_ref[step] * tk
    t = q0 + lax.broadcasted_iota(jnp.int32, (1, tq), 1)
    key = k0 + lax.broadcasted_iota(jnp.int32, (tk, 1), 0)

    if mode == "slc":
        tile_blocks = tk // SLC_BLOCK
        pad_blocks = BF16_ROWS - tile_blocks % BF16_ROWS
        key_blk = lax.broadcasted_iota(jnp.int32, (tk, tile_blocks + pad_blocks), 0) // SLC_BLOCK
        blk = lax.broadcasted_iota(jnp.int32, (tk, tile_blocks + pad_blocks), 1)
        expand = (blk == key_blk).astype(BF16)
        causal_bias = jnp.where(key <= t, 0.0, NEG)
        bias = []
        for g in range(N_GROUPS):
            first_block = pl.multiple_of(g * n_slc + ki_ref[step] * tile_blocks, tile_blocks)
            rows = jnp.concatenate([sel_ref[pl.ds(first_block, tile_blocks), :],
                                    jnp.zeros((pad_blocks, tq), F32)], axis=0)
            bias.append(jnp.minimum(_dot(expand, rows.astype(BF16)), causal_bias))
    else:
        d = t - key
        band = jnp.where((d >= 0) & (d < WINDOW), 0.0, NEG)
        bias = [band] * N_GROUPS

    k_half = _split_heads(k_ref[...])
    vt = vt_ref[...]
    ones = jnp.ones((ACC_ROWS - HEAD_DIM, tk), BF16)
    v_aug = [jnp.concatenate([vt[g * HEAD_DIM:(g + 1) * HEAD_DIM, :], ones], axis=0) for g in range(N_GROUPS)]

    units = [(h, slice(qs * QSUB, (qs + 1) * QSUB)) for qs in range(tq // QSUB) for h in range(N_HEADS)]

    def scores(unit):
        h, cols = unit
        p, e = divmod(h, 2)
        g = p // PAIRS_PER_GROUP
        s = _dot_nt(k_half[g][e], q_ref[cols, p * LANES:(p + 1) * LANES]) + bias[g][:, cols]
        return s, jnp.max(s, axis=0, keepdims=True)

    pending = [scores(u) for u in units[:SCORE_LOOKAHEAD]]
    for n, (h, cols) in enumerate(units):
        g = h // HEADS_PER_GROUP
        if n + SCORE_LOOKAHEAD < len(units):
            pending.append(scores(units[n + SCORE_LOOKAHEAD]))
        s, s_max = pending.pop(0)
        m_prev = m_ref[h:h + 1, cols]
        m_new = jnp.maximum(m_prev, s_max)
        alpha = jnp.exp2(m_prev - m_new)
        pr = jnp.exp2(s - m_new)
        m_ref[h:h + 1, cols] = m_new
        acc_ref[h, :, cols] = acc_ref[h, :, cols] * alpha + _dot(v_aug[g], pr.astype(BF16))

    @pl.when((flags & 2) != 0)
    def _():
        gates = _sigmoid(g3t_ref[...].astype(F32))
        for h in range(N_HEADS):
            acc = acc_ref[h]
            scale = gates[3 * h + branch:3 * h + branch + 1, :] / acc[HEAD_DIM:HEAD_DIM + 1, :]
            o_ref[h * HEAD_DIM:(h + 1) * HEAD_DIM, :] = (acc[:HEAD_DIM, :] * scale).astype(BF16)


def _flash_tables(s, tq, tk, mode):
    qi, ki, fl = [], [], []
    for i in range(s // tq):
        q_lo, q_hi = i * tq, i * tq + tq - 1
        first = 0 if mode == "slc" else max(0, (q_lo - (WINDOW - 1)) // tk)
        last = q_hi // tk
        for k in range(first, last + 1):
            qi.append(i)
            ki.append(k)
            fl.append((1 if k == first else 0) | (2 if k == last else 0))
    return (jnp.asarray(qi, jnp.int32), jnp.asarray(ki, jnp.int32), jnp.asarray(fl, jnp.int32))


def _flash(p_arr, vt, g3t, sel_t, mode):
    s = p_arr.shape[0]
    n_slc = s // SLC_BLOCK
    tq, tk = (SLC_TQ, SLC_TK) if mode == "slc" else (WIN_TQ, WIN_TK)
    k_col, branch = (P_KS, 1) if mode == "slc" else (P_KW, 2)
    qi, ki, fl = _flash_tables(s, tq, tk, mode)
    in_specs = [
        pl.BlockSpec((tq, D_MODEL), lambda n, qi, ki, fl: (qi[n], P_Q // 8)),
        pl.BlockSpec((tk, LANES), lambda n, qi, ki, fl: (ki[n], k_col)),
        pl.BlockSpec((LANES, tk), lambda n, qi, ki, fl: (0, ki[n])),
        pl.BlockSpec((LANES, tq), lambda n, qi, ki, fl: (0, qi[n])),
    ]
    args = [p_arr, p_arr, vt, g3t]
    if mode == "slc":
        in_specs.append(pl.BlockSpec((N_GROUPS * n_slc, tq), lambda n, qi, ki, fl: (0, qi[n])))
        args.append(sel_t)
    grid_spec = pltpu.PrefetchScalarGridSpec(
        num_scalar_prefetch=3,
        grid=(qi.shape[0],),
        in_specs=in_specs,
        out_specs=pl.BlockSpec((D_MODEL, tq), lambda n, qi, ki, fl: (0, qi[n])),
        scratch_shapes=[
            pltpu.VMEM((N_HEADS, tq), F32),
            pltpu.VMEM((N_HEADS, ACC_ROWS, tq), F32),
        ],
    )
    return pl.pallas_call(
        functools.partial(_flash_kernel, mode=mode, n_slc=n_slc, branch=branch),
        grid_spec=grid_spec,
        out_shape=jax.ShapeDtypeStruct((D_MODEL, s), BF16),
        compiler_params=_params(("arbitrary",)),
        name="flash_" + mode,
    )(qi, ki, fl, *args)


OUT_TM = 512
PREV_ROWS = 16


def _out_kernel(x_ref, oc_ref, os_ref, ow_ref, cb_ref, cc_ref, cx_ref, pc_ref, px_ref, ga_ref, gb_ref,
                cw_ref, wa_ref, wb_ref, wo_ref, o_ref, u_ref):
    tm = x_ref.shape[0]
    o_nsa_t = oc_ref[...].astype(F32) + os_ref[...].astype(F32) + ow_ref[...].astype(F32)
    y_a = _dot(o_nsa_t.T.astype(BF16), wa_ref[...])

    prev = pc_ref[...].astype(F32) * px_ref[...].astype(F32)
    u_ref[0:PREV_ROWS, :] = jnp.where(pl.program_id(0) == 0, 0.0, prev)
    u_ref[PREV_ROWS:, :] = cc_ref[...].astype(F32) * cx_ref[...].astype(F32)
    cw = cw_ref[...]
    conv = (cw[0:1, :] * u_ref[pl.ds(PREV_ROWS - 2, tm), :]
            + cw[1:2, :] * u_ref[pl.ds(PREV_ROWS - 1, tm), :]
            + cw[2:3, :] * u_ref[pl.ds(PREV_ROWS, tm), :])
    y_b = _dot((cb_ref[...].astype(F32) * conv).astype(BF16), wb_ref[...])

    merged = _sigmoid(ga_ref[...].astype(F32)) * y_a + _sigmoid(gb_ref[...].astype(F32)) * y_b
    o_ref[...] = x_ref[...] + _dot(merged.astype(BF16), wo_ref[...])


def _out(x1, o_cmp_t, o_slc_t, o_win_t, p_arr, conv_w, w_nsa_out, w_conv_out, w_out):
    s = x1.shape[0]
    tm = OUT_TM
    row = lambda i: (i, 0)
    col = lambda i: (0, i)
    prev_blocks = tm // PREV_ROWS

    def seg(c):
        return pl.BlockSpec((tm, D_MODEL), lambda i: (i, c // 8))

    def seg_prev(c):
        return pl.BlockSpec((PREV_ROWS, D_MODEL), lambda i: (jnp.maximum(i * prev_blocks - 1, 0), c // 8))

    full = lambda shape: pl.BlockSpec(shape, lambda i: (0, 0))
    return pl.pallas_call(
        _out_kernel,
        grid=(s // tm,),
        in_specs=[
            pl.BlockSpec((tm, D_MODEL), row), pl.BlockSpec((D_MODEL, tm), col),
            pl.BlockSpec((D_MODEL, tm), col), pl.BlockSpec((D_MODEL, tm), col),
            seg(P_CB), seg(P_CC), seg(P_CX), seg_prev(P_CC), seg_prev(P_CX), seg(P_GA), seg(P_GB),
            full((CONV_K, CONV_WIDTH)), full((D_MODEL, D_MODEL)), full((CONV_WIDTH, D_MODEL)),
            full((D_MODEL, D_MODEL)),
        ],
        out_specs=pl.BlockSpec((tm, D_MODEL), row),
        out_shape=jax.ShapeDtypeStruct((s, D_MODEL), F32),
        scratch_shapes=[pltpu.VMEM((tm + PREV_ROWS, CONV_WIDTH), F32)],
        compiler_params=_params(("parallel",)),
        name="merge_out",
    )(x1, o_cmp_t, o_slc_t, o_win_t, p_arr, p_arr, p_arr, p_arr, p_arr, p_arr, p_arr,
      conv_w, w_nsa_out, w_conv_out, w_out)


def _layer(x, ffn1_norm, ffn1_w_gate, ffn1_w_up, ffn1_w_down, mix_norm, w_in, cmp_pe_k, cmp_pe_v,
           cmp_k_w1, cmp_k_w2, cmp_v_w1, cmp_v_w2, conv_w, w_nsa_out, w_conv_out, w_out,
           ffn2_norm, ffn2_w_gate, ffn2_w_up, ffn2_w_down, final_g, final_norm):
    s = x.shape[0]
    n_cmp = s // CMP_STRIDE
    n_slc = s // SLC_BLOCK
    row = lambda v: v.reshape(1, -1).astype(F32)

    x1 = _ffn(x, row(ffn1_norm), ffn1_w_gate.astype(BF16), ffn1_w_up.astype(BF16),
              ffn1_w_down.astype(BF16), row(final_g), False)
    p_arr = _proj(x1, row(mix_norm), _proj_weight(w_in))

    col = lambda c: p_arr[:, c * LANES:(c + 1) * LANES]
    chunks = lambda c: col(c).reshape(n_cmp, CMP_STRIDE * LANES)
    kc = _compress(chunks(P_KC), *_compress_weights(cmp_pe_k, cmp_k_w1, cmp_k_w2, False), False)
    vct = _compress(chunks(P_VC), *_compress_weights(cmp_pe_v, cmp_v_w1, cmp_v_w2, True), True)

    g3t = col(P_G3).T
    o_cmp_t, sel_t = _cmp_attention(p_arr, kc, vct, _overlap_matrix_t(n_cmp, n_slc), g3t)
    o_slc_t = _flash(p_arr, col(P_VS).T, g3t, sel_t, "slc")
    o_win_t = _flash(p_arr, col(P_VW).T, g3t, None, "win")
    x2 = _out(x1, o_cmp_t, o_slc_t, o_win_t, p_arr, conv_w.astype(F32), w_nsa_out.astype(BF16),
              w_conv_out.astype(BF16), w_out.astype(BF16))
    return _ffn(x2, row(ffn2_norm), ffn2_w_gate.astype(BF16), ffn2_w_up.astype(BF16),
                ffn2_w_down.astype(BF16), row(final_g), final_norm)


def kernel(x, ffn1_norm, ffn1_w_gate, ffn1_w_up, ffn1_w_down, mix_norm, w_in, cmp_pe_k, cmp_pe_v,
           cmp_k_w1, cmp_k_w2, cmp_v_w1, cmp_v_w2, conv_w, w_nsa_out, w_conv_out, w_out, ffn2_norm,
           ffn2_w_gate, ffn2_w_up, ffn2_w_down, final_norm):
    batch, _, _ = x.shape
    depth = ffn1_norm.shape[0]
    outs = []
    for b in range(batch):
        xb = x[b]
        for l in range(depth):
            xb = _layer(xb, ffn1_norm[l], ffn1_w_gate[l], ffn1_w_up[l], ffn1_w_down[l], mix_norm[l],
                        w_in[l], cmp_pe_k[l], cmp_pe_v[l], cmp_k_w1[l], cmp_k_w2[l], cmp_v_w1[l],
                        cmp_v_w2[l], conv_w[l], w_nsa_out[l], w_conv_out[l], w_out[l], ffn2_norm[l],
                        ffn2_w_gate[l], ffn2_w_up[l], ffn2_w_down[l], final_norm, l == depth - 1)
        outs.append(xb)
    return jnp.stack(outs)
```

```python
import functools

import jax
import jax.numpy as jnp
import numpy as np
from jax import lax
from jax.experimental import pallas as pl
from jax.experimental.pallas import tpu as pltpu

D_MODEL = 1024
N_HEADS = 16
HEAD_DIM = 64
N_GROUPS = 2
HEADS_PER_GROUP = N_HEADS // N_GROUPS
N_PAIRS = N_HEADS // 2
PAIRS_PER_GROUP = N_PAIRS // N_GROUPS
CMP_BLOCK = 32
CMP_STRIDE = 16
CMP_HIDDEN = 256
SLC_BLOCK = 64
N_SELECT = 16
WINDOW = 512
CONV_WIDTH = 1024
CONV_K = 3
D_FF = 2816
EPS = 1e-6
NEG = -(2.0 ** 100)
N_FORCED = 3
LOG2E = float(np.log2(np.e))
LANES = 128
BF16_ROWS = 16

P_Q, P_CB, P_CC, P_CX, P_GA, P_GB = 0, 8, 16, 24, 32, 40
P_KC, P_VC, P_KS, P_VS, P_KW, P_VW, P_G3 = 48, 49, 50, 51, 52, 53, 54
P_BLOCKS = 56
P_COLS = P_BLOCKS * LANES

F32 = jnp.float32
BF16 = jnp.bfloat16

VMEM_LIMIT = 52 * 1024 * 1024


def _dot(a, b):
    return jnp.dot(a, b, preferred_element_type=F32)


def _dot_nt(a, b):
    return lax.dot_general(a, b, (((1,), (1,)), ((), ())), preferred_element_type=F32)


def _sigmoid(x):
    return 1.0 / (1.0 + jnp.exp(-x))


def _rms(x, g):
    return x * lax.rsqrt(jnp.mean(x * x, axis=-1, keepdims=True) + EPS) * g


def _params(sem, limit=VMEM_LIMIT, flags=None):
    return pltpu.CompilerParams(dimension_semantics=sem, vmem_limit_bytes=limit, flags=flags)


FFN_TM = 512
FFN_CHUNK = 256


def _ffn_kernel(x_ref, g_ref, wg_ref, wu_ref, wd_ref, fg_ref, o_ref, *, final_norm):
    x = x_ref[...]
    h = _rms(x, g_ref[...]).astype(BF16)
    acc = None
    for c in range(D_FF // FFN_CHUNK):
        cols = slice(c * FFN_CHUNK, (c + 1) * FFN_CHUNK)
        gate = _dot(h, wg_ref[:, cols])
        up = _dot(h, wu_ref[:, cols])
        a = (gate * _sigmoid(gate) * up).astype(BF16)
        part = _dot(a, wd_ref[cols, :])
        acc = part if acc is None else acc + part
    y = x + 0.5 * acc
    if final_norm:
        y = _rms(y, fg_ref[...])
    o_ref[...] = y


def _ffn(x, norm_g, wg, wu, wd, final_g, final_norm):
    s = x.shape[0]
    resident = lambda shape: pl.BlockSpec(shape, lambda i: (0, 0), pipeline_mode=pl.Buffered(1))
    return pl.pallas_call(
        functools.partial(_ffn_kernel, final_norm=final_norm),
        grid=(s // FFN_TM,),
        in_specs=[
            pl.BlockSpec((FFN_TM, D_MODEL), lambda i: (i, 0)),
            resident((1, D_MODEL)),
            resident((D_MODEL, D_FF)),
            resident((D_MODEL, D_FF)),
            resident((D_FF, D_MODEL)),
            resident((1, D_MODEL)),
        ],
        out_specs=pl.BlockSpec((FFN_TM, D_MODEL), lambda i: (i, 0)),
        out_shape=jax.ShapeDtypeStruct((s, D_MODEL), F32),
        compiler_params=_params(("parallel",)),
        name="ffn_final" if final_norm else "ffn",
    )(x, norm_g, wg, wu, wd, final_g)


PROJ_TM = 1024
PROJ_TN = 1024


def _proj_kernel(x_ref, g_ref, w_ref, o_ref, h_ref):
    @pl.when(pl.program_id(1) == 0)
    def _():
        h_ref[...] = _rms(x_ref[...], g_ref[...]).astype(BF16)

    o_ref[...] = _dot(h_ref[...], w_ref[...]).astype(BF16)


def _proj(x, norm_g, w):
    s = x.shape[0]
    grid = (s // PROJ_TM, P_COLS // PROJ_TN)
    return pl.pallas_call(
        _proj_kernel,
        grid=grid,
        in_specs=[
            pl.BlockSpec((PROJ_TM, D_MODEL), lambda i, j: (i, 0)),
            pl.BlockSpec((1, D_MODEL), lambda i, j: (0, 0)),
            pl.BlockSpec((D_MODEL, PROJ_TN), lambda i, j: (0, j)),
        ],
        out_specs=pl.BlockSpec((PROJ_TM, PROJ_TN), lambda i, j: (i, j)),
        out_shape=jax.ShapeDtypeStruct((s, P_COLS), BF16),
        scratch_shapes=[pltpu.VMEM((PROJ_TM, D_MODEL), BF16)],
        compiler_params=_params(("parallel", "arbitrary")),
        name="proj",
    )(x, norm_g, w)


def _proj_weight(w_in):
    q, kc, vc, ks, vs, kw, vw, g3, cb, cc, cx, ga, gb = jnp.split(
        w_in, np.cumsum([1024, 128, 128, 128, 128, 128, 128, 48, 1024, 1024, 1024, 1024]).tolist(), axis=1)
    g3 = jnp.pad(g3, ((0, 0), (0, (P_BLOCKS - P_G3) * LANES - g3.shape[1])))
    w = jnp.concatenate([q * (HEAD_DIM ** -0.5 * LOG2E), cb, cc, cx, ga, gb, kc, vc, ks, vs, kw, vw, g3], axis=1)
    return w.astype(BF16)


def _gelu_tanh(x):
    return 0.5 * x * (1.0 + jnp.tanh(np.sqrt(2.0 / np.pi).astype(np.float32) * (x + 0.044715 * (x * x * x))))


def _compress_kernel(cf_ref, pe_ref, w1_ref, w2_ref, o_ref, *, transposed):
    cf = cf_ref[...].astype(F32)
    top = (cf + pe_ref[0]).astype(BF16)
    bot = (cf + pe_ref[1]).astype(BF16)
    a = _dot(top, w1_ref[0, 0])
    b = _dot(bot, w1_ref[0, 1])
    n = a.shape[0]
    pre = a + pltpu.roll(b, n - 1, 0)
    h = _gelu_tanh(pre).astype(BF16)
    if transposed:
        o_ref[0] = _dot_nt(w2_ref[...], h).astype(BF16)
    else:
        o_ref[0, 0] = _dot(h, w2_ref[0]).astype(BF16)
        o_ref[0, 1] = _dot(h, w2_ref[1]).astype(BF16)


def _compress(cf, pe, w1, w2, transposed):
    n = cf.shape[0]
    width = CMP_STRIDE * LANES
    if transposed:
        w2_spec = pl.BlockSpec((HEAD_DIM, CMP_HIDDEN), lambda g: (0, 0))
        out_spec = pl.BlockSpec((1, HEAD_DIM, n), lambda g: (g, 0, 0))
        out_shape = jax.ShapeDtypeStruct((N_GROUPS, HEAD_DIM, n), BF16)
    else:
        w2_spec = pl.BlockSpec((2, CMP_HIDDEN, LANES), lambda g: (0, 0, 0))
        out_spec = pl.BlockSpec((1, 2, n, LANES), lambda g: (g, 0, 0, 0))
        out_shape = jax.ShapeDtypeStruct((N_GROUPS, 2, n, LANES), BF16)
    return pl.pallas_call(
        functools.partial(_compress_kernel, transposed=transposed),
        grid=(N_GROUPS,),
        in_specs=[
            pl.BlockSpec((n, width), lambda g: (0, 0)),
            pl.BlockSpec((2, 1, width), lambda g: (0, 0, 0)),
            pl.BlockSpec((1, 2, width, CMP_HIDDEN), lambda g: (g, 0, 0, 0)),
            w2_spec,
        ],
        out_specs=out_spec,
        out_shape=out_shape,
        compiler_params=_params(("parallel",)),
        name="compress_v" if transposed else "compress_k",
    )(cf, pe, w1, w2)


def _compress_weights(pe, w1, w2, transposed):
    w1h = w1.reshape(2, CMP_STRIDE, HEAD_DIM, CMP_HIDDEN)
    zeros = jnp.zeros_like(w1h)
    w1g = jnp.stack([jnp.concatenate([w1h, zeros], axis=2), jnp.concatenate([zeros, w1h], axis=2)])
    w1g = w1g.reshape(N_GROUPS, 2, CMP_STRIDE * LANES, CMP_HIDDEN).astype(BF16)
    peh = pe.reshape(2, CMP_STRIDE, HEAD_DIM)
    pe2 = jnp.concatenate([peh, peh], axis=2).reshape(2, 1, CMP_STRIDE * LANES)
    if transposed:
        w2p = w2.T.astype(BF16)
    else:
        z2 = jnp.zeros_like(w2)
        w2p = jnp.stack([jnp.concatenate([w2, z2], axis=1), jnp.concatenate([z2, w2], axis=1)]).astype(BF16)
    return pe2, w1g, w2p


def _split_heads(kt):
    lo = lax.broadcasted_iota(jnp.int32, kt.shape, 1) < HEAD_DIM
    kr = pltpu.roll(kt, HEAD_DIM, 1)
    zero = jnp.zeros_like(kt)
    return [[jnp.where(lo, kt, zero), jnp.where(lo, zero, kr)],
            [jnp.where(lo, kr, zero), jnp.where(lo, zero, kt)]]


CMP_TQ = 256
CMP_KEY_CHUNK = 256


def _cmp_kernel(q_ref, kc_ref, vct_ref, ovt_ref, g3t_ref, o_ref, sel_ref, *, n_slc):
    tq = q_ref.shape[0]
    n_cmp = kc_ref.shape[2]
    q0 = pl.program_id(0) * tq
    t = q0 + lax.broadcasted_iota(jnp.int32, (1, tq), 1)
    any_valid = t >= CMP_BLOCK - 1
    cur = t // SLC_BLOCK

    def body(n_keys, n_blk):
        c_end = lax.broadcasted_iota(jnp.int32, (n_keys, 1), 0) * CMP_STRIDE + (CMP_BLOCK - 1)
        bias = jnp.where(c_end <= t, 0.0, NEG)
        gates = _sigmoid(g3t_ref[...].astype(F32))

        def scores(h):
            p, e = divmod(h, 2)
            g = p // PAIRS_PER_GROUP
            s = _dot_nt(kc_ref[g, e, :n_keys, :], q_ref[:, p * LANES:(p + 1) * LANES]) + bias
            return s, jnp.max(s, axis=0, keepdims=True)

        pcsum = [None] * N_GROUPS
        pending = [scores(h) for h in range(SCORE_LOOKAHEAD)]
        for h in range(N_HEADS):
            g = h // HEADS_PER_GROUP
            if h + SCORE_LOOKAHEAD < N_HEADS:
                pending.append(scores(h + SCORE_LOOKAHEAD))
            s, m = pending.pop(0)
            pr = jnp.exp2(s - m)
            l = jnp.sum(pr, axis=0, keepdims=True)
            pc = pr * jnp.where(any_valid, 1.0 / l, 0.0)
            pcsum[g] = pc if pcsum[g] is None else pcsum[g] + pc
            o_h = _dot(vct_ref[g, :, :n_keys], pc.astype(BF16)) * gates[3 * h:3 * h + 1, :]
            o_ref[h * HEAD_DIM:(h + 1) * HEAD_DIM, :] = o_h.astype(BF16)

        blk = lax.broadcasted_iota(jnp.int32, (n_blk, tq), 0)
        blk_f = blk.astype(F32)
        forced = (blk == 0) | (blk == cur) | (blk == cur - 1)
        causal = blk * SLC_BLOCK <= t
        ovt = ovt_ref[:n_blk, :n_keys]
        for g in range(N_GROUPS):
            hi = pcsum[g].astype(BF16)
            lo_part = (pcsum[g] - hi.astype(F32)).astype(BF16)
            ps = _dot(ovt, hi) + _dot(ovt, lo_part)
            work = jnp.where(causal & ~forced, ps, -1.0)
            sel = jnp.where(forced, 1.0, 0.0)
            for _ in range(N_SELECT - N_FORCED):
                m = jnp.max(work, axis=0, keepdims=True)
                first = jnp.min(jnp.where(work == m, blk_f, float(n_blk)), axis=0, keepdims=True)
                hit = blk_f == first
                sel = jnp.where(hit, 1.0, sel)
                work = jnp.where(hit, -jnp.inf, work)
            sel_ref[g * n_slc:g * n_slc + n_blk, :] = jnp.where(sel > 0.0, 0.0, NEG)
            if n_blk < n_slc:
                sel_ref[g * n_slc + n_blk:(g + 1) * n_slc, :] = jnp.full((n_slc - n_blk, tq), NEG, F32)

    chunk_tokens = CMP_KEY_CHUNK * CMP_STRIDE
    variant = (q0 + tq - 1) // chunk_tokens
    for v in range(n_cmp // CMP_KEY_CHUNK):
        pl.when(variant == v)(functools.partial(body, (v + 1) * CMP_KEY_CHUNK,
                                                (v + 1) * chunk_tokens // SLC_BLOCK))


def _cmp_attention(p_arr, kc, vct, overlap_t, g3t):
    s = p_arr.shape[0]
    n_cmp = kc.shape[2]
    n_slc = s // SLC_BLOCK
    return pl.pallas_call(
        functools.partial(_cmp_kernel, n_slc=n_slc),
        grid=(s // CMP_TQ,),
        in_specs=[
            pl.BlockSpec((CMP_TQ, D_MODEL), lambda i: (i, P_Q // 8)),
            pl.BlockSpec((N_GROUPS, 2, n_cmp, LANES), lambda i: (0, 0, 0, 0)),
            pl.BlockSpec((N_GROUPS, HEAD_DIM, n_cmp), lambda i: (0, 0, 0)),
            pl.BlockSpec((n_slc, n_cmp), lambda i: (0, 0)),
            pl.BlockSpec((LANES, CMP_TQ), lambda i: (0, i)),
        ],
        out_specs=[
            pl.BlockSpec((D_MODEL, CMP_TQ), lambda i: (0, i)),
            pl.BlockSpec((N_GROUPS * n_slc, CMP_TQ), lambda i: (0, i)),
        ],
        out_shape=[
            jax.ShapeDtypeStruct((D_MODEL, s), BF16),
            jax.ShapeDtypeStruct((N_GROUPS * n_slc, s), F32),
        ],
        compiler_params=_params(("parallel",)),
        name="cmp_attention",
    )(p_arr, kc, vct, overlap_t, g3t)


def _overlap_matrix_t(n_cmp, n_slc):
    c0 = np.arange(n_cmp)[None, :] * CMP_STRIDE
    s0 = np.arange(n_slc)[:, None] * SLC_BLOCK
    ov = (c0 < s0 + SLC_BLOCK) & (c0 + CMP_BLOCK > s0)
    return jnp.asarray(ov, dtype=BF16)


QSUB = 256
SLC_TQ = 512
SLC_TK = 512
WIN_TQ = 256
WIN_TK = 256
SCORE_LOOKAHEAD = 4
ACC_ROWS = HEAD_DIM + BF16_ROWS


def _flash_kernel(qi_ref, ki_ref, fl_ref, *refs, mode, n_slc, branch):
    if mode == "slc":
        q_ref, k_ref, vt_ref, g3t_ref, sel_ref, o_ref, m_ref, acc_ref = refs
    else:
        q_ref, k_ref, vt_ref, g3t_ref, o_ref, m_ref, acc_ref = refs
    step = pl.program_id(0)
    tq = q_ref.shape[0]
    tk = k_ref.shape[0]
    flags = fl_ref[step]

    @pl.when((flags & 1) != 0)
    def _():
        m_ref[...] = jnp.full_like(m_ref, NEG)
        acc_ref[...] = jnp.zeros_like(acc_ref)

    q0 = qi_ref[step] * tq
    k0 = ki_ref[step] * tk
    t = q0 + lax.broadcasted_iota(jnp.int32, (1, tq), 1)
    key = k0 + lax.broadcasted_iota(jnp.int32, (tk, 1), 0)

    if mode == "slc":
        tile_blocks = tk // SLC_BLOCK
        pad_blocks = BF16_ROWS - tile_blocks % BF16_ROWS
        key_blk = lax.broadcasted_iota(jnp.int32, (tk, tile_blocks + pad_blocks), 0) // SLC_BLOCK
        blk = lax.broadcasted_iota(jnp.int32, (tk, tile_blocks + pad_blocks), 1)
        expand = (blk == key_blk).astype(BF16)
        causal_bias = jnp.where(key <= t, 0.0, NEG)
        bias = []
        for g in range(N_GROUPS):
            first_block = pl.multiple_of(g * n_slc + ki_ref[step] * tile_blocks, tile_blocks)
            rows = jnp.concatenate([sel_ref[pl.ds(first_block, tile_blocks), :],
                                    jnp.zeros((pad_blocks, tq), F32)], axis=0)
            bias.append(jnp.minimum(_dot(expand, rows.astype(BF16)), causal_bias))
    else:
        d = t - key
        band = jnp.where((d >= 0) & (d < WINDOW), 0.0, NEG)
        bias = [band] * N_GROUPS

    k_half = _split_heads(k_ref[...])
    vt = vt_ref[...]
    ones = jnp.ones((ACC_ROWS - HEAD_DIM, tk), BF16)
    v_aug = [jnp.concatenate([vt[g * HEAD_DIM:(g + 1) * HEAD_DIM, :], ones], axis=0) for g in range(N_GROUPS)]

    units = [(h, slice(qs * QSUB, (qs + 1) * QSUB)) for qs in range(tq // QSUB) for h in range(N_HEADS)]

    def scores(unit):
        h, cols = unit
        p, e = divmod(h, 2)
        g = p // PAIRS_PER_GROUP
        s = (_dot_nt(k_half[g][e], q_ref[cols, p * LANES:(p + 1) * LANES]) + bias[g][:, cols]).astype(BF16)
        return s, jnp.max(s, axis=0, keepdims=True)

    pending = [scores(u) for u in units[:SCORE_LOOKAHEAD]]
    for n, (h, cols) in enumerate(units):
        g = h // HEADS_PER_GROUP
        if n + SCORE_LOOKAHEAD < len(units):
            pending.append(scores(units[n + SCORE_LOOKAHEAD]))
        s, s_max = pending.pop(0)
        m_prev = m_ref[h:h + 1, cols]
        m_new = jnp.maximum(m_prev, s_max.astype(F32))
        alpha = jnp.exp2(m_prev - m_new)
        pr = jnp.exp2(s - m_new.astype(BF16))
        m_ref[h:h + 1, cols] = m_new
        acc_ref[h, :, cols] = acc_ref[h, :, cols] * alpha + _dot(v_aug[g], pr)

    @pl.when((flags & 2) != 0)
    def _():
        gates = _sigmoid(g3t_ref[...].astype(F32))
        for h in range(N_HEADS):
            acc = acc_ref[h]
            scale = gates[3 * h + branch:3 * h + branch + 1, :] / acc[HEAD_DIM:HEAD_DIM + 1, :]
            o_ref[h * HEAD_DIM:(h + 1) * HEAD_DIM, :] = (acc[:HEAD_DIM, :] * scale).astype(BF16)


def _flash_tables(s, tq, tk, mode):
    qi, ki, fl = [], [], []
    for i in range(s // tq):
        q_lo, q_hi = i * tq, i * tq + tq - 1
        first = 0 if mode == "slc" else max(0, (q_lo - (WINDOW - 1)) // tk)
        last = q_hi // tk
        for k in range(first, last + 1):
            qi.append(i)
            ki.append(k)
            fl.append((1 if k == first else 0) | (2 if k == last else 0))
    return (jnp.asarray(qi, jnp.int32), jnp.asarray(ki, jnp.int32), jnp.asarray(fl, jnp.int32))


def _flash(p_arr, vt, g3t, sel_t, mode):
    s = p_arr.shape[0]
    n_slc = s // SLC_BLOCK
    tq, tk = (SLC_TQ, SLC_TK) if mode == "slc" else (WIN_TQ, WIN_TK)
    k_col, branch = (P_KS, 1) if mode == "slc" else (P_KW, 2)
    qi, ki, fl = _flash_tables(s, tq, tk, mode)
    in_specs = [
        pl.BlockSpec((tq, D_MODEL), lambda n, qi, ki, fl: (qi[n], P_Q // 8)),
        pl.BlockSpec((tk, LANES), lambda n, qi, ki, fl: (ki[n], k_col)),
        pl.BlockSpec((LANES, tk), lambda n, qi, ki, fl: (0, ki[n])),
        pl.BlockSpec((LANES, tq), lambda n, qi, ki, fl: (0, qi[n])),
    ]
    args = [p_arr, p_arr, vt, g3t]
    if mode == "slc":
        in_specs.append(pl.BlockSpec((N_GROUPS * n_slc, tq), lambda n, qi, ki, fl: (0, qi[n])))
        args.append(sel_t)
    grid_spec = pltpu.PrefetchScalarGridSpec(
        num_scalar_prefetch=3,
        grid=(qi.shape[0],),
        in_specs=in_specs,
        out_specs=pl.BlockSpec((D_MODEL, tq), lambda n, qi, ki, fl: (0, qi[n])),
        scratch_shapes=[
            pltpu.VMEM((N_HEADS, tq), F32),
            pltpu.VMEM((N_HEADS, ACC_ROWS, tq), F32),
        ],
    )
    return pl.pallas_call(
        functools.partial(_flash_kernel, mode=mode, n_slc=n_slc, branch=branch),
        grid_spec=grid_spec,
        out_shape=jax.ShapeDtypeStruct((D_MODEL, s), BF16),
        compiler_params=_params(("arbitrary",)),
        name="flash_" + mode,
    )(qi, ki, fl, *args)


OUT_TM = 512
PREV_ROWS = 16


def _out_kernel(x_ref, oc_ref, os_ref, ow_ref, cb_ref, cc_ref, cx_ref, pc_ref, px_ref, ga_ref, gb_ref,
                cw_ref, wa_ref, wb_ref, wo_ref, o_ref, u_ref):
    tm = x_ref.shape[0]
    o_nsa_t = oc_ref[...].astype(F32) + os_ref[...].astype(F32) + ow_ref[...].astype(F32)
    y_a = _dot(o_nsa_t.T.astype(BF16), wa_ref[...])

    prev = pc_ref[...].astype(F32) * px_ref[...].astype(F32)
    u_ref[0:PREV_ROWS, :] = jnp.where(pl.program_id(0) == 0, 0.0, prev)
    u_ref[PREV_ROWS:, :] = cc_ref[...].astype(F32) * cx_ref[...].astype(F32)
    cw = cw_ref[...]
    conv = (cw[0:1, :] * u_ref[pl.ds(PREV_ROWS - 2, tm), :]
            + cw[1:2, :] * u_ref[pl.ds(PREV_ROWS - 1, tm), :]
            + cw[2:3, :] * u_ref[pl.ds(PREV_ROWS, tm), :])
    y_b = _dot((cb_ref[...].astype(F32) * conv).astype(BF16), wb_ref[...])

    merged = _sigmoid(ga_ref[...].astype(F32)) * y_a + _sigmoid(gb_ref[...].astype(F32)) * y_b
    o_ref[...] = x_ref[...] + _dot(merged.astype(BF16), wo_ref[...])


def _out(x1, o_cmp_t, o_slc_t, o_win_t, p_arr, conv_w, w_nsa_out, w_conv_out, w_out):
    s = x1.shape[0]
    tm = OUT_TM
    row = lambda i: (i, 0)
    col = lambda i: (0, i)
    prev_blocks = tm // PREV_ROWS

    def seg(c):
        return pl.BlockSpec((tm, D_MODEL), lambda i: (i, c // 8))

    def seg_prev(c):
        return pl.BlockSpec((PREV_ROWS, D_MODEL), lambda i: (jnp.maximum(i * prev_blocks - 1, 0), c // 8))

    full = lambda shape: pl.BlockSpec(shape, lambda i: (0, 0))
    return pl.pallas_call(
        _out_kernel,
        grid=(s // tm,),
        in_specs=[
            pl.BlockSpec((tm, D_MODEL), row), pl.BlockSpec((D_MODEL, tm), col),
            pl.BlockSpec((D_MODEL, tm), col), pl.BlockSpec((D_MODEL, tm), col),
            seg(P_CB), seg(P_CC), seg(P_CX), seg_prev(P_CC), seg_prev(P_CX), seg(P_GA), seg(P_GB),
            full((CONV_K, CONV_WIDTH)), full((D_MODEL, D_MODEL)), full((CONV_WIDTH, D_MODEL)),
            full((D_MODEL, D_MODEL)),
        ],
        out_specs=pl.BlockSpec((tm, D_MODEL), row),
        out_shape=jax.ShapeDtypeStruct((s, D_MODEL), F32),
        scratch_shapes=[pltpu.VMEM((tm + PREV_ROWS, CONV_WIDTH), F32)],
        compiler_params=_params(("parallel",)),
        name="merge_out",
    )(x1, o_cmp_t, o_slc_t, o_win_t, p_arr, p_arr, p_arr, p_arr, p_arr, p_arr, p_arr,
      conv_w, w_nsa_out, w_conv_out, w_out)


def _layer(x, ffn1_norm, ffn1_w_gate, ffn1_w_up, ffn1_w_down, mix_norm, w_in, cmp_pe_k, cmp_pe_v,
           cmp_k_w1, cmp_k_w2, cmp_v_w1, cmp_v_w2, conv_w, w_nsa_out, w_conv_out, w_out,
           ffn2_norm, ffn2_w_gate, ffn2_w_up, ffn2_w_down, final_g, final_norm):
    s = x.shape[0]
    n_cmp = s // CMP_STRIDE
    n_slc = s // SLC_BLOCK
    row = lambda v: v.reshape(1, -1).astype(F32)

    x1 = _ffn(x, row(ffn1_norm), ffn1_w_gate.astype(BF16), ffn1_w_up.astype(BF16),
              ffn1_w_down.astype(BF16), row(final_g), False)
    p_arr = _proj(x1, row(mix_norm), _proj_weight(w_in))

    col = lambda c: p_arr[:, c * LANES:(c + 1) * LANES]
    chunks = lambda c: col(c).reshape(n_cmp, CMP_STRIDE * LANES)
    kc = _compress(chunks(P_KC), *_compress_weights(cmp_pe_k, cmp_k_w1, cmp_k_w2, False), False)
    vct = _compress(chunks(P_VC), *_compress_weights(cmp_pe_v, cmp_v_w1, cmp_v_w2, True), True)

    g3t = col(P_G3).T
    o_cmp_t, sel_t = _cmp_attention(p_arr, kc, vct, _overlap_matrix_t(n_cmp, n_slc), g3t)
    o_slc_t = _flash(p_arr, col(P_VS).T, g3t, sel_t, "slc")
    o_win_t = _flash(p_arr, col(P_VW).T, g3t, None, "win")
    x2 = _out(x1, o_cmp_t, o_slc_t, o_win_t, p_arr, conv_w.astype(F32), w_nsa_out.astype(BF16),
              w_conv_out.astype(BF16), w_out.astype(BF16))
    return _ffn(x2, row(ffn2_norm), ffn2_w_gate.astype(BF16), ffn2_w_up.astype(BF16),
                ffn2_w_down.astype(BF16), row(final_g), final_norm)


def kernel(x, ffn1_norm, ffn1_w_gate, ffn1_w_up, ffn1_w_down, mix_norm, w_in, cmp_pe_k, cmp_pe_v,
           cmp_k_w1, cmp_k_w2, cmp_v_w1, cmp_v_w2, conv_w, w_nsa_out, w_conv_out, w_out, ffn2_norm,
           ffn2_w_gate, ffn2_w_up, ffn2_w_down, final_norm):
    batch, _, _ = x.shape
    depth = ffn1_norm.shape[0]
    outs = []
    for b in range(batch):
        xb = x[b]
        for l in range(depth):
            xb = _layer(xb, ffn1_norm[l], ffn1_w_gate[l], ffn1_w_up[l], ffn1_w_down[l], mix_norm[l],
                        w_in[l], cmp_pe_k[l], cmp_pe_v[l], cmp_k_w1[l], cmp_k_w2[l], cmp_v_w1[l],
                        cmp_v_w2[l], conv_w[l], w_nsa_out[l], w_conv_out[l], w_out[l], ffn2_norm[l],
                        ffn2_w_gate[l], ffn2_w_up[l], ffn2_w_down[l], final_norm, l == depth - 1)
        outs.append(xb)
    return jnp.stack(outs)
```

```python
import functools

import jax
import jax.numpy as jnp
import numpy as np
from jax import lax
from jax.experimental import pallas as pl
from jax.experimental.pallas import tpu as pltpu

D_MODEL = 1024
N_HEADS = 16
HEAD_DIM = 64
N_GROUPS = 2
HEADS_PER_GROUP = N_HEADS // N_GROUPS
N_PAIRS = N_HEADS // 2
PAIRS_PER_GROUP = N_PAIRS // N_GROUPS
CMP_BLOCK = 32
CMP_STRIDE = 16
CMP_HIDDEN = 256
SLC_BLOCK = 64
N_SELECT = 16
WINDOW = 512
CONV_WIDTH = 1024
CONV_K = 3
D_FF = 2816
EPS = 1e-6
NEG = -(2.0 ** 100)
N_FORCED = 3
LOG2E = float(np.log2(np.e))
LANES = 128
BF16_ROWS = 16

P_Q, P_CB, P_CC, P_CX, P_GA, P_GB = 0, 8, 16, 24, 32, 40
P_KC, P_VC, P_KS, P_VS, P_KW, P_VW, P_G3 = 48, 49, 50, 51, 52, 53, 54
P_BLOCKS = 56
P_COLS = P_BLOCKS * LANES

F32 = jnp.float32
BF16 = jnp.bfloat16

VMEM_LIMIT = 52 * 1024 * 1024


def _dot(a, b):
    return jnp.dot(a, b, preferred_element_type=F32)


def _dot_nt(a, b):
    return lax.dot_general(a, b, (((1,), (1,)), ((), ())), preferred_element_type=F32)


def _sigmoid(x):
    return 1.0 / (1.0 + jnp.exp(-x))


def _rms(x, g):
    return x * lax.rsqrt(jnp.mean(x * x, axis=-1, keepdims=True) + EPS) * g


def _params(sem, limit=VMEM_LIMIT, flags=None):
    return pltpu.CompilerParams(dimension_semantics=sem, vmem_limit_bytes=limit, flags=flags)


FFN_TM = 512
FFN_CHUNK = 256


def _ffn_kernel(x_ref, g_ref, wg_ref, wu_ref, wd_ref, fg_ref, o_ref, *, final_norm):
    x = x_ref[...]
    h = _rms(x, g_ref[...]).astype(BF16)
    acc = None
    for c in range(D_FF // FFN_CHUNK):
        cols = slice(c * FFN_CHUNK, (c + 1) * FFN_CHUNK)
        gate = _dot(h, wg_ref[:, cols])
        up = _dot(h, wu_ref[:, cols])
        a = (gate * _sigmoid(gate) * up).astype(BF16)
        part = _dot(a, wd_ref[cols, :])
        acc = part if acc is None else acc + part
    y = x + 0.5 * acc
    if final_norm:
        y = _rms(y, fg_ref[...])
    o_ref[...] = y


def _ffn(x, norm_g, wg, wu, wd, final_g, final_norm):
    s = x.shape[0]
    resident = lambda shape: pl.BlockSpec(shape, lambda i: (0, 0), pipeline_mode=pl.Buffered(1))
    return pl.pallas_call(
        functools.partial(_ffn_kernel, final_norm=final_norm),
        grid=(s // FFN_TM,),
        in_specs=[
            pl.BlockSpec((FFN_TM, D_MODEL), lambda i: (i, 0)),
            resident((1, D_MODEL)),
            resident((D_MODEL, D_FF)),
            resident((D_MODEL, D_FF)),
            resident((D_FF, D_MODEL)),
            resident((1, D_MODEL)),
        ],
        out_specs=pl.BlockSpec((FFN_TM, D_MODEL), lambda i: (i, 0)),
        out_shape=jax.ShapeDtypeStruct((s, D_MODEL), F32),
        compiler_params=_params(("parallel",)),
        name="ffn_final" if final_norm else "ffn",
    )(x, norm_g, wg, wu, wd, final_g)


PROJ_TM = 1024
PROJ_TN = 1024


def _proj_kernel(x_ref, g_ref, w_ref, o_ref, h_ref):
    @pl.when(pl.program_id(1) == 0)
    def _():
        h_ref[...] = _rms(x_ref[...], g_ref[...]).astype(BF16)

    o_ref[...] = _dot(h_ref[...], w_ref[...]).astype(BF16)


def _proj(x, norm_g, w):
    s = x.shape[0]
    grid = (s // PROJ_TM, P_COLS // PROJ_TN)
    return pl.pallas_call(
        _proj_kernel,
        grid=grid,
        in_specs=[
            pl.BlockSpec((PROJ_TM, D_MODEL), lambda i, j: (i, 0)),
            pl.BlockSpec((1, D_MODEL), lambda i, j: (0, 0)),
            pl.BlockSpec((D_MODEL, PROJ_TN), lambda i, j: (0, j)),
        ],
        out_specs=pl.BlockSpec((PROJ_TM, PROJ_TN), lambda i, j: (i, j)),
        out_shape=jax.ShapeDtypeStruct((s, P_COLS), BF16),
        scratch_shapes=[pltpu.VMEM((PROJ_TM, D_MODEL), BF16)],
        compiler_params=_params(("parallel", "arbitrary")),
        name="proj",
    )(x, norm_g, w)


def _proj_weight(w_in):
    q, kc, vc, ks, vs, kw, vw, g3, cb, cc, cx, ga, gb = jnp.split(
        w_in, np.cumsum([1024, 128, 128, 128, 128, 128, 128, 48, 1024, 1024, 1024, 1024]).tolist(), axis=1)
    g3 = jnp.pad(g3, ((0, 0), (0, (P_BLOCKS - P_G3) * LANES - g3.shape[1])))
    w = jnp.concatenate([q * (HEAD_DIM ** -0.5 * LOG2E), cb, cc, cx, ga, gb, kc, vc, ks, vs, kw, vw, g3], axis=1)
    return w.astype(BF16)


def _gelu_tanh(x):
    return 0.5 * x * (1.0 + jnp.tanh(np.sqrt(2.0 / np.pi).astype(np.float32) * (x + 0.044715 * (x * x * x))))


def _compress_kernel(cf_ref, pe_ref, w1_ref, w2_ref, o_ref, *, transposed):
    cf = cf_ref[...].astype(F32)
    top = (cf + pe_ref[0]).astype(BF16)
    bot = (cf + pe_ref[1]).astype(BF16)
    a = _dot(top, w1_ref[0, 0])
    b = _dot(bot, w1_ref[0, 1])
    n = a.shape[0]
    pre = a + pltpu.roll(b, n - 1, 0)
    h = _gelu_tanh(pre).astype(BF16)
    if transposed:
        o_ref[0] = _dot_nt(w2_ref[...], h).astype(BF16)
    else:
        o_ref[0, 0] = _dot(h, w2_ref[0]).astype(BF16)
        o_ref[0, 1] = _dot(h, w2_ref[1]).astype(BF16)


def _compress(cf, pe, w1, w2, transposed):
    n = cf.shape[0]
    width = CMP_STRIDE * LANES
    if transposed:
        w2_spec = pl.BlockSpec((HEAD_DIM, CMP_HIDDEN), lambda g: (0, 0))
        out_spec = pl.BlockSpec((1, HEAD_DIM, n), lambda g: (g, 0, 0))
        out_shape = jax.ShapeDtypeStruct((N_GROUPS, HEAD_DIM, n), BF16)
    else:
        w2_spec = pl.BlockSpec((2, CMP_HIDDEN, LANES), lambda g: (0, 0, 0))
        out_spec = pl.BlockSpec((1, 2, n, LANES), lambda g: (g, 0, 0, 0))
        out_shape = jax.ShapeDtypeStruct((N_GROUPS, 2, n, LANES), BF16)
    return pl.pallas_call(
        functools.partial(_compress_kernel, transposed=transposed),
        grid=(N_GROUPS,),
        in_specs=[
            pl.BlockSpec((n, width), lambda g: (0, 0)),
            pl.BlockSpec((2, 1, width), lambda g: (0, 0, 0)),
            pl.BlockSpec((1, 2, width, CMP_HIDDEN), lambda g: (g, 0, 0, 0)),
            w2_spec,
        ],
        out_specs=out_spec,
        out_shape=out_shape,
        compiler_params=_params(("parallel",)),
        name="compress_v" if transposed else "compress_k",
    )(cf, pe, w1, w2)


def _compress_weights(pe, w1, w2, transposed):
    w1h = w1.reshape(2, CMP_STRIDE, HEAD_DIM, CMP_HIDDEN)
    zeros = jnp.zeros_like(w1h)
    w1g = jnp.stack([jnp.concatenate([w1h, zeros], axis=2), jnp.concatenate([zeros, w1h], axis=2)])
    w1g = w1g.reshape(N_GROUPS, 2, CMP_STRIDE * LANES, CMP_HIDDEN).astype(BF16)
    peh = pe.reshape(2, CMP_STRIDE, HEAD_DIM)
    pe2 = jnp.concatenate([peh, peh], axis=2).reshape(2, 1, CMP_STRIDE * LANES)
    if transposed:
        w2p = w2.T.astype(BF16)
    else:
        z2 = jnp.zeros_like(w2)
        w2p = jnp.stack([jnp.concatenate([w2, z2], axis=1), jnp.concatenate([z2, w2], axis=1)]).astype(BF16)
    return pe2, w1g, w2p


def _split_heads(kt):
    lo = lax.broadcasted_iota(jnp.int32, kt.shape, 1) < HEAD_DIM
    kr = pltpu.roll(kt, HEAD_DIM, 1)
    zero = jnp.zeros_like(kt)
    return [[jnp.where(lo, kt, zero), jnp.where(lo, zero, kr)],
            [jnp.where(lo, kr, zero), jnp.where(lo, zero, kt)]]


QSUB = 256
CMP_TQ = 512
CMP_KEY_CHUNK = 256
CMP_LOOKAHEAD = 4


def _cmp_kernel(q_ref, kc_ref, vct_ref, ovt_ref, g3t_ref, o_ref, sel_ref, *, n_slc):
    tq = q_ref.shape[0]
    n_cmp = kc_ref.shape[2]
    q0 = pl.program_id(0) * tq
    t = q0 + lax.broadcasted_iota(jnp.int32, (1, tq), 1)
    any_valid = t >= CMP_BLOCK - 1
    cur = t // SLC_BLOCK

    def body(n_keys, n_blk):
        c_end = lax.broadcasted_iota(jnp.int32, (n_keys, 1), 0) * CMP_STRIDE + (CMP_BLOCK - 1)
        bias = jnp.where(c_end <= t, 0.0, NEG)
        gates = _sigmoid(g3t_ref[...].astype(F32))

        ovt = ovt_ref[:n_blk, :n_keys]
        ones = jnp.ones((BF16_ROWS, n_keys), BF16)
        lhs = [jnp.concatenate([vct_ref[g, :, :n_keys], ones, ovt], axis=0) for g in range(N_GROUPS)]
        imp_row = HEAD_DIM + BF16_ROWS
        blk = lax.broadcasted_iota(jnp.int32, (n_blk, QSUB), 0)
        blk_f = blk.astype(F32)

        def select(cols, importance):
            forced = (blk == 0) | (blk == cur[:, cols]) | (blk == cur[:, cols] - 1)
            causal = blk * SLC_BLOCK <= t[:, cols]
            for g in range(N_GROUPS):
                work = jnp.where(causal & ~forced, importance[g], -1.0)
                sel = jnp.where(forced, 1.0, 0.0)
                for _ in range(N_SELECT - N_FORCED):
                    m = jnp.max(work, axis=0, keepdims=True)
                    first = jnp.min(jnp.where(work == m, blk_f, float(n_blk)), axis=0, keepdims=True)
                    hit = blk_f == first
                    sel = jnp.where(hit, 1.0, sel)
                    work = jnp.where(hit, -jnp.inf, work)
                sel_ref[g * n_slc:g * n_slc + n_blk, cols] = jnp.where(sel > 0.0, 0.0, NEG)
                if n_blk < n_slc:
                    sel_ref[g * n_slc + n_blk:(g + 1) * n_slc, cols] = jnp.full((n_slc - n_blk, QSUB), NEG, F32)

        units = [(h, slice(qs * QSUB, (qs + 1) * QSUB)) for qs in range(tq // QSUB) for h in range(N_HEADS)]

        def scores(unit):
            h, cols = unit
            p, e = divmod(h, 2)
            g = p // PAIRS_PER_GROUP
            s = _dot_nt(kc_ref[g, e, :n_keys, :], q_ref[cols, p * LANES:(p + 1) * LANES]) + bias[:, cols]
            return s, jnp.max(s, axis=0, keepdims=True)

        importance = [None] * N_GROUPS
        pending = [scores(u) for u in units[:CMP_LOOKAHEAD]]
        for n, (h, cols) in enumerate(units):
            g = h // HEADS_PER_GROUP
            if n + CMP_LOOKAHEAD < len(units):
                pending.append(scores(units[n + CMP_LOOKAHEAD]))
            s, m = pending.pop(0)
            pr = jnp.exp2(s - m).astype(BF16)
            res = _dot(lhs[g], pr)
            inv_l = jnp.where(any_valid[:, cols], 1.0 / res[HEAD_DIM:HEAD_DIM + 1, :], 0.0)
            o_h = res[:HEAD_DIM, :] * (inv_l * gates[3 * h:3 * h + 1, cols])
            o_ref[h * HEAD_DIM:(h + 1) * HEAD_DIM, cols] = o_h.astype(BF16)
            imp_h = res[imp_row:, :] * inv_l
            importance[g] = imp_h if importance[g] is None else importance[g] + imp_h
            if h == N_HEADS - 1:
                select(cols, importance)
                importance = [None] * N_GROUPS

    chunk_tokens = CMP_KEY_CHUNK * CMP_STRIDE
    variant = (q0 + tq - 1) // chunk_tokens
    for v in range(n_cmp // CMP_KEY_CHUNK):
        pl.when(variant == v)(functools.partial(body, (v + 1) * CMP_KEY_CHUNK,
                                                (v + 1) * chunk_tokens // SLC_BLOCK))


def _cmp_attention(p_arr, kc, vct, overlap_t, g3t):
    s = p_arr.shape[0]
    n_cmp = kc.shape[2]
    n_slc = s // SLC_BLOCK
    return pl.pallas_call(
        functools.partial(_cmp_kernel, n_slc=n_slc),
        grid=(s // CMP_TQ,),
        in_specs=[
            pl.BlockSpec((CMP_TQ, D_MODEL), lambda i: (i, P_Q // 8)),
            pl.BlockSpec((N_GROUPS, 2, n_cmp, LANES), lambda i: (0, 0, 0, 0)),
            pl.BlockSpec((N_GROUPS, HEAD_DIM, n_cmp), lambda i: (0, 0, 0)),
            pl.BlockSpec((n_slc, n_cmp), lambda i: (0, 0)),
            pl.BlockSpec((LANES, CMP_TQ), lambda i: (0, i)),
        ],
        out_specs=[
            pl.BlockSpec((D_MODEL, CMP_TQ), lambda i: (0, i)),
            pl.BlockSpec((N_GROUPS * n_slc, CMP_TQ), lambda i: (0, i)),
        ],
        out_shape=[
            jax.ShapeDtypeStruct((D_MODEL, s), BF16),
            jax.ShapeDtypeStruct((N_GROUPS * n_slc, s), F32),
        ],
        compiler_params=_params(("parallel",)),
        name="cmp_attention",
    )(p_arr, kc, vct, overlap_t, g3t)


def _overlap_matrix_t(n_cmp, n_slc):
    c0 = np.arange(n_cmp)[None, :] * CMP_STRIDE
    s0 = np.arange(n_slc)[:, None] * SLC_BLOCK
    ov = (c0 < s0 + SLC_BLOCK) & (c0 + CMP_BLOCK > s0)
    return jnp.asarray(ov, dtype=BF16)


SLC_TQ = 1024
SLC_TK = 512
SLC_LOOKAHEAD = 4
WIN_TQ = 256
WIN_TK = 256
WIN_LOOKAHEAD = 6
ACC_ROWS = HEAD_DIM + BF16_ROWS


def _flash_kernel(qi_ref, ki_ref, fl_ref, *refs, mode, n_slc, branch, lookahead):
    if mode == "slc":
        q_ref, k_ref, vt_ref, g3t_ref, sel_ref, o_ref, m_ref, acc_ref = refs
    else:
        q_ref, k_ref, vt_ref, g3t_ref, o_ref, m_ref, acc_ref = refs
    step = pl.program_id(0)
    tq = q_ref.shape[0]
    tk = k_ref.shape[0]
    flags = fl_ref[step]

    @pl.when((flags & 1) != 0)
    def _():
        m_ref[...] = jnp.full_like(m_ref, NEG)
        acc_ref[...] = jnp.zeros_like(acc_ref)

    q0 = qi_ref[step] * tq
    k0 = ki_ref[step] * tk
    t = q0 + lax.broadcasted_iota(jnp.int32, (1, tq), 1)
    key = k0 + lax.broadcasted_iota(jnp.int32, (tk, 1), 0)

    if mode == "slc":
        tile_blocks = tk // SLC_BLOCK
        pad_blocks = BF16_ROWS - tile_blocks % BF16_ROWS
        key_blk = lax.broadcasted_iota(jnp.int32, (tk, tile_blocks + pad_blocks), 0) // SLC_BLOCK
        blk = lax.broadcasted_iota(jnp.int32, (tk, tile_blocks + pad_blocks), 1)
        expand = (blk == key_blk).astype(BF16)
        causal_bias = jnp.where(key <= t, 0.0, NEG)
        bias = []
        for g in range(N_GROUPS):
            first_block = pl.multiple_of(g * n_slc + ki_ref[step] * tile_blocks, tile_blocks)
            rows = jnp.concatenate([sel_ref[pl.ds(first_block, tile_blocks), :],
                                    jnp.zeros((pad_blocks, tq), F32)], axis=0)
            bias.append(jnp.minimum(_dot(expand, rows.astype(BF16)), causal_bias))
    else:
        d = t - key
        band = jnp.where((d >= 0) & (d < WINDOW), 0.0, NEG)
        bias = [band] * N_GROUPS

    k_half = _split_heads(k_ref[...])
    vt = vt_ref[...]
    ones = jnp.ones((ACC_ROWS - HEAD_DIM, tk), BF16)
    v_aug = [jnp.concatenate([vt[g * HEAD_DIM:(g + 1) * HEAD_DIM, :], ones], axis=0) for g in range(N_GROUPS)]

    units = [(h, slice(qs * QSUB, (qs + 1) * QSUB)) for qs in range(tq // QSUB) for h in range(N_HEADS)]

    def scores(unit):
        h, cols = unit
        p, e = divmod(h, 2)
        g = p // PAIRS_PER_GROUP
        s = (_dot_nt(k_half[g][e], q_ref[cols, p * LANES:(p + 1) * LANES]) + bias[g][:, cols]).astype(BF16)
        return s, jnp.max(s, axis=0, keepdims=True)

    pending = [scores(u) for u in units[:lookahead]]
    for n, (h, cols) in enumerate(units):
        g = h // HEADS_PER_GROUP
        if n + lookahead < len(units):
            pending.append(scores(units[n + lookahead]))
        s, s_max = pending.pop(0)
        m_prev = m_ref[h:h + 1, cols]
        m_new = jnp.maximum(m_prev, s_max.astype(F32))
        alpha = jnp.exp2(m_prev - m_new)
        pr = jnp.exp2(s - m_new.astype(BF16))
        m_ref[h:h + 1, cols] = m_new
        acc_ref[h, :, cols] = acc_ref[h, :, cols] * alpha + _dot(v_aug[g], pr)

    @pl.when((flags & 2) != 0)
    def _():
        gates = _sigmoid(g3t_ref[...].astype(F32))
        for h in range(N_HEADS):
            acc = acc_ref[h]
            scale = gates[3 * h + branch:3 * h + branch + 1, :] / acc[HEAD_DIM:HEAD_DIM + 1, :]
            o_ref[h * HEAD_DIM:(h + 1) * HEAD_DIM, :] = (acc[:HEAD_DIM, :] * scale).astype(BF16)


def _flash_tables(s, tq, tk, mode):
    qi, ki, fl = [], [], []
    for i in range(s // tq):
        q_lo, q_hi = i * tq, i * tq + tq - 1
        first = 0 if mode == "slc" else max(0, (q_lo - (WINDOW - 1)) // tk)
        last = q_hi // tk
        for k in range(first, last + 1):
            qi.append(i)
            ki.append(k)
            fl.append((1 if k == first else 0) | (2 if k == last else 0))
    return (jnp.asarray(qi, jnp.int32), jnp.asarray(ki, jnp.int32), jnp.asarray(fl, jnp.int32))


def _flash(p_arr, vt, g3t, sel_t, mode):
    s = p_arr.shape[0]
    n_slc = s // SLC_BLOCK
    tq, tk, lookahead = (SLC_TQ, SLC_TK, SLC_LOOKAHEAD) if mode == "slc" else (WIN_TQ, WIN_TK, WIN_LOOKAHEAD)
    k_col, branch = (P_KS, 1) if mode == "slc" else (P_KW, 2)
    qi, ki, fl = _flash_tables(s, tq, tk, mode)
    in_specs = [
        pl.BlockSpec((tq, D_MODEL), lambda n, qi, ki, fl: (qi[n], P_Q // 8)),
        pl.BlockSpec((tk, LANES), lambda n, qi, ki, fl: (ki[n], k_col)),
        pl.BlockSpec((LANES, tk), lambda n, qi, ki, fl: (0, ki[n])),
        pl.BlockSpec((LANES, tq), lambda n, qi, ki, fl: (0, qi[n])),
    ]
    args = [p_arr, p_arr, vt, g3t]
    if mode == "slc":
        in_specs.append(pl.BlockSpec((N_GROUPS * n_slc, tq), lambda n, qi, ki, fl: (0, qi[n])))
        args.append(sel_t)
    grid_spec = pltpu.PrefetchScalarGridSpec(
        num_scalar_prefetch=3,
        grid=(qi.shape[0],),
        in_specs=in_specs,
        out_specs=pl.BlockSpec((D_MODEL, tq), lambda n, qi, ki, fl: (0, qi[n])),
        scratch_shapes=[
            pltpu.VMEM((N_HEADS, tq), F32),
            pltpu.VMEM((N_HEADS, ACC_ROWS, tq), F32),
        ],
    )
    return pl.pallas_call(
        functools.partial(_flash_kernel, mode=mode, n_slc=n_slc, branch=branch, lookahead=lookahead),
        grid_spec=grid_spec,
        out_shape=jax.ShapeDtypeStruct((D_MODEL, s), BF16),
        compiler_params=_params(("arbitrary",)),
        name="flash_" + mode,
    )(qi, ki, fl, *args)


OUT_TM = 512
PREV_ROWS = 16


def _out_kernel(x_ref, oc_ref, os_ref, ow_ref, cb_ref, cc_ref, cx_ref, pc_ref, px_ref, ga_ref, gb_ref,
                cw_ref, wa_ref, wb_ref, wo_ref, o_ref, u_ref):
    tm = x_ref.shape[0]
    o_nsa_t = oc_ref[...].astype(F32) + os_ref[...].astype(F32) + ow_ref[...].astype(F32)
    y_a = _dot(o_nsa_t.T.astype(BF16), wa_ref[...])

    prev = pc_ref[...].astype(F32) * px_ref[...].astype(F32)
    u_ref[0:PREV_ROWS, :] = jnp.where(pl.program_id(0) == 0, 0.0, prev)
    u_ref[PREV_ROWS:, :] = cc_ref[...].astype(F32) * cx_ref[...].astype(F32)
    cw = cw_ref[...]
    conv = (cw[0:1, :] * u_ref[pl.ds(PREV_ROWS - 2, tm), :]
            + cw[1:2, :] * u_ref[pl.ds(PREV_ROWS - 1, tm), :]
            + cw[2:3, :] * u_ref[pl.ds(PREV_ROWS, tm), :])
    y_b = _dot((cb_ref[...].astype(F32) * conv).astype(BF16), wb_ref[...])

    merged = _sigmoid(ga_ref[...].astype(F32)) * y_a + _sigmoid(gb_ref[...].astype(F32)) * y_b
    o_ref[...] = x_ref[...] + _dot(merged.astype(BF16), wo_ref[...])


def _out(x1, o_cmp_t, o_slc_t, o_win_t, p_arr, conv_w, w_nsa_out, w_conv_out, w_out):
    s = x1.shape[0]
    tm = OUT_TM
    row = lambda i: (i, 0)
    col = lambda i: (0, i)
    prev_blocks = tm // PREV_ROWS

    def seg(c):
        return pl.BlockSpec((tm, D_MODEL), lambda i: (i, c // 8))

    def seg_prev(c):
        return pl.BlockSpec((PREV_ROWS, D_MODEL), lambda i: (jnp.maximum(i * prev_blocks - 1, 0), c // 8))

    full = lambda shape: pl.BlockSpec(shape, lambda i: (0, 0))
    return pl.pallas_call(
        _out_kernel,
        grid=(s // tm,),
        in_specs=[
            pl.BlockSpec((tm, D_MODEL), row), pl.BlockSpec((D_MODEL, tm), col),
            pl.BlockSpec((D_MODEL, tm), col), pl.BlockSpec((D_MODEL, tm), col),
            seg(P_CB), seg(P_CC), seg(P_CX), seg_prev(P_CC), seg_prev(P_CX), seg(P_GA), seg(P_GB),
            full((CONV_K, CONV_WIDTH)), full((D_MODEL, D_MODEL)), full((CONV_WIDTH, D_MODEL)),
            full((D_MODEL, D_MODEL)),
        ],
        out_specs=pl.BlockSpec((tm, D_MODEL), row),
        out_shape=jax.ShapeDtypeStruct((s, D_MODEL), F32),
        scratch_shapes=[pltpu.VMEM((tm + PREV_ROWS, CONV_WIDTH), F32)],
        compiler_params=_params(("parallel",)),
        name="merge_out",
    )(x1, o_cmp_t, o_slc_t, o_win_t, p_arr, p_arr, p_arr, p_arr, p_arr, p_arr, p_arr,
      conv_w, w_nsa_out, w_conv_out, w_out)


def _layer(x, ffn1_norm, ffn1_w_gate, ffn1_w_up, ffn1_w_down, mix_norm, w_in, cmp_pe_k, cmp_pe_v,
           cmp_k_w1, cmp_k_w2, cmp_v_w1, cmp_v_w2, conv_w, w_nsa_out, w_conv_out, w_out,
           ffn2_norm, ffn2_w_gate, ffn2_w_up, ffn2_w_down, final_g, final_norm):
    s = x.shape[0]
    n_cmp = s // CMP_STRIDE
    n_slc = s // SLC_BLOCK
    row = lambda v: v.reshape(1, -1).astype(F32)

    x1 = _ffn(x, row(ffn1_norm), ffn1_w_gate.astype(BF16), ffn1_w_up.astype(BF16),
              ffn1_w_down.astype(BF16), row(final_g), False)
    p_arr = _proj(x1, row(mix_norm), _proj_weight(w_in))

    col = lambda c: p_arr[:, c * LANES:(c + 1) * LANES]
    chunks = lambda c: col(c).reshape(n_cmp, CMP_STRIDE * LANES)
    kc = _compress(chunks(P_KC), *_compress_weights(cmp_pe_k, cmp_k_w1, cmp_k_w2, False), False)
    vct = _compress(chunks(P_VC), *_compress_weights(cmp_pe_v, cmp_v_w1, cmp_v_w2, True), True)

    g3t = col(P_G3).T
    o_cmp_t, sel_t = _cmp_attention(p_arr, kc, vct, _overlap_matrix_t(n_cmp, n_slc), g3t)
    o_slc_t = _flash(p_arr, col(P_VS).T, g3t, sel_t, "slc")
    o_win_t = _flash(p_arr, col(P_VW).T, g3t, None, "win")
    x2 = _out(x1, o_cmp_t, o_slc_t, o_win_t, p_arr, conv_w.astype(F32), w_nsa_out.astype(BF16),
              w_conv_out.astype(BF16), w_out.astype(BF16))
    return _ffn(x2, row(ffn2_norm), ffn2_w_gate.astype(BF16), ffn2_w_up.astype(BF16),
                ffn2_w_down.astype(BF16), row(final_g), final_norm)


def kernel(x, ffn1_norm, ffn1_w_gate, ffn1_w_up, ffn1_w_down, mix_norm, w_in, cmp_pe_k, cmp_pe_v,
           cmp_k_w1, cmp_k_w2, cmp_v_w1, cmp_v_w2, conv_w, w_nsa_out, w_conv_out, w_out, ffn2_norm,
           ffn2_w_gate, ffn2_w_up, ffn2_w_down, final_norm):
    batch, _, _ = x.shape
    depth = ffn1_norm.shape[0]
    outs = []
    for b in range(batch):
        xb = x[b]
        for l in range(depth):
            xb = _layer(xb, ffn1_norm[l], ffn1_w_gate[l], ffn1_w_up[l], ffn1_w_down[l], mix_norm[l],
                        w_in[l], cmp_pe_k[l], cmp_pe_v[l], cmp_k_w1[l], cmp_k_w2[l], cmp_v_w1[l],
                        cmp_v_w2[l], conv_w[l], w_nsa_out[l], w_conv_out[l], w_out[l], ffn2_norm[l],
                        ffn2_w_gate[l], ffn2_w_up[l], ffn2_w_down[l], final_norm, l == depth - 1)
        outs.append(xb)
    return jnp.stack(outs)
```

```python
import functools

import jax
import jax.numpy as jnp
import numpy as np
from jax import lax
from jax.experimental import pallas as pl
from jax.experimental.pallas import tpu as pltpu

D_MODEL = 1024
N_HEADS = 16
HEAD_DIM = 64
N_GROUPS = 2
HEADS_PER_GROUP = N_HEADS // N_GROUPS
N_PAIRS = N_HEADS // 2
PAIRS_PER_GROUP = N_PAIRS // N_GROUPS
CMP_BLOCK = 32
CMP_STRIDE = 16
CMP_HIDDEN = 256
SLC_BLOCK = 64
N_SELECT = 16
WINDOW = 512
CONV_WIDTH = 1024
CONV_K = 3
D_FF = 2816
EPS = 1e-6
NEG = -(2.0 ** 100)
N_FORCED = 3
LOG2E = float(np.log2(np.e))
LANES = 128
BF16_ROWS = 16

P_Q, P_CB, P_CC, P_CX, P_GA, P_GB = 0, 8, 16, 24, 32, 40
P_KC, P_VC, P_KS, P_VS, P_KW, P_VW, P_G3 = 48, 49, 50, 51, 52, 53, 54
P_BLOCKS = 56
P_COLS = P_BLOCKS * LANES

F32 = jnp.float32
BF16 = jnp.bfloat16

VMEM_LIMIT = 52 * 1024 * 1024


def _dot(a, b):
    return jnp.dot(a, b, preferred_element_type=F32)


def _dot_nt(a, b):
    return lax.dot_general(a, b, (((1,), (1,)), ((), ())), preferred_element_type=F32)


def _sigmoid(x):
    return 1.0 / (1.0 + jnp.exp(-x))


def _rms(x, g):
    return x * lax.rsqrt(jnp.mean(x * x, axis=-1, keepdims=True) + EPS) * g


def _params(sem, limit=VMEM_LIMIT, flags=None):
    return pltpu.CompilerParams(dimension_semantics=sem, vmem_limit_bytes=limit, flags=flags)


FFN_TM = 512
FFN_CHUNK = 256


def _ffn_kernel(x_ref, g_ref, wg_ref, wu_ref, wd_ref, fg_ref, o_ref, *, final_norm):
    x = x_ref[...]
    h = _rms(x, g_ref[...]).astype(BF16)
    acc = None
    for c in range(D_FF // FFN_CHUNK):
        cols = slice(c * FFN_CHUNK, (c + 1) * FFN_CHUNK)
        gate = _dot(h, wg_ref[:, cols])
        up = _dot(h, wu_ref[:, cols])
        a = (gate * _sigmoid(gate) * up).astype(BF16)
        part = _dot(a, wd_ref[cols, :])
        acc = part if acc is None else acc + part
    y = x + 0.5 * acc
    if final_norm:
        y = _rms(y, fg_ref[...])
    o_ref[...] = y


def _ffn(x, norm_g, wg, wu, wd, final_g, final_norm):
    s = x.shape[0]
    resident = lambda shape: pl.BlockSpec(shape, lambda i: (0, 0), pipeline_mode=pl.Buffered(1))
    return pl.pallas_call(
        functools.partial(_ffn_kernel, final_norm=final_norm),
        grid=(s // FFN_TM,),
        in_specs=[
            pl.BlockSpec((FFN_TM, D_MODEL), lambda i: (i, 0)),
            resident((1, D_MODEL)),
            resident((D_MODEL, D_FF)),
            resident((D_MODEL, D_FF)),
            resident((D_FF, D_MODEL)),
            resident((1, D_MODEL)),
        ],
        out_specs=pl.BlockSpec((FFN_TM, D_MODEL), lambda i: (i, 0)),
        out_shape=jax.ShapeDtypeStruct((s, D_MODEL), F32),
        compiler_params=_params(("parallel",)),
        name="ffn_final" if final_norm else "ffn",
    )(x, norm_g, wg, wu, wd, final_g)


PROJ_TM = 1024
PROJ_TN = 1024


def _proj_kernel(x_ref, g_ref, w_ref, o_ref, h_ref):
    @pl.when(pl.program_id(1) == 0)
    def _():
        h_ref[...] = _rms(x_ref[...], g_ref[...]).astype(BF16)

    o_ref[...] = _dot(h_ref[...], w_ref[...]).astype(BF16)


def _proj(x, norm_g, w):
    s = x.shape[0]
    grid = (s // PROJ_TM, P_COLS // PROJ_TN)
    return pl.pallas_call(
        _proj_kernel,
        grid=grid,
        in_specs=[
            pl.BlockSpec((PROJ_TM, D_MODEL), lambda i, j: (i, 0)),
            pl.BlockSpec((1, D_MODEL), lambda i, j: (0, 0)),
            pl.BlockSpec((D_MODEL, PROJ_TN), lambda i, j: (0, j)),
        ],
        out_specs=pl.BlockSpec((PROJ_TM, PROJ_TN), lambda i, j: (i, j)),
        out_shape=jax.ShapeDtypeStruct((s, P_COLS), BF16),
        scratch_shapes=[pltpu.VMEM((PROJ_TM, D_MODEL), BF16)],
        compiler_params=_params(("parallel", "arbitrary")),
        name="proj",
    )(x, norm_g, w)


def _proj_weight(w_in):
    q, kc, vc, ks, vs, kw, vw, g3, cb, cc, cx, ga, gb = jnp.split(
        w_in, np.cumsum([1024, 128, 128, 128, 128, 128, 128, 48, 1024, 1024, 1024, 1024]).tolist(), axis=1)
    g3 = jnp.pad(g3, ((0, 0), (0, (P_BLOCKS - P_G3) * LANES - g3.shape[1])))
    w = jnp.concatenate([q * (HEAD_DIM ** -0.5 * LOG2E), cb, cc, cx, ga, gb, kc, vc, ks, vs, kw, vw, g3], axis=1)
    return w.astype(BF16)


def _gelu_tanh(x):
    return 0.5 * x * (1.0 + jnp.tanh(np.sqrt(2.0 / np.pi).astype(np.float32) * (x + 0.044715 * (x * x * x))))


def _compress_kernel(cf_ref, pe_ref, w1_ref, w2_ref, o_ref, *, transposed):
    cf = cf_ref[...].astype(F32)
    top = (cf + pe_ref[0]).astype(BF16)
    bot = (cf + pe_ref[1]).astype(BF16)
    a = _dot(top, w1_ref[0, 0])
    b = _dot(bot, w1_ref[0, 1])
    n = a.shape[0]
    pre = a + pltpu.roll(b, n - 1, 0)
    h = _gelu_tanh(pre).astype(BF16)
    if transposed:
        o_ref[0] = _dot_nt(w2_ref[...], h).astype(BF16)
    else:
        o_ref[0, 0] = _dot(h, w2_ref[0]).astype(BF16)
        o_ref[0, 1] = _dot(h, w2_ref[1]).astype(BF16)


def _compress(cf, pe, w1, w2, transposed):
    n = cf.shape[0]
    width = CMP_STRIDE * LANES
    if transposed:
        w2_spec = pl.BlockSpec((HEAD_DIM, CMP_HIDDEN), lambda g: (0, 0))
        out_spec = pl.BlockSpec((1, HEAD_DIM, n), lambda g: (g, 0, 0))
        out_shape = jax.ShapeDtypeStruct((N_GROUPS, HEAD_DIM, n), BF16)
    else:
        w2_spec = pl.BlockSpec((2, CMP_HIDDEN, LANES), lambda g: (0, 0, 0))
        out_spec = pl.BlockSpec((1, 2, n, LANES), lambda g: (g, 0, 0, 0))
        out_shape = jax.ShapeDtypeStruct((N_GROUPS, 2, n, LANES), BF16)
    return pl.pallas_call(
        functools.partial(_compress_kernel, transposed=transposed),
        grid=(N_GROUPS,),
        in_specs=[
            pl.BlockSpec((n, width), lambda g: (0, 0)),
            pl.BlockSpec((2, 1, width), lambda g: (0, 0, 0)),
            pl.BlockSpec((1, 2, width, CMP_HIDDEN), lambda g: (g, 0, 0, 0)),
            w2_spec,
        ],
        out_specs=out_spec,
        out_shape=out_shape,
        compiler_params=_params(("parallel",)),
        name="compress_v" if transposed else "compress_k",
    )(cf, pe, w1, w2)


def _compress_weights(pe, w1, w2, transposed):
    w1h = w1.reshape(2, CMP_STRIDE, HEAD_DIM, CMP_HIDDEN)
    zeros = jnp.zeros_like(w1h)
    w1g = jnp.stack([jnp.concatenate([w1h, zeros], axis=2), jnp.concatenate([zeros, w1h], axis=2)])
    w1g = w1g.reshape(N_GROUPS, 2, CMP_STRIDE * LANES, CMP_HIDDEN).astype(BF16)
    peh = pe.reshape(2, CMP_STRIDE, HEAD_DIM)
    pe2 = jnp.concatenate([peh, peh], axis=2).reshape(2, 1, CMP_STRIDE * LANES)
    if transposed:
        w2p = w2.T.astype(BF16)
    else:
        z2 = jnp.zeros_like(w2)
        w2p = jnp.stack([jnp.concatenate([w2, z2], axis=1), jnp.concatenate([z2, w2], axis=1)]).astype(BF16)
    return pe2, w1g, w2p


def _split_heads(kt):
    lo = lax.broadcasted_iota(jnp.int32, kt.shape, 1) < HEAD_DIM
    kr = pltpu.roll(kt, HEAD_DIM, 1)
    zero = jnp.zeros_like(kt)
    return [[jnp.where(lo, kt, zero), jnp.where(lo, zero, kr)],
            [jnp.where(lo, kr, zero), jnp.where(lo, zero, kt)]]


QSUB = 256
CMP_TQ = 256
CMP_KEY_CHUNK = 256
CMP_LOOKAHEAD = 4


def _cmp_kernel(q_ref, kc_ref, vct_ref, ovt_ref, g3t_ref, o_ref, sel_ref, *, n_slc):
    tq = q_ref.shape[0]
    n_cmp = kc_ref.shape[2]
    q0 = pl.program_id(0) * tq
    t = q0 + lax.broadcasted_iota(jnp.int32, (1, tq), 1)
    any_valid = t >= CMP_BLOCK - 1
    cur = t // SLC_BLOCK

    def body(n_keys, n_blk):
        c_end = lax.broadcasted_iota(jnp.int32, (n_keys, 1), 0) * CMP_STRIDE + (CMP_BLOCK - 1)
        bias = jnp.where(c_end <= t, 0.0, NEG)
        gates = _sigmoid(g3t_ref[...].astype(F32))

        ovt = ovt_ref[:n_blk, :n_keys]
        ones = jnp.ones((BF16_ROWS, n_keys), BF16)
        lhs = [jnp.concatenate([vct_ref[g, :, :n_keys], ones, ovt], axis=0) for g in range(N_GROUPS)]
        imp_row = HEAD_DIM + BF16_ROWS
        blk = lax.broadcasted_iota(jnp.int32, (n_blk, QSUB), 0)
        blk_f = blk.astype(F32)

        def select(cols, importance):
            forced = (blk == 0) | (blk == cur[:, cols]) | (blk == cur[:, cols] - 1)
            causal = blk * SLC_BLOCK <= t[:, cols]
            for g in range(N_GROUPS):
                work = jnp.where(causal & ~forced, importance[g], -1.0)
                sel = jnp.where(forced, 1.0, 0.0)
                for _ in range(N_SELECT - N_FORCED):
                    m = jnp.max(work, axis=0, keepdims=True)
                    first = jnp.min(jnp.where(work == m, blk_f, float(n_blk)), axis=0, keepdims=True)
                    hit = blk_f == first
                    sel = jnp.where(hit, 1.0, sel)
                    work = jnp.where(hit, -jnp.inf, work)
                sel_ref[g * n_slc:g * n_slc + n_blk, cols] = jnp.where(sel > 0.0, 0.0, NEG)
                if n_blk < n_slc:
                    sel_ref[g * n_slc + n_blk:(g + 1) * n_slc, cols] = jnp.full((n_slc - n_blk, QSUB), NEG, F32)

        units = [(h, slice(qs * QSUB, (qs + 1) * QSUB)) for qs in range(tq // QSUB) for h in range(N_HEADS)]

        def scores(unit):
            h, cols = unit
            p, e = divmod(h, 2)
            g = p // PAIRS_PER_GROUP
            s = _dot_nt(kc_ref[g, e, :n_keys, :], q_ref[cols, p * LANES:(p + 1) * LANES]) + bias[:, cols]
            return s, jnp.max(s, axis=0, keepdims=True)

        importance = [None] * N_GROUPS
        pending = [scores(u) for u in units[:CMP_LOOKAHEAD]]
        for n, (h, cols) in enumerate(units):
            g = h // HEADS_PER_GROUP
            if n + CMP_LOOKAHEAD < len(units):
                pending.append(scores(units[n + CMP_LOOKAHEAD]))
            s, m = pending.pop(0)
            pr = jnp.exp2(s - m).astype(BF16)
            res = _dot(lhs[g], pr)
            inv_l = jnp.where(any_valid[:, cols], 1.0 / res[HEAD_DIM:HEAD_DIM + 1, :], 0.0)
            o_h = res[:HEAD_DIM, :] * (inv_l * gates[3 * h:3 * h + 1, cols])
            o_ref[h * HEAD_DIM:(h + 1) * HEAD_DIM, cols] = o_h.astype(BF16)
            imp_h = res[imp_row:, :] * inv_l
            importance[g] = imp_h if importance[g] is None else importance[g] + imp_h
            if h == N_HEADS - 1:
                select(cols, importance)
                importance = [None] * N_GROUPS

    chunk_tokens = CMP_KEY_CHUNK * CMP_STRIDE
    variant = (q0 + tq - 1) // chunk_tokens
    for v in range(n_cmp // CMP_KEY_CHUNK):
        pl.when(variant == v)(functools.partial(body, (v + 1) * CMP_KEY_CHUNK,
                                                (v + 1) * chunk_tokens // SLC_BLOCK))


def _cmp_attention(p_arr, kc, vct, overlap_t, g3t):
    s = p_arr.shape[0]
    n_cmp = kc.shape[2]
    n_slc = s // SLC_BLOCK
    return pl.pallas_call(
        functools.partial(_cmp_kernel, n_slc=n_slc),
        grid=(s // CMP_TQ,),
        in_specs=[
            pl.BlockSpec((CMP_TQ, D_MODEL), lambda i: (i, P_Q // 8)),
            pl.BlockSpec((N_GROUPS, 2, n_cmp, LANES), lambda i: (0, 0, 0, 0)),
            pl.BlockSpec((N_GROUPS, HEAD_DIM, n_cmp), lambda i: (0, 0, 0)),
            pl.BlockSpec((n_slc, n_cmp), lambda i: (0, 0)),
            pl.BlockSpec((LANES, CMP_TQ), lambda i: (0, i)),
        ],
        out_specs=[
            pl.BlockSpec((D_MODEL, CMP_TQ), lambda i: (0, i)),
            pl.BlockSpec((N_GROUPS * n_slc, CMP_TQ), lambda i: (0, i)),
        ],
        out_shape=[
            jax.ShapeDtypeStruct((D_MODEL, s), BF16),
            jax.ShapeDtypeStruct((N_GROUPS * n_slc, s), F32),
        ],
        compiler_params=_params(("parallel",)),
        name="cmp_attention",
    )(p_arr, kc, vct, overlap_t, g3t)


def _overlap_matrix_t(n_cmp, n_slc):
    c0 = np.arange(n_cmp)[None, :] * CMP_STRIDE
    s0 = np.arange(n_slc)[:, None] * SLC_BLOCK
    ov = (c0 < s0 + SLC_BLOCK) & (c0 + CMP_BLOCK > s0)
    return jnp.asarray(ov, dtype=BF16)


SLC_TQ = 1024
SLC_TK = 512
SLC_LOOKAHEAD = 4
WIN_TQ = 256
WIN_TK = 256
WIN_LOOKAHEAD = 6
ACC_ROWS = HEAD_DIM + BF16_ROWS


def _flash_kernel(qi_ref, ki_ref, fl_ref, *refs, mode, n_slc, branch, lookahead):
    if mode == "slc":
        q_ref, k_ref, vt_ref, g3t_ref, sel_ref, o_ref, m_ref, acc_ref = refs
    else:
        q_ref, k_ref, vt_ref, g3t_ref, o_ref, m_ref, acc_ref = refs
    step = pl.program_id(0)
    tq = q_ref.shape[0]
    tk = k_ref.shape[0]
    flags = fl_ref[step]

    @pl.when((flags & 1) != 0)
    def _():
        m_ref[...] = jnp.full_like(m_ref, NEG)
        acc_ref[...] = jnp.zeros_like(acc_ref)

    q0 = qi_ref[step] * tq
    k0 = ki_ref[step] * tk
    t = q0 + lax.broadcasted_iota(jnp.int32, (1, tq), 1)
    key = k0 + lax.broadcasted_iota(jnp.int32, (tk, 1), 0)

    if mode == "slc":
        tile_blocks = tk // SLC_BLOCK
        pad_blocks = BF16_ROWS - tile_blocks % BF16_ROWS
        key_blk = lax.broadcasted_iota(jnp.int32, (tk, tile_blocks + pad_blocks), 0) // SLC_BLOCK
        blk = lax.broadcasted_iota(jnp.int32, (tk, tile_blocks + pad_blocks), 1)
        expand = (blk == key_blk).astype(BF16)
        causal_bias = jnp.where(key <= t, 0.0, NEG)
        bias = []
        for g in range(N_GROUPS):
            first_block = pl.multiple_of(g * n_slc + ki_ref[step] * tile_blocks, tile_blocks)
            rows = jnp.concatenate([sel_ref[pl.ds(first_block, tile_blocks), :],
                                    jnp.zeros((pad_blocks, tq), F32)], axis=0)
            bias.append(jnp.minimum(_dot(expand, rows.astype(BF16)), causal_bias))
    else:
        d = t - key
        band = jnp.where((d >= 0) & (d < WINDOW), 0.0, NEG)
        bias = [band] * N_GROUPS

    k_half = _split_heads(k_ref[...])
    vt = vt_ref[...]
    ones = jnp.ones((ACC_ROWS - HEAD_DIM, tk), BF16)
    v_aug = [jnp.concatenate([vt[g * HEAD_DIM:(g + 1) * HEAD_DIM, :], ones], axis=0) for g in range(N_GROUPS)]

    units = [(h, slice(qs * QSUB, (qs + 1) * QSUB)) for qs in range(tq // QSUB) for h in range(N_HEADS)]

    def scores(unit):
        h, cols = unit
        p, e = divmod(h, 2)
        g = p // PAIRS_PER_GROUP
        s = (_dot_nt(k_half[g][e], q_ref[cols, p * LANES:(p + 1) * LANES]) + bias[g][:, cols]).astype(BF16)
        return s, jnp.max(s, axis=0, keepdims=True)

    pending = [scores(u) for u in units[:lookahead]]
    for n, (h, cols) in enumerate(units):
        g = h // HEADS_PER_GROUP
        if n + lookahead < len(units):
            pending.append(scores(units[n + lookahead]))
        s, s_max = pending.pop(0)
        m_prev = m_ref[h:h + 1, cols]
        m_new = jnp.maximum(m_prev, s_max.astype(F32))
        alpha = jnp.exp2(m_prev - m_new)
        pr = jnp.exp2(s - m_new.astype(BF16))
        m_ref[h:h + 1, cols] = m_new
        acc_ref[h, :, cols] = acc_ref[h, :, cols] * alpha + _dot(v_aug[g], pr)

    @pl.when((flags & 2) != 0)
    def _():
        gates = _sigmoid(g3t_ref[...].astype(F32))
        for h in range(N_HEADS):
            acc = acc_ref[h]
            scale = gates[3 * h + branch:3 * h + branch + 1, :] / acc[HEAD_DIM:HEAD_DIM + 1, :]
            o_ref[h * HEAD_DIM:(h + 1) * HEAD_DIM, :] = (acc[:HEAD_DIM, :] * scale).astype(BF16)


def _flash_tables(s, tq, tk, mode):
    qi, ki, fl = [], [], []
    for i in range(s // tq):
        q_lo, q_hi = i * tq, i * tq + tq - 1
        first = 0 if mode == "slc" else max(0, (q_lo - (WINDOW - 1)) // tk)
        last = q_hi // tk
        for k in range(first, last + 1):
            qi.append(i)
            ki.append(k)
            fl.append((1 if k == first else 0) | (2 if k == last else 0))
    return (jnp.asarray(qi, jnp.int32), jnp.asarray(ki, jnp.int32), jnp.asarray(fl, jnp.int32))


def _flash(p_arr, vt, g3t, sel_t, mode):
    s = p_arr.shape[0]
    n_slc = s // SLC_BLOCK
    tq, tk, lookahead = (SLC_TQ, SLC_TK, SLC_LOOKAHEAD) if mode == "slc" else (WIN_TQ, WIN_TK, WIN_LOOKAHEAD)
    k_col, branch = (P_KS, 1) if mode == "slc" else (P_KW, 2)
    qi, ki, fl = _flash_tables(s, tq, tk, mode)
    in_specs = [
        pl.BlockSpec((tq, D_MODEL), lambda n, qi, ki, fl: (qi[n], P_Q // 8)),
        pl.BlockSpec((tk, LANES), lambda n, qi, ki, fl: (ki[n], k_col)),
        pl.BlockSpec((LANES, tk), lambda n, qi, ki, fl: (0, ki[n])),
        pl.BlockSpec((LANES, tq), lambda n, qi, ki, fl: (0, qi[n])),
    ]
    args = [p_arr, p_arr, vt, g3t]
    if mode == "slc":
        in_specs.append(pl.BlockSpec((N_GROUPS * n_slc, tq), lambda n, qi, ki, fl: (0, qi[n])))
        args.append(sel_t)
    grid_spec = pltpu.PrefetchScalarGridSpec(
        num_scalar_prefetch=3,
        grid=(qi.shape[0],),
        in_specs=in_specs,
        out_specs=pl.BlockSpec((D_MODEL, tq), lambda n, qi, ki, fl: (0, qi[n])),
        scratch_shapes=[
            pltpu.VMEM((N_HEADS, tq), F32),
            pltpu.VMEM((N_HEADS, ACC_ROWS, tq), F32),
        ],
    )
    return pl.pallas_call(
        functools.partial(_flash_kernel, mode=mode, n_slc=n_slc, branch=branch, lookahead=lookahead),
        grid_spec=grid_spec,
        out_shape=jax.ShapeDtypeStruct((D_MODEL, s), BF16),
        compiler_params=_params(("arbitrary",)),
        name="flash_" + mode,
    )(qi, ki, fl, *args)


OUT_TM = 512
PREV_ROWS = 16


def _out_kernel(x_ref, oc_ref, os_ref, ow_ref, cb_ref, cc_ref, cx_ref, pc_ref, px_ref, ga_ref, gb_ref,
                cw_ref, wa_ref, wb_ref, wo_ref, o_ref, u_ref):
    tm = x_ref.shape[0]
    o_nsa_t = oc_ref[...].astype(F32) + os_ref[...].astype(F32) + ow_ref[...].astype(F32)
    y_a = _dot(o_nsa_t.T.astype(BF16), wa_ref[...])

    prev = pc_ref[...].astype(F32) * px_ref[...].astype(F32)
    u_ref[0:PREV_ROWS, :] = jnp.where(pl.program_id(0) == 0, 0.0, prev)
    u_ref[PREV_ROWS:, :] = cc_ref[...].astype(F32) * cx_ref[...].astype(F32)
    cw = cw_ref[...]
    conv = (cw[0:1, :] * u_ref[pl.ds(PREV_ROWS - 2, tm), :]
            + cw[1:2, :] * u_ref[pl.ds(PREV_ROWS - 1, tm), :]
            + cw[2:3, :] * u_ref[pl.ds(PREV_ROWS, tm), :])
    y_b = _dot((cb_ref[...].astype(F32) * conv).astype(BF16), wb_ref[...])

    merged = _sigmoid(ga_ref[...].astype(F32)) * y_a + _sigmoid(gb_ref[...].astype(F32)) * y_b
    o_ref[...] = x_ref[...] + _dot(merged.astype(BF16), wo_ref[...])


def _out(x1, o_cmp_t, o_slc_t, o_win_t, p_arr, conv_w, w_nsa_out, w_conv_out, w_out):
    s = x1.shape[0]
    tm = OUT_TM
    row = lambda i: (i, 0)
    col = lambda i: (0, i)
    prev_blocks = tm // PREV_ROWS

    def seg(c):
        return pl.BlockSpec((tm, D_MODEL), lambda i: (i, c // 8))

    def seg_prev(c):
        return pl.BlockSpec((PREV_ROWS, D_MODEL), lambda i: (jnp.maximum(i * prev_blocks - 1, 0), c // 8))

    full = lambda shape: pl.BlockSpec(shape, lambda i: (0, 0))
    return pl.pallas_call(
        _out_kernel,
        grid=(s // tm,),
        in_specs=[
            pl.BlockSpec((tm, D_MODEL), row), pl.BlockSpec((D_MODEL, tm), col),
            pl.BlockSpec((D_MODEL, tm), col), pl.BlockSpec((D_MODEL, tm), col),
            seg(P_CB), seg(P_CC), seg(P_CX), seg_prev(P_CC), seg_prev(P_CX), seg(P_GA), seg(P_GB),
            full((CONV_K, CONV_WIDTH)), full((D_MODEL, D_MODEL)), full((CONV_WIDTH, D_MODEL)),
            full((D_MODEL, D_MODEL)),
        ],
        out_specs=pl.BlockSpec((tm, D_MODEL), row),
        out_shape=jax.ShapeDtypeStruct((s, D_MODEL), F32),
        scratch_shapes=[pltpu.VMEM((tm + PREV_ROWS, CONV_WIDTH), F32)],
        compiler_params=_params(("parallel",)),
        name="merge_out",
    )(x1, o_cmp_t, o_slc_t, o_win_t, p_arr, p_arr, p_arr, p_arr, p_arr, p_arr, p_arr,
      conv_w, w_nsa_out, w_conv_out, w_out)


def _layer(x, ffn1_norm, ffn1_w_gate, ffn1_w_up, ffn1_w_down, mix_norm, w_in, cmp_pe_k, cmp_pe_v,
           cmp_k_w1, cmp_k_w2, cmp_v_w1, cmp_v_w2, conv_w, w_nsa_out, w_conv_out, w_out,
           ffn2_norm, ffn2_w_gate, ffn2_w_up, ffn2_w_down, final_g, final_norm):
    s = x.shape[0]
    n_cmp = s // CMP_STRIDE
    n_slc = s // SLC_BLOCK
    row = lambda v: v.reshape(1, -1).astype(F32)

    x1 = _ffn(x, row(ffn1_norm), ffn1_w_gate.astype(BF16), ffn1_w_up.astype(BF16),
              ffn1_w_down.astype(BF16), row(final_g), False)
    p_arr = _proj(x1, row(mix_norm), _proj_weight(w_in))

    col = lambda c: p_arr[:, c * LANES:(c + 1) * LANES]
    chunks = lambda c: col(c).reshape(n_cmp, CMP_STRIDE * LANES)
    kc = _compress(chunks(P_KC), *_compress_weights(cmp_pe_k, cmp_k_w1, cmp_k_w2, False), False)
    vct = _compress(chunks(P_VC), *_compress_weights(cmp_pe_v, cmp_v_w1, cmp_v_w2, True), True)

    g3t = col(P_G3).T
    o_cmp_t, sel_t = _cmp_attention(p_arr, kc, vct, _overlap_matrix_t(n_cmp, n_slc), g3t)
    o_slc_t = _flash(p_arr, col(P_VS).T, g3t, sel_t, "slc")
    o_win_t = _flash(p_arr, col(P_VW).T, g3t, None, "win")
    x2 = _out(x1, o_cmp_t, o_slc_t, o_win_t, p_arr, conv_w.astype(F32), w_nsa_out.astype(BF16),
              w_conv_out.astype(BF16), w_out.astype(BF16))
    return _ffn(x2, row(ffn2_norm), ffn2_w_gate.astype(BF16), ffn2_w_up.astype(BF16),
                ffn2_w_down.astype(BF16), row(final_g), final_norm)


def kernel(x, ffn1_norm, ffn1_w_gate, ffn1_w_up, ffn1_w_down, mix_norm, w_in, cmp_pe_k, cmp_pe_v,
           cmp_k_w1, cmp_k_w2, cmp_v_w1, cmp_v_w2, conv_w, w_nsa_out, w_conv_out, w_out, ffn2_norm,
           ffn2_w_gate, ffn2_w_up, ffn2_w_down, final_norm):
    batch, _, _ = x.shape
    depth = ffn1_norm.shape[0]
    outs = []
    for b in range(batch):
        xb = x[b]
        for l in range(depth):
            xb = _layer(xb, ffn1_norm[l], ffn1_w_gate[l], ffn1_w_up[l], ffn1_w_down[l], mix_norm[l],
                        w_in[l], cmp_pe_k[l], cmp_pe_v[l], cmp_k_w1[l], cmp_k_w2[l], cmp_v_w1[l],
                        cmp_v_w2[l], conv_w[l], w_nsa_out[l], w_conv_out[l], w_out[l], ffn2_norm[l],
                        ffn2_w_gate[l], ffn2_w_up[l], ffn2_w_down[l], final_norm, l == depth - 1)
        outs.append(xb)
    return jnp.stack(outs)
```

```python
import functools

import jax
import jax.numpy as jnp
import numpy as np
from jax import lax
from jax.experimental import pallas as pl
from jax.experimental.pallas import tpu as pltpu

D_MODEL = 1024
N_HEADS = 16
HEAD_DIM = 64
N_GROUPS = 2
HEADS_PER_GROUP = N_HEADS // N_GROUPS
N_PAIRS = N_HEADS // 2
PAIRS_PER_GROUP = N_PAIRS // N_GROUPS
CMP_BLOCK = 32
CMP_STRIDE = 16
CMP_HIDDEN = 256
SLC_BLOCK = 64
N_SELECT = 16
WINDOW = 512
CONV_WIDTH = 1024
CONV_K = 3
D_FF = 2816
EPS = 1e-6
NEG = -(2.0 ** 100)
N_FORCED = 3
LOG2E = float(np.log2(np.e))
LANES = 128
BF16_ROWS = 16

P_Q, P_CB, P_CC, P_CX, P_GA, P_GB = 0, 8, 16, 24, 32, 40
P_KC, P_VC, P_KS, P_VS, P_KW, P_VW, P_G3 = 48, 49, 50, 51, 52, 53, 54
P_BLOCKS = 56
P_COLS = P_BLOCKS * LANES

F32 = jnp.float32
BF16 = jnp.bfloat16

VMEM_LIMIT = 52 * 1024 * 1024


def _dot(a, b):
    return jnp.dot(a, b, preferred_element_type=F32)


def _dot_nt(a, b):
    return lax.dot_general(a, b, (((1,), (1,)), ((), ())), preferred_element_type=F32)


def _sigmoid(x):
    return 1.0 / (1.0 + jnp.exp(-x))


def _rms(x, g):
    return x * lax.rsqrt(jnp.mean(x * x, axis=-1, keepdims=True) + EPS) * g


def _params(sem, limit=VMEM_LIMIT, flags=None):
    return pltpu.CompilerParams(dimension_semantics=sem, vmem_limit_bytes=limit, flags=flags)


FFN_TM = 512
FFN_CHUNK = 256


def _ffn_kernel(x_ref, g_ref, wg_ref, wu_ref, wd_ref, fg_ref, o_ref, *, final_norm):
    x = x_ref[...]
    h = _rms(x, g_ref[...]).astype(BF16)
    acc = None
    for c in range(D_FF // FFN_CHUNK):
        cols = slice(c * FFN_CHUNK, (c + 1) * FFN_CHUNK)
        gate = _dot(h, wg_ref[:, cols])
        up = _dot(h, wu_ref[:, cols])
        a = (gate * _sigmoid(gate) * up).astype(BF16)
        part = _dot(a, wd_ref[cols, :])
        acc = part if acc is None else acc + part
    y = x + 0.5 * acc
    if final_norm:
        y = _rms(y, fg_ref[...])
    o_ref[...] = y


def _ffn(x, norm_g, wg, wu, wd, final_g, final_norm):
    s = x.shape[0]
    resident = lambda shape: pl.BlockSpec(shape, lambda i: (0, 0), pipeline_mode=pl.Buffered(1))
    return pl.pallas_call(
        functools.partial(_ffn_kernel, final_norm=final_norm),
        grid=(s // FFN_TM,),
        in_specs=[
            pl.BlockSpec((FFN_TM, D_MODEL), lambda i: (i, 0)),
            resident((1, D_MODEL)),
            resident((D_MODEL, D_FF)),
            resident((D_MODEL, D_FF)),
            resident((D_FF, D_MODEL)),
            resident((1, D_MODEL)),
        ],
        out_specs=pl.BlockSpec((FFN_TM, D_MODEL), lambda i: (i, 0)),
        out_shape=jax.ShapeDtypeStruct((s, D_MODEL), F32),
        compiler_params=_params(("parallel",)),
        name="ffn_final" if final_norm else "ffn",
    )(x, norm_g, wg, wu, wd, final_g)


PROJ_TM = 512
PROJ_CHUNK = 512


def _proj_kernel(x_ref, g_ref, w_ref, o_ref):
    h = _rms(x_ref[...], g_ref[...]).astype(BF16)
    for c in range(P_COLS // PROJ_CHUNK):
        cols = slice(c * PROJ_CHUNK, (c + 1) * PROJ_CHUNK)
        o_ref[:, cols] = _dot(h, w_ref[:, cols]).astype(BF16)


def _proj(x, norm_g, w):
    s = x.shape[0]
    resident = lambda shape: pl.BlockSpec(shape, lambda i: (0, 0), pipeline_mode=pl.Buffered(1))
    return pl.pallas_call(
        _proj_kernel,
        grid=(s // PROJ_TM,),
        in_specs=[
            pl.BlockSpec((PROJ_TM, D_MODEL), lambda i: (i, 0)),
            resident((1, D_MODEL)),
            resident((D_MODEL, P_COLS)),
        ],
        out_specs=pl.BlockSpec((PROJ_TM, P_COLS), lambda i: (i, 0)),
        out_shape=jax.ShapeDtypeStruct((s, P_COLS), BF16),
        compiler_params=_params(("parallel",)),
        name="proj",
    )(x, norm_g, w)


def _proj_weight(w_in):
    q, kc, vc, ks, vs, kw, vw, g3, cb, cc, cx, ga, gb = jnp.split(
        w_in, np.cumsum([1024, 128, 128, 128, 128, 128, 128, 48, 1024, 1024, 1024, 1024]).tolist(), axis=1)
    g3 = jnp.pad(g3, ((0, 0), (0, (P_BLOCKS - P_G3) * LANES - g3.shape[1])))
    w = jnp.concatenate([q * (HEAD_DIM ** -0.5 * LOG2E), cb, cc, cx, ga, gb, kc, vc, ks, vs, kw, vw, g3], axis=1)
    return w.astype(BF16)


def _gelu_tanh(x):
    return 0.5 * x * (1.0 + jnp.tanh(np.sqrt(2.0 / np.pi).astype(np.float32) * (x + 0.044715 * (x * x * x))))


def _compress_kernel(cf_ref, pe_ref, w1_ref, w2_ref, o_ref, *, transposed):
    cf = cf_ref[...].astype(F32)
    top = (cf + pe_ref[0]).astype(BF16)
    bot = (cf + pe_ref[1]).astype(BF16)
    a = _dot(top, w1_ref[0, 0])
    b = _dot(bot, w1_ref[0, 1])
    n = a.shape[0]
    pre = a + pltpu.roll(b, n - 1, 0)
    h = _gelu_tanh(pre).astype(BF16)
    if transposed:
        o_ref[0] = _dot_nt(w2_ref[...], h).astype(BF16)
    else:
        o_ref[0, 0] = _dot(h, w2_ref[0]).astype(BF16)
        o_ref[0, 1] = _dot(h, w2_ref[1]).astype(BF16)


def _compress(cf, pe, w1, w2, transposed):
    n = cf.shape[0]
    width = CMP_STRIDE * LANES
    if transposed:
        w2_spec = pl.BlockSpec((HEAD_DIM, CMP_HIDDEN), lambda g: (0, 0))
        out_spec = pl.BlockSpec((1, HEAD_DIM, n), lambda g: (g, 0, 0))
        out_shape = jax.ShapeDtypeStruct((N_GROUPS, HEAD_DIM, n), BF16)
    else:
        w2_spec = pl.BlockSpec((2, CMP_HIDDEN, LANES), lambda g: (0, 0, 0))
        out_spec = pl.BlockSpec((1, 2, n, LANES), lambda g: (g, 0, 0, 0))
        out_shape = jax.ShapeDtypeStruct((N_GROUPS, 2, n, LANES), BF16)
    return pl.pallas_call(
        functools.partial(_compress_kernel, transposed=transposed),
        grid=(N_GROUPS,),
        in_specs=[
            pl.BlockSpec((n, width), lambda g: (0, 0)),
            pl.BlockSpec((2, 1, width), lambda g: (0, 0, 0)),
            pl.BlockSpec((1, 2, width, CMP_HIDDEN), lambda g: (g, 0, 0, 0)),
            w2_spec,
        ],
        out_specs=out_spec,
        out_shape=out_shape,
        compiler_params=_params(("parallel",)),
        name="compress_v" if transposed else "compress_k",
    )(cf, pe, w1, w2)


def _compress_weights(pe, w1, w2, transposed):
    w1h = w1.reshape(2, CMP_STRIDE, HEAD_DIM, CMP_HIDDEN)
    zeros = jnp.zeros_like(w1h)
    w1g = jnp.stack([jnp.concatenate([w1h, zeros], axis=2), jnp.concatenate([zeros, w1h], axis=2)])
    w1g = w1g.reshape(N_GROUPS, 2, CMP_STRIDE * LANES, CMP_HIDDEN).astype(BF16)
    peh = pe.reshape(2, CMP_STRIDE, HEAD_DIM)
    pe2 = jnp.concatenate([peh, peh], axis=2).reshape(2, 1, CMP_STRIDE * LANES)
    if transposed:
        w2p = w2.T.astype(BF16)
    else:
        z2 = jnp.zeros_like(w2)
        w2p = jnp.stack([jnp.concatenate([w2, z2], axis=1), jnp.concatenate([z2, w2], axis=1)]).astype(BF16)
    return pe2, w1g, w2p


def _split_heads(kt):
    lo = lax.broadcasted_iota(jnp.int32, kt.shape, 1) < HEAD_DIM
    kr = pltpu.roll(kt, HEAD_DIM, 1)
    zero = jnp.zeros_like(kt)
    return [[jnp.where(lo, kt, zero), jnp.where(lo, zero, kr)],
            [jnp.where(lo, kr, zero), jnp.where(lo, zero, kt)]]


QSUB = 256
CMP_TQ = 256
CMP_KEY_CHUNK = 256
CMP_LOOKAHEAD = 4


def _cmp_kernel(q_ref, kc_ref, vct_ref, ovt_ref, g3t_ref, o_ref, sel_ref, *, n_slc):
    tq = q_ref.shape[0]
    n_cmp = kc_ref.shape[2]
    q0 = pl.program_id(0) * tq
    t = q0 + lax.broadcasted_iota(jnp.int32, (1, tq), 1)
    any_valid = t >= CMP_BLOCK - 1
    cur = t // SLC_BLOCK

    def body(n_keys, n_blk):
        c_end = lax.broadcasted_iota(jnp.int32, (n_keys, 1), 0) * CMP_STRIDE + (CMP_BLOCK - 1)
        bias = jnp.where(c_end <= t, 0.0, NEG)
        gates = _sigmoid(g3t_ref[...].astype(F32))

        ovt = ovt_ref[:n_blk, :n_keys]
        ones = jnp.ones((BF16_ROWS, n_keys), BF16)
        lhs = [jnp.concatenate([vct_ref[g, :, :n_keys], ones, ovt], axis=0) for g in range(N_GROUPS)]
        imp_row = HEAD_DIM + BF16_ROWS
        blk = lax.broadcasted_iota(jnp.int32, (n_blk, QSUB), 0)
        blk_f = blk.astype(F32)

        def select(cols, importance):
            forced = (blk == 0) | (blk == cur[:, cols]) | (blk == cur[:, cols] - 1)
            causal = blk * SLC_BLOCK <= t[:, cols]
            for g in range(N_GROUPS):
                work = jnp.where(causal & ~forced, importance[g], -1.0)
                sel = jnp.where(forced, 1.0, 0.0)
                for _ in range(N_SELECT - N_FORCED):
                    m = jnp.max(work, axis=0, keepdims=True)
                    first = jnp.min(jnp.where(work == m, blk_f, float(n_blk)), axis=0, keepdims=True)
                    hit = blk_f == first
                    sel = jnp.where(hit, 1.0, sel)
                    work = jnp.where(hit, -jnp.inf, work)
                sel_ref[g * n_slc:g * n_slc + n_blk, cols] = jnp.where(sel > 0.0, 0.0, NEG)
                if n_blk < n_slc:
                    sel_ref[g * n_slc + n_blk:(g + 1) * n_slc, cols] = jnp.full((n_slc - n_blk, QSUB), NEG, F32)

        units = [(h, slice(qs * QSUB, (qs + 1) * QSUB)) for qs in range(tq // QSUB) for h in range(N_HEADS)]

        def scores(unit):
            h, cols = unit
            p, e = divmod(h, 2)
            g = p // PAIRS_PER_GROUP
            s = _dot_nt(kc_ref[g, e, :n_keys, :], q_ref[cols, p * LANES:(p + 1) * LANES]) + bias[:, cols]
            return s, jnp.max(s, axis=0, keepdims=True)

        importance = [None] * N_GROUPS
        pending = [scores(u) for u in units[:CMP_LOOKAHEAD]]
        for n, (h, cols) in enumerate(units):
            g = h // HEADS_PER_GROUP
            if n + CMP_LOOKAHEAD < len(units):
                pending.append(scores(units[n + CMP_LOOKAHEAD]))
            s, m = pending.pop(0)
            pr = jnp.exp2(s - m).astype(BF16)
            res = _dot(lhs[g], pr)
            inv_l = jnp.where(any_valid[:, cols], 1.0 / res[HEAD_DIM:HEAD_DIM + 1, :], 0.0)
            o_h = res[:HEAD_DIM, :] * (inv_l * gates[3 * h:3 * h + 1, cols])
            o_ref[h * HEAD_DIM:(h + 1) * HEAD_DIM, cols] = o_h.astype(BF16)
            imp_h = res[imp_row:, :] * inv_l
            importance[g] = imp_h if importance[g] is None else importance[g] + imp_h
            if h == N_HEADS - 1:
                select(cols, importance)
                importance = [None] * N_GROUPS

    chunk_tokens = CMP_KEY_CHUNK * CMP_STRIDE
    variant = (q0 + tq - 1) // chunk_tokens
    for v in range(n_cmp // CMP_KEY_CHUNK):
        pl.when(variant == v)(functools.partial(body, (v + 1) * CMP_KEY_CHUNK,
                                                (v + 1) * chunk_tokens // SLC_BLOCK))


def _cmp_attention(p_arr, kc, vct, overlap_t, g3t):
    s = p_arr.shape[0]
    n_cmp = kc.shape[2]
    n_slc = s // SLC_BLOCK
    return pl.pallas_call(
        functools.partial(_cmp_kernel, n_slc=n_slc),
        grid=(s // CMP_TQ,),
        in_specs=[
            pl.BlockSpec((CMP_TQ, D_MODEL), lambda i: (i, P_Q // 8)),
            pl.BlockSpec((N_GROUPS, 2, n_cmp, LANES), lambda i: (0, 0, 0, 0)),
            pl.BlockSpec((N_GROUPS, HEAD_DIM, n_cmp), lambda i: (0, 0, 0)),
            pl.BlockSpec((n_slc, n_cmp), lambda i: (0, 0)),
            pl.BlockSpec((LANES, CMP_TQ), lambda i: (0, i)),
        ],
        out_specs=[
            pl.BlockSpec((D_MODEL, CMP_TQ), lambda i: (0, i)),
            pl.BlockSpec((N_GROUPS * n_slc, CMP_TQ), lambda i: (0, i)),
        ],
        out_shape=[
            jax.ShapeDtypeStruct((D_MODEL, s), BF16),
            jax.ShapeDtypeStruct((N_GROUPS * n_slc, s), F32),
        ],
        compiler_params=_params(("parallel",)),
        name="cmp_attention",
    )(p_arr, kc, vct, overlap_t, g3t)


def _overlap_matrix_t(n_cmp, n_slc):
    c0 = np.arange(n_cmp)[None, :] * CMP_STRIDE
    s0 = np.arange(n_slc)[:, None] * SLC_BLOCK
    ov = (c0 < s0 + SLC_BLOCK) & (c0 + CMP_BLOCK > s0)
    return jnp.asarray(ov, dtype=BF16)


SLC_TQ = 1024
SLC_TK = 512
SLC_LOOKAHEAD = 4
WIN_TQ = 256
WIN_TK = 256
WIN_LOOKAHEAD = 6
ACC_ROWS = HEAD_DIM + BF16_ROWS


def _flash_kernel(qi_ref, ki_ref, fl_ref, *refs, mode, n_slc, branch, lookahead):
    if mode == "slc":
        q_ref, k_ref, vt_ref, g3t_ref, sel_ref, o_ref, m_ref, acc_ref = refs
    else:
        q_ref, k_ref, vt_ref, g3t_ref, o_ref, m_ref, acc_ref = refs
    step = pl.program_id(0)
    tq = q_ref.shape[0]
    tk = k_ref.shape[0]
    flags = fl_ref[step]

    @pl.when((flags & 1) != 0)
    def _():
        m_ref[...] = jnp.full_like(m_ref, NEG)
        acc_ref[...] = jnp.zeros_like(acc_ref)

    q0 = qi_ref[step] * tq
    k0 = ki_ref[step] * tk
    t = q0 + lax.broadcasted_iota(jnp.int32, (1, tq), 1)
    key = k0 + lax.broadcasted_iota(jnp.int32, (tk, 1), 0)

    if mode == "slc":
        tile_blocks = tk // SLC_BLOCK
        pad_blocks = BF16_ROWS - tile_blocks % BF16_ROWS
        key_blk = lax.broadcasted_iota(jnp.int32, (tk, tile_blocks + pad_blocks), 0) // SLC_BLOCK
        blk = lax.broadcasted_iota(jnp.int32, (tk, tile_blocks + pad_blocks), 1)
        expand = (blk == key_blk).astype(BF16)
        causal_bias = jnp.where(key <= t, 0.0, NEG)
        bias = []
        for g in range(N_GROUPS):
            first_block = pl.multiple_of(g * n_slc + ki_ref[step] * tile_blocks, tile_blocks)
            rows = jnp.concatenate([sel_ref[pl.ds(first_block, tile_blocks), :],
                                    jnp.zeros((pad_blocks, tq), F32)], axis=0)
            bias.append(jnp.minimum(_dot(expand, rows.astype(BF16)), causal_bias))
    else:
        d = t - key
        band = jnp.where((d >= 0) & (d < WINDOW), 0.0, NEG)
        bias = [band] * N_GROUPS

    k_half = _split_heads(k_ref[...])
    vt = vt_ref[...]
    ones = jnp.ones((ACC_ROWS - HEAD_DIM, tk), BF16)
    v_aug = [jnp.concatenate([vt[g * HEAD_DIM:(g + 1) * HEAD_DIM, :], ones], axis=0) for g in range(N_GROUPS)]

    units = [(h, slice(qs * QSUB, (qs + 1) * QSUB)) for qs in range(tq // QSUB) for h in range(N_HEADS)]

    def scores(unit):
        h, cols = unit
        p, e = divmod(h, 2)
        g = p // PAIRS_PER_GROUP
        s = (_dot_nt(k_half[g][e], q_ref[cols, p * LANES:(p + 1) * LANES]) + bias[g][:, cols]).astype(BF16)
        return s, jnp.max(s, axis=0, keepdims=True)

    pending = [scores(u) for u in units[:lookahead]]
    for n, (h, cols) in enumerate(units):
        g = h // HEADS_PER_GROUP
        if n + lookahead < len(units):
            pending.append(scores(units[n + lookahead]))
        s, s_max = pending.pop(0)
        m_prev = m_ref[h:h + 1, cols]
        m_new = jnp.maximum(m_prev, s_max.astype(F32))
        alpha = jnp.exp2(m_prev - m_new)
        pr = jnp.exp2(s - m_new.astype(BF16))
        m_ref[h:h + 1, cols] = m_new
        acc_ref[h, :, cols] = acc_ref[h, :, cols] * alpha + _dot(v_aug[g], pr)

    @pl.when((flags & 2) != 0)
    def _():
        gates = _sigmoid(g3t_ref[...].astype(F32))
        for h in range(N_HEADS):
            acc = acc_ref[h]
            scale = gates[3 * h + branch:3 * h + branch + 1, :] / acc[HEAD_DIM:HEAD_DIM + 1, :]
            o_ref[h * HEAD_DIM:(h + 1) * HEAD_DIM, :] = (acc[:HEAD_DIM, :] * scale).astype(BF16)


def _flash_tables(s, tq, tk, mode):
    qi, ki, fl = [], [], []
    for i in range(s // tq):
        q_lo, q_hi = i * tq, i * tq + tq - 1
        first = 0 if mode == "slc" else max(0, (q_lo - (WINDOW - 1)) // tk)
        last = q_hi // tk
        for k in range(first, last + 1):
            qi.append(i)
            ki.append(k)
            fl.append((1 if k == first else 0) | (2 if k == last else 0))
    return (jnp.asarray(qi, jnp.int32), jnp.asarray(ki, jnp.int32), jnp.asarray(fl, jnp.int32))


def _flash(p_arr, vt, g3t, sel_t, mode):
    s = p_arr.shape[0]
    n_slc = s // SLC_BLOCK
    tq, tk, lookahead = (SLC_TQ, SLC_TK, SLC_LOOKAHEAD) if mode == "slc" else (WIN_TQ, WIN_TK, WIN_LOOKAHEAD)
    k_col, branch = (P_KS, 1) if mode == "slc" else (P_KW, 2)
    qi, ki, fl = _flash_tables(s, tq, tk, mode)
    in_specs = [
        pl.BlockSpec((tq, D_MODEL), lambda n, qi, ki, fl: (qi[n], P_Q // 8)),
        pl.BlockSpec((tk, LANES), lambda n, qi, ki, fl: (ki[n], k_col)),
        pl.BlockSpec((LANES, tk), lambda n, qi, ki, fl: (0, ki[n])),
        pl.BlockSpec((LANES, tq), lambda n, qi, ki, fl: (0, qi[n])),
    ]
    args = [p_arr, p_arr, vt, g3t]
    if mode == "slc":
        in_specs.append(pl.BlockSpec((N_GROUPS * n_slc, tq), lambda n, qi, ki, fl: (0, qi[n])))
        args.append(sel_t)
    grid_spec = pltpu.PrefetchScalarGridSpec(
        num_scalar_prefetch=3,
        grid=(qi.shape[0],),
        in_specs=in_specs,
        out_specs=pl.BlockSpec((D_MODEL, tq), lambda n, qi, ki, fl: (0, qi[n])),
        scratch_shapes=[
            pltpu.VMEM((N_HEADS, tq), F32),
            pltpu.VMEM((N_HEADS, ACC_ROWS, tq), F32),
        ],
    )
    return pl.pallas_call(
        functools.partial(_flash_kernel, mode=mode, n_slc=n_slc, branch=branch, lookahead=lookahead),
        grid_spec=grid_spec,
        out_shape=jax.ShapeDtypeStruct((D_MODEL, s), BF16),
        compiler_params=_params(("arbitrary",)),
        name="flash_" + mode,
    )(qi, ki, fl, *args)


OUT_TM = 512
PREV_ROWS = 16


def _out_kernel(x_ref, oc_ref, os_ref, ow_ref, cb_ref, cc_ref, cx_ref, pc_ref, px_ref, ga_ref, gb_ref,
                cw_ref, wa_ref, wb_ref, wo_ref, o_ref, u_ref):
    tm = x_ref.shape[0]
    prev = pc_ref[...].astype(F32) * px_ref[...].astype(F32)
    u_ref[0:PREV_ROWS, :] = jnp.where(pl.program_id(0) == 0, 0.0, prev)
    u_ref[PREV_ROWS:, :] = cc_ref[...].astype(F32) * cx_ref[...].astype(F32)
    cw = cw_ref[...]
    conv = (cw[0:1, :] * u_ref[pl.ds(PREV_ROWS - 2, tm), :]
            + cw[1:2, :] * u_ref[pl.ds(PREV_ROWS - 1, tm), :]
            + cw[2:3, :] * u_ref[pl.ds(PREV_ROWS, tm), :])
    y_b = _dot((cb_ref[...].astype(F32) * conv).astype(BF16), wb_ref[...])

    o_nsa_t = oc_ref[...].astype(F32) + os_ref[...].astype(F32) + ow_ref[...].astype(F32)
    y_a = _dot(o_nsa_t.T.astype(BF16), wa_ref[...])

    merged = _sigmoid(ga_ref[...].astype(F32)) * y_a + _sigmoid(gb_ref[...].astype(F32)) * y_b
    o_ref[...] = x_ref[...] + _dot(merged.astype(BF16), wo_ref[...])


def _out(x1, o_cmp_t, o_slc_t, o_win_t, p_arr, conv_w, w_nsa_out, w_conv_out, w_out):
    s = x1.shape[0]
    tm = OUT_TM
    row = lambda i: (i, 0)
    col = lambda i: (0, i)
    prev_blocks = tm // PREV_ROWS

    def seg(c):
        return pl.BlockSpec((tm, D_MODEL), lambda i: (i, c // 8))

    def seg_prev(c):
        return pl.BlockSpec((PREV_ROWS, D_MODEL), lambda i: (jnp.maximum(i * prev_blocks - 1, 0), c // 8))

    full = lambda shape: pl.BlockSpec(shape, lambda i: (0, 0))
    return pl.pallas_call(
        _out_kernel,
        grid=(s // tm,),
        in_specs=[
            pl.BlockSpec((tm, D_MODEL), row), pl.BlockSpec((D_MODEL, tm), col),
            pl.BlockSpec((D_MODEL, tm), col), pl.BlockSpec((D_MODEL, tm), col),
            seg(P_CB), seg(P_CC), seg(P_CX), seg_prev(P_CC), seg_prev(P_CX), seg(P_GA), seg(P_GB),
            full((CONV_K, CONV_WIDTH)), full((D_MODEL, D_MODEL)), full((CONV_WIDTH, D_MODEL)),
            full((D_MODEL, D_MODEL)),
        ],
        out_specs=pl.BlockSpec((tm, D_MODEL), row),
        out_shape=jax.ShapeDtypeStruct((s, D_MODEL), F32),
        scratch_shapes=[pltpu.VMEM((tm + PREV_ROWS, CONV_WIDTH), F32)],
        compiler_params=_params(("parallel",)),
        name="merge_out",
    )(x1, o_cmp_t, o_slc_t, o_win_t, p_arr, p_arr, p_arr, p_arr, p_arr, p_arr, p_arr,
      conv_w, w_nsa_out, w_conv_out, w_out)


def _layer(x, ffn1_norm, ffn1_w_gate, ffn1_w_up, ffn1_w_down, mix_norm, w_in, cmp_pe_k, cmp_pe_v,
           cmp_k_w1, cmp_k_w2, cmp_v_w1, cmp_v_w2, conv_w, w_nsa_out, w_conv_out, w_out,
           ffn2_norm, ffn2_w_gate, ffn2_w_up, ffn2_w_down, final_g, final_norm):
    s = x.shape[0]
    n_cmp = s // CMP_STRIDE
    n_slc = s // SLC_BLOCK
    row = lambda v: v.reshape(1, -1).astype(F32)

    x1 = _ffn(x, row(ffn1_norm), ffn1_w_gate.astype(BF16), ffn1_w_up.astype(BF16),
              ffn1_w_down.astype(BF16), row(final_g), False)
    p_arr = _proj(x1, row(mix_norm), _proj_weight(w_in))

    col = lambda c: p_arr[:, c * LANES:(c + 1) * LANES]
    chunks = lambda c: col(c).reshape(n_cmp, CMP_STRIDE * LANES)
    kc = _compress(chunks(P_KC), *_compress_weights(cmp_pe_k, cmp_k_w1, cmp_k_w2, False), False)
    vct = _compress(chunks(P_VC), *_compress_weights(cmp_pe_v, cmp_v_w1, cmp_v_w2, True), True)

    g3t = col(P_G3).T
    o_cmp_t, sel_t = _cmp_attention(p_arr, kc, vct, _overlap_matrix_t(n_cmp, n_slc), g3t)
    o_slc_t = _flash(p_arr, col(P_VS).T, g3t, sel_t, "slc")
    o_win_t = _flash(p_arr, col(P_VW).T, g3t, None, "win")
    x2 = _out(x1, o_cmp_t, o_slc_t, o_win_t, p_arr, conv_w.astype(F32), w_nsa_out.astype(BF16),
              w_conv_out.astype(BF16), w_out.astype(BF16))
    return _ffn(x2, row(ffn2_norm), ffn2_w_gate.astype(BF16), ffn2_w_up.astype(BF16),
                ffn2_w_down.astype(BF16), row(final_g), final_norm)


def kernel(x, ffn1_norm, ffn1_w_gate, ffn1_w_up, ffn1_w_down, mix_norm, w_in, cmp_pe_k, cmp_pe_v,
           cmp_k_w1, cmp_k_w2, cmp_v_w1, cmp_v_w2, conv_w, w_nsa_out, w_conv_out, w_out, ffn2_norm,
           ffn2_w_gate, ffn2_w_up, ffn2_w_down, final_norm):
    batch, _, _ = x.shape
    depth = ffn1_norm.shape[0]
    outs = []
    for b in range(batch):
        xb = x[b]
        for l in range(depth):
            xb = _layer(xb, ffn1_norm[l], ffn1_w_gate[l], ffn1_w_up[l], ffn1_w_down[l], mix_norm[l],
                        w_in[l], cmp_pe_k[l], cmp_pe_v[l], cmp_k_w1[l], cmp_k_w2[l], cmp_v_w1[l],
                        cmp_v_w2[l], conv_w[l], w_nsa_out[l], w_conv_out[l], w_out[l], ffn2_norm[l],
                        ffn2_w_gate[l], ffn2_w_up[l], ffn2_w_down[l], final_norm, l == depth - 1)
        outs.append(xb)
    return jnp.stack(outs)
```

```python
import functools

import jax
import jax.numpy as jnp
import numpy as np
from jax import lax
from jax.experimental import pallas as pl
from jax.experimental.pallas import tpu as pltpu

D_MODEL = 1024
N_HEADS = 16
HEAD_DIM = 64
N_GROUPS = 2
HEADS_PER_GROUP = N_HEADS // N_GROUPS
N_PAIRS = N_HEADS // 2
PAIRS_PER_GROUP = N_PAIRS // N_GROUPS
CMP_BLOCK = 32
CMP_STRIDE = 16
CMP_HIDDEN = 256
SLC_BLOCK = 64
N_SELECT = 16
WINDOW = 512
CONV_WIDTH = 1024
CONV_K = 3
D_FF = 2816
EPS = 1e-6
NEG = -(2.0 ** 100)
N_FORCED = 3
LOG2E = float(np.log2(np.e))
LANES = 128
BF16_ROWS = 16

P_Q, P_CB, P_CC, P_CX, P_GA, P_GB = 0, 8, 16, 24, 32, 40
P_KC, P_VC, P_KS, P_VS, P_KW, P_VW, P_G3 = 48, 49, 50, 51, 52, 53, 54
P_BLOCKS = 56
P_COLS = P_BLOCKS * LANES

F32 = jnp.float32
BF16 = jnp.bfloat16

VMEM_LIMIT = 52 * 1024 * 1024


def _dot(a, b):
    return jnp.dot(a, b, preferred_element_type=F32)


def _dot_nt(a, b):
    return lax.dot_general(a, b, (((1,), (1,)), ((), ())), preferred_element_type=F32)


def _sigmoid(x):
    return 1.0 / (1.0 + jnp.exp(-x))


def _rms(x, g):
    return x * lax.rsqrt(jnp.mean(x * x, axis=-1, keepdims=True) + EPS) * g


def _params(sem, limit=VMEM_LIMIT, flags=None):
    return pltpu.CompilerParams(dimension_semantics=sem, vmem_limit_bytes=limit, flags=flags)


FFN_TM = 512
FFN_CHUNK = 256


def _ffn_kernel(x_ref, g_ref, wg_ref, wu_ref, wd_ref, fg_ref, o_ref, *, final_norm):
    x = x_ref[...]
    h = _rms(x, g_ref[...]).astype(BF16)
    acc = None
    for c in range(D_FF // FFN_CHUNK):
        cols = slice(c * FFN_CHUNK, (c + 1) * FFN_CHUNK)
        gate = _dot(h, wg_ref[:, cols])
        up = _dot(h, wu_ref[:, cols])
        a = (gate * _sigmoid(gate) * up).astype(BF16)
        part = _dot(a, wd_ref[cols, :])
        acc = part if acc is None else acc + part
    y = x + 0.5 * acc
    if final_norm:
        y = _rms(y, fg_ref[...])
    o_ref[...] = y


def _ffn(x, norm_g, wg, wu, wd, final_g, final_norm):
    s = x.shape[0]
    resident = lambda shape: pl.BlockSpec(shape, lambda i: (0, 0), pipeline_mode=pl.Buffered(1))
    return pl.pallas_call(
        functools.partial(_ffn_kernel, final_norm=final_norm),
        grid=(s // FFN_TM,),
        in_specs=[
            pl.BlockSpec((FFN_TM, D_MODEL), lambda i: (i, 0)),
            resident((1, D_MODEL)),
            resident((D_MODEL, D_FF)),
            resident((D_MODEL, D_FF)),
            resident((D_FF, D_MODEL)),
            resident((1, D_MODEL)),
        ],
        out_specs=pl.BlockSpec((FFN_TM, D_MODEL), lambda i: (i, 0)),
        out_shape=jax.ShapeDtypeStruct((s, D_MODEL), F32),
        compiler_params=_params(("parallel",)),
        name="ffn_final" if final_norm else "ffn",
    )(x, norm_g, wg, wu, wd, final_g)


PROJ_TM = 512
PROJ_CHUNK = 512


def _proj_kernel(x_ref, g_ref, w_ref, o_ref):
    h = _rms(x_ref[...], g_ref[...]).astype(BF16)
    for c in range(P_COLS // PROJ_CHUNK):
        cols = slice(c * PROJ_CHUNK, (c + 1) * PROJ_CHUNK)
        o_ref[:, cols] = _dot(h, w_ref[:, cols]).astype(BF16)


def _proj(x, norm_g, w):
    s = x.shape[0]
    resident = lambda shape: pl.BlockSpec(shape, lambda i: (0, 0), pipeline_mode=pl.Buffered(1))
    return pl.pallas_call(
        _proj_kernel,
        grid=(s // PROJ_TM,),
        in_specs=[
            pl.BlockSpec((PROJ_TM, D_MODEL), lambda i: (i, 0)),
            resident((1, D_MODEL)),
            resident((D_MODEL, P_COLS)),
        ],
        out_specs=pl.BlockSpec((PROJ_TM, P_COLS), lambda i: (i, 0)),
        out_shape=jax.ShapeDtypeStruct((s, P_COLS), BF16),
        compiler_params=_params(("parallel",)),
        name="proj",
    )(x, norm_g, w)


def _proj_weight(w_in):
    q, kc, vc, ks, vs, kw, vw, g3, cb, cc, cx, ga, gb = jnp.split(
        w_in, np.cumsum([1024, 128, 128, 128, 128, 128, 128, 48, 1024, 1024, 1024, 1024]).tolist(), axis=1)
    g3 = jnp.pad(g3, ((0, 0), (0, (P_BLOCKS - P_G3) * LANES - g3.shape[1])))
    w = jnp.concatenate([q * (HEAD_DIM ** -0.5 * LOG2E), cb, cc, cx, ga, gb, kc, vc, ks, vs, kw, vw, g3], axis=1)
    return w.astype(BF16)


def _gelu_tanh(x):
    return 0.5 * x * (1.0 + jnp.tanh(np.sqrt(2.0 / np.pi).astype(np.float32) * (x + 0.044715 * (x * x * x))))


def _compress_kernel(cf_ref, pe_ref, w1_ref, w2_ref, o_ref, *, transposed):
    cf = cf_ref[...].astype(F32)
    top = (cf + pe_ref[0]).astype(BF16)
    bot = (cf + pe_ref[1]).astype(BF16)
    a = _dot(top, w1_ref[0, 0])
    b = _dot(bot, w1_ref[0, 1])
    n = a.shape[0]
    pre = a + pltpu.roll(b, n - 1, 0)
    h = _gelu_tanh(pre).astype(BF16)
    if transposed:
        o_ref[0] = _dot_nt(w2_ref[...], h).astype(BF16)
    else:
        o_ref[0, 0] = _dot(h, w2_ref[0]).astype(BF16)
        o_ref[0, 1] = _dot(h, w2_ref[1]).astype(BF16)


def _compress(cf, pe, w1, w2, transposed):
    n = cf.shape[0]
    width = CMP_STRIDE * LANES
    if transposed:
        w2_spec = pl.BlockSpec((HEAD_DIM, CMP_HIDDEN), lambda g: (0, 0))
        out_spec = pl.BlockSpec((1, HEAD_DIM, n), lambda g: (g, 0, 0))
        out_shape = jax.ShapeDtypeStruct((N_GROUPS, HEAD_DIM, n), BF16)
    else:
        w2_spec = pl.BlockSpec((2, CMP_HIDDEN, LANES), lambda g: (0, 0, 0))
        out_spec = pl.BlockSpec((1, 2, n, LANES), lambda g: (g, 0, 0, 0))
        out_shape = jax.ShapeDtypeStruct((N_GROUPS, 2, n, LANES), BF16)
    return pl.pallas_call(
        functools.partial(_compress_kernel, transposed=transposed),
        grid=(N_GROUPS,),
        in_specs=[
            pl.BlockSpec((n, width), lambda g: (0, 0)),
            pl.BlockSpec((2, 1, width), lambda g: (0, 0, 0)),
            pl.BlockSpec((1, 2, width, CMP_HIDDEN), lambda g: (g, 0, 0, 0)),
            w2_spec,
        ],
        out_specs=out_spec,
        out_shape=out_shape,
        compiler_params=_params(("parallel",)),
        name="compress_v" if transposed else "compress_k",
    )(cf, pe, w1, w2)


def _compress_weights(pe, w1, w2, transposed):
    w1h = w1.reshape(2, CMP_STRIDE, HEAD_DIM, CMP_HIDDEN)
    zeros = jnp.zeros_like(w1h)
    w1g = jnp.stack([jnp.concatenate([w1h, zeros], axis=2), jnp.concatenate([zeros, w1h], axis=2)])
    w1g = w1g.reshape(N_GROUPS, 2, CMP_STRIDE * LANES, CMP_HIDDEN).astype(BF16)
    peh = pe.reshape(2, CMP_STRIDE, HEAD_DIM)
    pe2 = jnp.concatenate([peh, peh], axis=2).reshape(2, 1, CMP_STRIDE * LANES)
    if transposed:
        w2p = w2.T.astype(BF16)
    else:
        z2 = jnp.zeros_like(w2)
        w2p = jnp.stack([jnp.concatenate([w2, z2], axis=1), jnp.concatenate([z2, w2], axis=1)]).astype(BF16)
    return pe2, w1g, w2p


def _split_heads(kt):
    lo = lax.broadcasted_iota(jnp.int32, kt.shape, 1) < HEAD_DIM
    kr = pltpu.roll(kt, HEAD_DIM, 1)
    zero = jnp.zeros_like(kt)
    return [[jnp.where(lo, kt, zero), jnp.where(lo, zero, kr)],
            [jnp.where(lo, kr, zero), jnp.where(lo, zero, kt)]]


QSUB = 256
CMP_TQ = 256
CMP_KEY_CHUNK = 256
CMP_LOOKAHEAD = 4


def _cmp_kernel(q_ref, kc_ref, vct_ref, ovt_ref, g3t_ref, o_ref, sel_ref, *, n_slc):
    tq = q_ref.shape[0]
    n_cmp = kc_ref.shape[2]
    q0 = pl.program_id(0) * tq
    t = q0 + lax.broadcasted_iota(jnp.int32, (1, tq), 1)
    any_valid = t >= CMP_BLOCK - 1
    cur = t // SLC_BLOCK

    def body(n_keys, n_blk):
        c_end = lax.broadcasted_iota(jnp.int32, (n_keys, 1), 0) * CMP_STRIDE + (CMP_BLOCK - 1)
        bias = jnp.where(c_end <= t, 0.0, NEG)
        gates = _sigmoid(g3t_ref[...].astype(F32))

        ovt = ovt_ref[:n_blk, :n_keys]
        ones = jnp.ones((BF16_ROWS, n_keys), BF16)
        lhs = [jnp.concatenate([vct_ref[g, :, :n_keys], ones, ovt], axis=0) for g in range(N_GROUPS)]
        imp_row = HEAD_DIM + BF16_ROWS
        blk = lax.broadcasted_iota(jnp.int32, (n_blk, QSUB), 0)
        blk_f = blk.astype(F32)

        def select(cols, importance):
            forced = (blk == 0) | (blk == cur[:, cols]) | (blk == cur[:, cols] - 1)
            causal = blk * SLC_BLOCK <= t[:, cols]
            for g in range(N_GROUPS):
                work = jnp.where(causal & ~forced, importance[g], -1.0)
                sel = jnp.where(forced, 1.0, 0.0)
                for _ in range(N_SELECT - N_FORCED):
                    m = jnp.max(work, axis=0, keepdims=True)
                    first = jnp.min(jnp.where(work == m, blk_f, float(n_blk)), axis=0, keepdims=True)
                    hit = blk_f == first
                    sel = jnp.where(hit, 1.0, sel)
                    work = jnp.where(hit, -jnp.inf, work)
                sel_ref[g * n_slc:g * n_slc + n_blk, cols] = jnp.where(sel > 0.0, 0.0, NEG)
                if n_blk < n_slc:
                    sel_ref[g * n_slc + n_blk:(g + 1) * n_slc, cols] = jnp.full((n_slc - n_blk, QSUB), NEG, F32)

        units = [(h, slice(qs * QSUB, (qs + 1) * QSUB)) for qs in range(tq // QSUB) for h in range(N_HEADS)]

        def scores(unit):
            h, cols = unit
            p, e = divmod(h, 2)
            g = p // PAIRS_PER_GROUP
            s = _dot_nt(kc_ref[g, e, :n_keys, :], q_ref[cols, p * LANES:(p + 1) * LANES]) + bias[:, cols]
            return s, jnp.max(s, axis=0, keepdims=True)

        importance = [None] * N_GROUPS
        pending = [scores(u) for u in units[:CMP_LOOKAHEAD]]
        for n, (h, cols) in enumerate(units):
            g = h // HEADS_PER_GROUP
            if n + CMP_LOOKAHEAD < len(units):
                pending.append(scores(units[n + CMP_LOOKAHEAD]))
            s, m = pending.pop(0)
            pr = jnp.exp2(s - m).astype(BF16)
            res = _dot(lhs[g], pr)
            inv_l = jnp.where(any_valid[:, cols], 1.0 / res[HEAD_DIM:HEAD_DIM + 1, :], 0.0)
            o_h = res[:HEAD_DIM, :] * (inv_l * gates[3 * h:3 * h + 1, cols])
            o_ref[h * HEAD_DIM:(h + 1) * HEAD_DIM, cols] = o_h.astype(BF16)
            imp_h = res[imp_row:, :] * inv_l
            importance[g] = imp_h if importance[g] is None else importance[g] + imp_h
            if h == N_HEADS - 1:
                select(cols, importance)
                importance = [None] * N_GROUPS

    chunk_tokens = CMP_KEY_CHUNK * CMP_STRIDE
    variant = (q0 + tq - 1) // chunk_tokens
    for v in range(n_cmp // CMP_KEY_CHUNK):
        pl.when(variant == v)(functools.partial(body, (v + 1) * CMP_KEY_CHUNK,
                                                (v + 1) * chunk_tokens // SLC_BLOCK))


def _cmp_attention(p_arr, kc, vct, overlap_t, g3t):
    s = p_arr.shape[0]
    n_cmp = kc.shape[2]
    n_slc = s // SLC_BLOCK
    return pl.pallas_call(
        functools.partial(_cmp_kernel, n_slc=n_slc),
        grid=(s // CMP_TQ,),
        in_specs=[
            pl.BlockSpec((CMP_TQ, D_MODEL), lambda i: (i, P_Q // 8)),
            pl.BlockSpec((N_GROUPS, 2, n_cmp, LANES), lambda i: (0, 0, 0, 0)),
            pl.BlockSpec((N_GROUPS, HEAD_DIM, n_cmp), lambda i: (0, 0, 0)),
            pl.BlockSpec((n_slc, n_cmp), lambda i: (0, 0)),
            pl.BlockSpec((LANES, CMP_TQ), lambda i: (0, i)),
        ],
        out_specs=[
            pl.BlockSpec((D_MODEL, CMP_TQ), lambda i: (0, i)),
            pl.BlockSpec((N_GROUPS * n_slc, CMP_TQ), lambda i: (0, i)),
        ],
        out_shape=[
            jax.ShapeDtypeStruct((D_MODEL, s), BF16),
            jax.ShapeDtypeStruct((N_GROUPS * n_slc, s), F32),
        ],
        compiler_params=_params(("parallel",)),
        name="cmp_attention",
    )(p_arr, kc, vct, overlap_t, g3t)


def _overlap_matrix_t(n_cmp, n_slc):
    c0 = np.arange(n_cmp)[None, :] * CMP_STRIDE
    s0 = np.arange(n_slc)[:, None] * SLC_BLOCK
    ov = (c0 < s0 + SLC_BLOCK) & (c0 + CMP_BLOCK > s0)
    return jnp.asarray(ov, dtype=BF16)


SLC_TQ = 1024
SLC_TK = 512
SLC_LOOKAHEAD = 4
ACC_ROWS = HEAD_DIM + BF16_ROWS


def _selected_kernel(qi_ref, ki_ref, fl_ref, q_ref, k_ref, vt_ref, g3t_ref, sel_ref, o_ref, m_ref, acc_ref, *,
                     n_slc):
    step = pl.program_id(0)
    tq = q_ref.shape[0]
    tk = k_ref.shape[0]
    flags = fl_ref[step]

    @pl.when((flags & 1) != 0)
    def _():
        m_ref[...] = jnp.full_like(m_ref, NEG)
        acc_ref[...] = jnp.zeros_like(acc_ref)

    q0 = qi_ref[step] * tq
    k0 = ki_ref[step] * tk
    t = q0 + lax.broadcasted_iota(jnp.int32, (1, tq), 1)
    key = k0 + lax.broadcasted_iota(jnp.int32, (tk, 1), 0)

    tile_blocks = tk // SLC_BLOCK
    pad_blocks = BF16_ROWS - tile_blocks % BF16_ROWS
    key_blk = lax.broadcasted_iota(jnp.int32, (tk, tile_blocks + pad_blocks), 0) // SLC_BLOCK
    blk = lax.broadcasted_iota(jnp.int32, (tk, tile_blocks + pad_blocks), 1)
    expand = (blk == key_blk).astype(BF16)
    causal_bias = jnp.where(key <= t, 0.0, NEG)
    bias = []
    for g in range(N_GROUPS):
        first_block = pl.multiple_of(g * n_slc + ki_ref[step] * tile_blocks, tile_blocks)
        rows = jnp.concatenate([sel_ref[pl.ds(first_block, tile_blocks), :],
                                jnp.zeros((pad_blocks, tq), F32)], axis=0)
        bias.append(jnp.minimum(_dot(expand, rows.astype(BF16)), causal_bias))

    k_half = _split_heads(k_ref[...])
    vt = vt_ref[...]
    ones = jnp.ones((ACC_ROWS - HEAD_DIM, tk), BF16)
    v_aug = [jnp.concatenate([vt[g * HEAD_DIM:(g + 1) * HEAD_DIM, :], ones], axis=0) for g in range(N_GROUPS)]

    units = [(h, slice(qs * QSUB, (qs + 1) * QSUB)) for qs in range(tq // QSUB) for h in range(N_HEADS)]

    def scores(unit):
        h, cols = unit
        p, e = divmod(h, 2)
        g = p // PAIRS_PER_GROUP
        s = (_dot_nt(k_half[g][e], q_ref[cols, p * LANES:(p + 1) * LANES]) + bias[g][:, cols]).astype(BF16)
        return s, jnp.max(s, axis=0, keepdims=True)

    pending = [scores(u) for u in units[:SLC_LOOKAHEAD]]
    for n, (h, cols) in enumerate(units):
        g = h // HEADS_PER_GROUP
        if n + SLC_LOOKAHEAD < len(units):
            pending.append(scores(units[n + SLC_LOOKAHEAD]))
        s, s_max = pending.pop(0)
        m_prev = m_ref[h:h + 1, cols]
        m_new = jnp.maximum(m_prev, s_max.astype(F32))
        alpha = jnp.exp2(m_prev - m_new)
        pr = jnp.exp2(s - m_new.astype(BF16))
        m_ref[h:h + 1, cols] = m_new
        acc_ref[h, :, cols] = acc_ref[h, :, cols] * alpha + _dot(v_aug[g], pr)

    @pl.when((flags & 2) != 0)
    def _():
        gates = _sigmoid(g3t_ref[...].astype(F32))
        for h in range(N_HEADS):
            acc = acc_ref[h]
            scale = gates[3 * h + 1:3 * h + 2, :] / acc[HEAD_DIM:HEAD_DIM + 1, :]
            o_ref[h * HEAD_DIM:(h + 1) * HEAD_DIM, :] = (acc[:HEAD_DIM, :] * scale).astype(BF16)


def _causal_tile_tables(s, tq, tk):
    qi, ki, fl = [], [], []
    for i in range(s // tq):
        last = (i * tq + tq - 1) // tk
        for k in range(last + 1):
            qi.append(i)
            ki.append(k)
            fl.append((1 if k == 0 else 0) | (2 if k == last else 0))
    return (jnp.asarray(qi, jnp.int32), jnp.asarray(ki, jnp.int32), jnp.asarray(fl, jnp.int32))


def _selected_attention(p_arr, vt, g3t, sel_t):
    s = p_arr.shape[0]
    n_slc = s // SLC_BLOCK
    tq, tk = SLC_TQ, SLC_TK
    qi, ki, fl = _causal_tile_tables(s, tq, tk)
    grid_spec = pltpu.PrefetchScalarGridSpec(
        num_scalar_prefetch=3,
        grid=(qi.shape[0],),
        in_specs=[
            pl.BlockSpec((tq, D_MODEL), lambda n, qi, ki, fl: (qi[n], P_Q // 8)),
            pl.BlockSpec((tk, LANES), lambda n, qi, ki, fl: (ki[n], P_KS)),
            pl.BlockSpec((LANES, tk), lambda n, qi, ki, fl: (0, ki[n])),
            pl.BlockSpec((LANES, tq), lambda n, qi, ki, fl: (0, qi[n])),
            pl.BlockSpec((N_GROUPS * n_slc, tq), lambda n, qi, ki, fl: (0, qi[n])),
        ],
        out_specs=pl.BlockSpec((D_MODEL, tq), lambda n, qi, ki, fl: (0, qi[n])),
        scratch_shapes=[
            pltpu.VMEM((N_HEADS, tq), F32),
            pltpu.VMEM((N_HEADS, ACC_ROWS, tq), F32),
        ],
    )
    return pl.pallas_call(
        functools.partial(_selected_kernel, n_slc=n_slc),
        grid_spec=grid_spec,
        out_shape=jax.ShapeDtypeStruct((D_MODEL, s), BF16),
        compiler_params=_params(("arbitrary",)),
        name="selected_attention",
    )(qi, ki, fl, p_arr, p_arr, vt, g3t, sel_t)


WINDOW_TQ = 1024
WINDOW_BLOCK = 256
WINDOW_LOOKAHEAD = 4


def _window_kernel(q_ref, *refs, n_blocks):
    k_refs, vt_refs = refs[:n_blocks], refs[n_blocks:2 * n_blocks]
    g3t_ref, o_ref = refs[2 * n_blocks:]
    tq = q_ref.shape[0]
    span = WINDOW + QSUB
    k_half = _split_heads(jnp.concatenate([r[...] for r in k_refs], axis=0))
    vt = jnp.concatenate([r[...] for r in vt_refs], axis=1)
    ones = jnp.ones((BF16_ROWS, vt.shape[1]), BF16)
    v_aug = [jnp.concatenate([vt[g * HEAD_DIM:(g + 1) * HEAD_DIM, :], ones], axis=0) for g in range(N_GROUPS)]
    gates = _sigmoid(g3t_ref[...].astype(F32))

    r = lax.broadcasted_iota(jnp.int32, (span, 1), 0)
    c = lax.broadcasted_iota(jnp.int32, (1, QSUB), 1)
    in_band = (r > c) & (r <= c + WINDOW)
    first_key = pl.program_id(0) * tq - WINDOW
    n_sub = tq // QSUB
    bias = [jnp.where(in_band & (first_key + qs * QSUB + r >= 0), 0.0, NEG) for qs in range(n_sub)]

    units = [(h, qs) for qs in range(n_sub) for h in range(N_HEADS)]

    def scores(unit):
        h, qs = unit
        p, e = divmod(h, 2)
        g = p // PAIRS_PER_GROUP
        q = q_ref[qs * QSUB:(qs + 1) * QSUB, p * LANES:(p + 1) * LANES]
        s = (_dot_nt(k_half[g][e][qs * QSUB:qs * QSUB + span, :], q) + bias[qs]).astype(BF16)
        return s, jnp.max(s, axis=0, keepdims=True)

    pending = [scores(u) for u in units[:WINDOW_LOOKAHEAD]]
    for n, (h, qs) in enumerate(units):
        g = h // HEADS_PER_GROUP
        cols = slice(qs * QSUB, (qs + 1) * QSUB)
        if n + WINDOW_LOOKAHEAD < len(units):
            pending.append(scores(units[n + WINDOW_LOOKAHEAD]))
        s, m = pending.pop(0)
        res = _dot(v_aug[g][:, qs * QSUB:qs * QSUB + span], jnp.exp2(s - m))
        scale = gates[3 * h + 2:3 * h + 3, cols] / res[HEAD_DIM:HEAD_DIM + 1, :]
        o_ref[h * HEAD_DIM:(h + 1) * HEAD_DIM, cols] = (res[:HEAD_DIM, :] * scale).astype(BF16)


def _window_attention(p_arr, vt, g3t):
    s = p_arr.shape[0]
    tq = WINDOW_TQ
    n_blocks = (WINDOW + tq) // WINDOW_BLOCK
    back = WINDOW // WINDOW_BLOCK
    per_tile = tq // WINDOW_BLOCK

    def block(j):
        return lambda i: jnp.maximum(i * per_tile - back + j, 0)

    k_specs = [pl.BlockSpec((WINDOW_BLOCK, LANES), lambda i, b=block(j): (b(i), P_KW)) for j in range(n_blocks)]
    v_specs = [pl.BlockSpec((LANES, WINDOW_BLOCK), lambda i, b=block(j): (0, b(i))) for j in range(n_blocks)]
    return pl.pallas_call(
        functools.partial(_window_kernel, n_blocks=n_blocks),
        grid=(s // tq,),
        in_specs=[pl.BlockSpec((tq, D_MODEL), lambda i: (i, P_Q // 8))] + k_specs + v_specs
        + [pl.BlockSpec((LANES, tq), lambda i: (0, i))],
        out_specs=pl.BlockSpec((D_MODEL, tq), lambda i: (0, i)),
        out_shape=jax.ShapeDtypeStruct((D_MODEL, s), BF16),
        compiler_params=_params(("parallel",)),
        name="window_attention",
    )(p_arr, *([p_arr] * n_blocks), *([vt] * n_blocks), g3t)


OUT_TM = 512
PREV_ROWS = 16


def _out_kernel(x_ref, oc_ref, os_ref, ow_ref, cb_ref, cc_ref, cx_ref, pc_ref, px_ref, ga_ref, gb_ref,
                cw_ref, wa_ref, wb_ref, wo_ref, o_ref, u_ref):
    tm = x_ref.shape[0]
    prev = pc_ref[...].astype(F32) * px_ref[...].astype(F32)
    u_ref[0:PREV_ROWS, :] = jnp.where(pl.program_id(0) == 0, 0.0, prev)
    u_ref[PREV_ROWS:, :] = cc_ref[...].astype(F32) * cx_ref[...].astype(F32)
    cw = cw_ref[...]
    conv = (cw[0:1, :] * u_ref[pl.ds(PREV_ROWS - 2, tm), :]
            + cw[1:2, :] * u_ref[pl.ds(PREV_ROWS - 1, tm), :]
            + cw[2:3, :] * u_ref[pl.ds(PREV_ROWS, tm), :])
    y_b = _dot((cb_ref[...].astype(F32) * conv).astype(BF16), wb_ref[...])

    o_nsa_t = oc_ref[...].astype(F32) + os_ref[...].astype(F32) + ow_ref[...].astype(F32)
    y_a = _dot(o_nsa_t.T.astype(BF16), wa_ref[...])

    merged = _sigmoid(ga_ref[...].astype(F32)) * y_a + _sigmoid(gb_ref[...].astype(F32)) * y_b
    o_ref[...] = x_ref[...] + _dot(merged.astype(BF16), wo_ref[...])


def _out(x1, o_cmp_t, o_slc_t, o_win_t, p_arr, conv_w, w_nsa_out, w_conv_out, w_out):
    s = x1.shape[0]
    tm = OUT_TM
    row = lambda i: (i, 0)
    col = lambda i: (0, i)
    prev_blocks = tm // PREV_ROWS

    def seg(c):
        return pl.BlockSpec((tm, D_MODEL), lambda i: (i, c // 8))

    def seg_prev(c):
        return pl.BlockSpec((PREV_ROWS, D_MODEL), lambda i: (jnp.maximum(i * prev_blocks - 1, 0), c // 8))

    full = lambda shape: pl.BlockSpec(shape, lambda i: (0, 0))
    return pl.pallas_call(
        _out_kernel,
        grid=(s // tm,),
        in_specs=[
            pl.BlockSpec((tm, D_MODEL), row), pl.BlockSpec((D_MODEL, tm), col),
            pl.BlockSpec((D_MODEL, tm), col), pl.BlockSpec((D_MODEL, tm), col),
            seg(P_CB), seg(P_CC), seg(P_CX), seg_prev(P_CC), seg_prev(P_CX), seg(P_GA), seg(P_GB),
            full((CONV_K, CONV_WIDTH)), full((D_MODEL, D_MODEL)), full((CONV_WIDTH, D_MODEL)),
            full((D_MODEL, D_MODEL)),
        ],
        out_specs=pl.BlockSpec((tm, D_MODEL), row),
        out_shape=jax.ShapeDtypeStruct((s, D_MODEL), F32),
        scratch_shapes=[pltpu.VMEM((tm + PREV_ROWS, CONV_WIDTH), F32)],
        compiler_params=_params(("parallel",)),
        name="merge_out",
    )(x1, o_cmp_t, o_slc_t, o_win_t, p_arr, p_arr, p_arr, p_arr, p_arr, p_arr, p_arr,
      conv_w, w_nsa_out, w_conv_out, w_out)


def _layer(x, ffn1_norm, ffn1_w_gate, ffn1_w_up, ffn1_w_down, mix_norm, w_in, cmp_pe_k, cmp_pe_v,
           cmp_k_w1, cmp_k_w2, cmp_v_w1, cmp_v_w2, conv_w, w_nsa_out, w_conv_out, w_out,
           ffn2_norm, ffn2_w_gate, ffn2_w_up, ffn2_w_down, final_g, final_norm):
    s = x.shape[0]
    n_cmp = s // CMP_STRIDE
    n_slc = s // SLC_BLOCK
    row = lambda v: v.reshape(1, -1).astype(F32)

    x1 = _ffn(x, row(ffn1_norm), ffn1_w_gate.astype(BF16), ffn1_w_up.astype(BF16),
              ffn1_w_down.astype(BF16), row(final_g), False)
    p_arr = _proj(x1, row(mix_norm), _proj_weight(w_in))

    col = lambda c: p_arr[:, c * LANES:(c + 1) * LANES]
    chunks = lambda c: col(c).reshape(n_cmp, CMP_STRIDE * LANES)
    kc = _compress(chunks(P_KC), *_compress_weights(cmp_pe_k, cmp_k_w1, cmp_k_w2, False), False)
    vct = _compress(chunks(P_VC), *_compress_weights(cmp_pe_v, cmp_v_w1, cmp_v_w2, True), True)

    g3t = col(P_G3).T
    o_cmp_t, sel_t = _cmp_attention(p_arr, kc, vct, _overlap_matrix_t(n_cmp, n_slc), g3t)
    o_slc_t = _selected_attention(p_arr, col(P_VS).T, g3t, sel_t)
    o_win_t = _window_attention(p_arr, col(P_VW).T, g3t)
    x2 = _out(x1, o_cmp_t, o_slc_t, o_win_t, p_arr, conv_w.astype(F32), w_nsa_out.astype(BF16),
              w_conv_out.astype(BF16), w_out.astype(BF16))
    return _ffn(x2, row(ffn2_norm), ffn2_w_gate.astype(BF16), ffn2_w_up.astype(BF16),
                ffn2_w_down.astype(BF16), row(final_g), final_norm)


def kernel(x, ffn1_norm, ffn1_w_gate, ffn1_w_up, ffn1_w_down, mix_norm, w_in, cmp_pe_k, cmp_pe_v,
           cmp_k_w1, cmp_k_w2, cmp_v_w1, cmp_v_w2, conv_w, w_nsa_out, w_conv_out, w_out, ffn2_norm,
           ffn2_w_gate, ffn2_w_up, ffn2_w_down, final_norm):
    batch, _, _ = x.shape
    depth = ffn1_norm.shape[0]
    outs = []
    for b in range(batch):
        xb = x[b]
        for l in range(depth):
            xb = _layer(xb, ffn1_norm[l], ffn1_w_gate[l], ffn1_w_up[l], ffn1_w_down[l], mix_norm[l],
                        w_in[l], cmp_pe_k[l], cmp_pe_v[l], cmp_k_w1[l], cmp_k_w2[l], cmp_v_w1[l],
                        cmp_v_w2[l], conv_w[l], w_nsa_out[l], w_conv_out[l], w_out[l], ffn2_norm[l],
                        ffn2_w_gate[l], ffn2_w_up[l], ffn2_w_down[l], final_norm, l == depth - 1)
        outs.append(xb)
    return jnp.stack(outs)
```

```python
import functools

import jax
import jax.numpy as jnp
import numpy as np
from jax import lax
from jax.experimental import pallas as pl
from jax.experimental.pallas import tpu as pltpu

D_MODEL = 1024
N_HEADS = 16
HEAD_DIM = 64
N_GROUPS = 2
HEADS_PER_GROUP = N_HEADS // N_GROUPS
N_PAIRS = N_HEADS // 2
PAIRS_PER_GROUP = N_PAIRS // N_GROUPS
CMP_BLOCK = 32
CMP_STRIDE = 16
CMP_HIDDEN = 256
SLC_BLOCK = 64
N_SELECT = 16
WINDOW = 512
CONV_WIDTH = 1024
CONV_K = 3
D_FF = 2816
EPS = 1e-6
NEG = -(2.0 ** 100)
N_FORCED = 3
LOG2E = float(np.log2(np.e))
LANES = 128
BF16_ROWS = 16

P_Q, P_CB, P_CC, P_CX, P_GA, P_GB = 0, 8, 16, 24, 32, 40
P_KC, P_VC, P_KS, P_VS, P_KW, P_VW, P_G3 = 48, 49, 50, 51, 52, 53, 54
P_BLOCKS = 56
P_COLS = P_BLOCKS * LANES

F32 = jnp.float32
BF16 = jnp.bfloat16

VMEM_LIMIT = 52 * 1024 * 1024


def _dot(a, b):
    return jnp.dot(a, b, preferred_element_type=F32)


def _dot_nt(a, b):
    return lax.dot_general(a, b, (((1,), (1,)), ((), ())), preferred_element_type=F32)


def _sigmoid(x):
    return 1.0 / (1.0 + jnp.exp(-x))


def _rms(x, g):
    return x * lax.rsqrt(jnp.mean(x * x, axis=-1, keepdims=True) + EPS) * g


def _params(sem, limit=VMEM_LIMIT, flags=None):
    return pltpu.CompilerParams(dimension_semantics=sem, vmem_limit_bytes=limit, flags=flags)


FFN_TM = 512
FFN_CHUNK = 256


def _ffn_kernel(x_ref, g_ref, wg_ref, wu_ref, wd_ref, fg_ref, o_ref, *, final_norm):
    x = x_ref[...]
    h = _rms(x, g_ref[...]).astype(BF16)
    acc = None
    for c in range(D_FF // FFN_CHUNK):
        cols = slice(c * FFN_CHUNK, (c + 1) * FFN_CHUNK)
        gate = _dot(h, wg_ref[:, cols])
        up = _dot(h, wu_ref[:, cols])
        a = (gate * _sigmoid(gate) * up).astype(BF16)
        part = _dot(a, wd_ref[cols, :])
        acc = part if acc is None else acc + part
    y = x + 0.5 * acc
    if final_norm:
        y = _rms(y, fg_ref[...])
    o_ref[...] = y


def _ffn(x, norm_g, wg, wu, wd, final_g, final_norm):
    s = x.shape[0]
    resident = lambda shape: pl.BlockSpec(shape, lambda i: (0, 0), pipeline_mode=pl.Buffered(1))
    return pl.pallas_call(
        functools.partial(_ffn_kernel, final_norm=final_norm),
        grid=(s // FFN_TM,),
        in_specs=[
            pl.BlockSpec((FFN_TM, D_MODEL), lambda i: (i, 0)),
            resident((1, D_MODEL)),
            resident((D_MODEL, D_FF)),
            resident((D_MODEL, D_FF)),
            resident((D_FF, D_MODEL)),
            resident((1, D_MODEL)),
        ],
        out_specs=pl.BlockSpec((FFN_TM, D_MODEL), lambda i: (i, 0)),
        out_shape=jax.ShapeDtypeStruct((s, D_MODEL), F32),
        compiler_params=_params(("parallel",)),
        name="ffn_final" if final_norm else "ffn",
    )(x, norm_g, wg, wu, wd, final_g)


PROJ_TM = 512
PROJ_CHUNK = 512


def _proj_kernel(x_ref, g_ref, w_ref, o_ref):
    h = _rms(x_ref[...], g_ref[...]).astype(BF16)
    for c in range(P_COLS // PROJ_CHUNK):
        cols = slice(c * PROJ_CHUNK, (c + 1) * PROJ_CHUNK)
        o_ref[:, cols] = _dot(h, w_ref[:, cols]).astype(BF16)


def _proj(x, norm_g, w):
    s = x.shape[0]
    resident = lambda shape: pl.BlockSpec(shape, lambda i: (0, 0), pipeline_mode=pl.Buffered(1))
    return pl.pallas_call(
        _proj_kernel,
        grid=(s // PROJ_TM,),
        in_specs=[
            pl.BlockSpec((PROJ_TM, D_MODEL), lambda i: (i, 0)),
            resident((1, D_MODEL)),
            resident((D_MODEL, P_COLS)),
        ],
        out_specs=pl.BlockSpec((PROJ_TM, P_COLS), lambda i: (i, 0)),
        out_shape=jax.ShapeDtypeStruct((s, P_COLS), BF16),
        compiler_params=_params(("parallel",)),
        name="proj",
    )(x, norm_g, w)


def _proj_weight(w_in):
    q, kc, vc, ks, vs, kw, vw, g3, cb, cc, cx, ga, gb = jnp.split(
        w_in, np.cumsum([1024, 128, 128, 128, 128, 128, 128, 48, 1024, 1024, 1024, 1024]).tolist(), axis=1)
    g3 = jnp.pad(g3, ((0, 0), (0, (P_BLOCKS - P_G3) * LANES - g3.shape[1])))
    w = jnp.concatenate([q * (HEAD_DIM ** -0.5 * LOG2E), cb, cc, cx, ga, gb, kc, vc, ks, vs, kw, vw, g3], axis=1)
    return w.astype(BF16)


def _gelu_tanh(x):
    return 0.5 * x * (1.0 + jnp.tanh(np.sqrt(2.0 / np.pi).astype(np.float32) * (x + 0.044715 * (x * x * x))))


def _compress_kernel(cf_ref, pe_ref, w1_ref, w2_ref, o_ref, *, transposed):
    cf = cf_ref[...].astype(F32)
    top = (cf + pe_ref[0]).astype(BF16)
    bot = (cf + pe_ref[1]).astype(BF16)
    a = _dot(top, w1_ref[0, 0])
    b = _dot(bot, w1_ref[0, 1])
    n = a.shape[0]
    pre = a + pltpu.roll(b, n - 1, 0)
    h = _gelu_tanh(pre).astype(BF16)
    if transposed:
        o_ref[0] = _dot_nt(w2_ref[...], h).astype(BF16)
    else:
        o_ref[0, 0] = _dot(h, w2_ref[0]).astype(BF16)
        o_ref[0, 1] = _dot(h, w2_ref[1]).astype(BF16)


def _compress(cf, pe, w1, w2, transposed):
    n = cf.shape[0]
    width = CMP_STRIDE * LANES
    if transposed:
        w2_spec = pl.BlockSpec((HEAD_DIM, CMP_HIDDEN), lambda g: (0, 0))
        out_spec = pl.BlockSpec((1, HEAD_DIM, n), lambda g: (g, 0, 0))
        out_shape = jax.ShapeDtypeStruct((N_GROUPS, HEAD_DIM, n), BF16)
    else:
        w2_spec = pl.BlockSpec((2, CMP_HIDDEN, LANES), lambda g: (0, 0, 0))
        out_spec = pl.BlockSpec((1, 2, n, LANES), lambda g: (g, 0, 0, 0))
        out_shape = jax.ShapeDtypeStruct((N_GROUPS, 2, n, LANES), BF16)
    return pl.pallas_call(
        functools.partial(_compress_kernel, transposed=transposed),
        grid=(N_GROUPS,),
        in_specs=[
            pl.BlockSpec((n, width), lambda g: (0, 0)),
            pl.BlockSpec((2, 1, width), lambda g: (0, 0, 0)),
            pl.BlockSpec((1, 2, width, CMP_HIDDEN), lambda g: (g, 0, 0, 0)),
            w2_spec,
        ],
        out_specs=out_spec,
        out_shape=out_shape,
        compiler_params=_params(("parallel",)),
        name="compress_v" if transposed else "compress_k",
    )(cf, pe, w1, w2)


def _compress_weights(pe, w1, w2, transposed):
    w1h = w1.reshape(2, CMP_STRIDE, HEAD_DIM, CMP_HIDDEN)
    zeros = jnp.zeros_like(w1h)
    w1g = jnp.stack([jnp.concatenate([w1h, zeros], axis=2), jnp.concatenate([zeros, w1h], axis=2)])
    w1g = w1g.reshape(N_GROUPS, 2, CMP_STRIDE * LANES, CMP_HIDDEN).astype(BF16)
    peh = pe.reshape(2, CMP_STRIDE, HEAD_DIM)
    pe2 = jnp.concatenate([peh, peh], axis=2).reshape(2, 1, CMP_STRIDE * LANES)
    if transposed:
        w2p = w2.T.astype(BF16)
    else:
        z2 = jnp.zeros_like(w2)
        w2p = jnp.stack([jnp.concatenate([w2, z2], axis=1), jnp.concatenate([z2, w2], axis=1)]).astype(BF16)
    return pe2, w1g, w2p


def _split_heads(kt):
    lo = lax.broadcasted_iota(jnp.int32, kt.shape, 1) < HEAD_DIM
    kr = pltpu.roll(kt, HEAD_DIM, 1)
    zero = jnp.zeros_like(kt)
    return [[jnp.where(lo, kt, zero), jnp.where(lo, zero, kr)],
            [jnp.where(lo, kr, zero), jnp.where(lo, zero, kt)]]


QSUB = 256
CMP_TQ = 256
CMP_KEY_CHUNK = 256
CMP_LOOKAHEAD = 4


def _cmp_kernel(q_ref, kc_ref, vct_ref, ovt_ref, g3t_ref, o_ref, sel_ref, *, n_slc):
    tq = q_ref.shape[0]
    n_cmp = kc_ref.shape[2]
    q0 = pl.program_id(0) * tq
    t = q0 + lax.broadcasted_iota(jnp.int32, (1, tq), 1)
    any_valid = t >= CMP_BLOCK - 1
    cur = t // SLC_BLOCK

    def body(n_keys, n_blk):
        c_end = lax.broadcasted_iota(jnp.int32, (n_keys, 1), 0) * CMP_STRIDE + (CMP_BLOCK - 1)
        bias = jnp.where(c_end <= t, 0.0, NEG)
        gates = _sigmoid(g3t_ref[...].astype(F32))

        ovt = ovt_ref[:n_blk, :n_keys]
        ones = jnp.ones((BF16_ROWS, n_keys), BF16)
        lhs = [jnp.concatenate([vct_ref[g, :, :n_keys], ones, ovt], axis=0) for g in range(N_GROUPS)]
        imp_row = HEAD_DIM + BF16_ROWS
        blk = lax.broadcasted_iota(jnp.int32, (n_blk, QSUB), 0)
        blk_f = blk.astype(F32)

        def select(cols, importance):
            forced = (blk == 0) | (blk == cur[:, cols]) | (blk == cur[:, cols] - 1)
            causal = blk * SLC_BLOCK <= t[:, cols]
            for g in range(N_GROUPS):
                work = jnp.where(causal & ~forced, importance[g], -1.0)
                sel = jnp.where(forced, 1.0, 0.0)
                for _ in range(N_SELECT - N_FORCED):
                    m = jnp.max(work, axis=0, keepdims=True)
                    first = jnp.min(jnp.where(work == m, blk_f, float(n_blk)), axis=0, keepdims=True)
                    hit = blk_f == first
                    sel = jnp.where(hit, 1.0, sel)
                    work = jnp.where(hit, -jnp.inf, work)
                sel_ref[g * n_slc:g * n_slc + n_blk, cols] = jnp.where(sel > 0.0, 0.0, NEG)
                if n_blk < n_slc:
                    sel_ref[g * n_slc + n_blk:(g + 1) * n_slc, cols] = jnp.full((n_slc - n_blk, QSUB), NEG, F32)

        units = [(h, slice(qs * QSUB, (qs + 1) * QSUB)) for qs in range(tq // QSUB) for h in range(N_HEADS)]

        def scores(unit):
            h, cols = unit
            p, e = divmod(h, 2)
            g = p // PAIRS_PER_GROUP
            s = _dot_nt(kc_ref[g, e, :n_keys, :], q_ref[cols, p * LANES:(p + 1) * LANES]) + bias[:, cols]
            return s, jnp.max(s, axis=0, keepdims=True)

        importance = [None] * N_GROUPS
        pending = [scores(u) for u in units[:CMP_LOOKAHEAD]]
        for n, (h, cols) in enumerate(units):
            g = h // HEADS_PER_GROUP
            if n + CMP_LOOKAHEAD < len(units):
                pending.append(scores(units[n + CMP_LOOKAHEAD]))
            s, m = pending.pop(0)
            pr = jnp.exp2(s - m).astype(BF16)
            res = _dot(lhs[g], pr)
            inv_l = jnp.where(any_valid[:, cols], 1.0 / res[HEAD_DIM:HEAD_DIM + 1, :], 0.0)
            o_h = res[:HEAD_DIM, :] * (inv_l * gates[3 * h:3 * h + 1, cols])
            o_ref[h * HEAD_DIM:(h + 1) * HEAD_DIM, cols] = o_h.astype(BF16)
            imp_h = res[imp_row:, :] * inv_l
            importance[g] = imp_h if importance[g] is None else importance[g] + imp_h
            if h == N_HEADS - 1:
                select(cols, importance)
                importance = [None] * N_GROUPS

    chunk_tokens = CMP_KEY_CHUNK * CMP_STRIDE
    variant = (q0 + tq - 1) // chunk_tokens
    for v in range(n_cmp // CMP_KEY_CHUNK):
        pl.when(variant == v)(functools.partial(body, (v + 1) * CMP_KEY_CHUNK,
                                                (v + 1) * chunk_tokens // SLC_BLOCK))


def _cmp_attention(p_arr, kc, vct, overlap_t, g3t):
    s = p_arr.shape[0]
    n_cmp = kc.shape[2]
    n_slc = s // SLC_BLOCK
    return pl.pallas_call(
        functools.partial(_cmp_kernel, n_slc=n_slc),
        grid=(s // CMP_TQ,),
        in_specs=[
            pl.BlockSpec((CMP_TQ, D_MODEL), lambda i: (i, P_Q // 8)),
            pl.BlockSpec((N_GROUPS, 2, n_cmp, LANES), lambda i: (0, 0, 0, 0)),
            pl.BlockSpec((N_GROUPS, HEAD_DIM, n_cmp), lambda i: (0, 0, 0)),
            pl.BlockSpec((n_slc, n_cmp), lambda i: (0, 0)),
            pl.BlockSpec((LANES, CMP_TQ), lambda i: (0, i)),
        ],
        out_specs=[
            pl.BlockSpec((D_MODEL, CMP_TQ), lambda i: (0, i)),
            pl.BlockSpec((N_GROUPS * n_slc, CMP_TQ), lambda i: (0, i)),
        ],
        out_shape=[
            jax.ShapeDtypeStruct((D_MODEL, s), BF16),
            jax.ShapeDtypeStruct((N_GROUPS * n_slc, s), F32),
        ],
        compiler_params=_params(("parallel",)),
        name="cmp_attention",
    )(p_arr, kc, vct, overlap_t, g3t)


def _overlap_matrix_t(n_cmp, n_slc):
    c0 = np.arange(n_cmp)[None, :] * CMP_STRIDE
    s0 = np.arange(n_slc)[:, None] * SLC_BLOCK
    ov = (c0 < s0 + SLC_BLOCK) & (c0 + CMP_BLOCK > s0)
    return jnp.asarray(ov, dtype=BF16)


SLC_TQ = 1024
SLC_TK = 512
SLC_LOOKAHEAD = 4
ACC_ROWS = HEAD_DIM + BF16_ROWS


def _selected_kernel(qi_ref, ki_ref, fl_ref, q_ref, k_ref, vt_ref, g3t_ref, sel_ref, o_ref, m_ref, acc_ref, *,
                     n_slc):
    step = pl.program_id(0)
    tq = q_ref.shape[0]
    tk = k_ref.shape[0]
    flags = fl_ref[step]

    @pl.when((flags & 1) != 0)
    def _():
        m_ref[...] = jnp.full_like(m_ref, NEG)
        acc_ref[...] = jnp.zeros_like(acc_ref)

    q0 = qi_ref[step] * tq
    k0 = ki_ref[step] * tk
    t = q0 + lax.broadcasted_iota(jnp.int32, (1, tq), 1)
    key = k0 + lax.broadcasted_iota(jnp.int32, (tk, 1), 0)

    tile_blocks = tk // SLC_BLOCK
    causal_bias = jnp.where(key <= t, 0.0, NEG)
    bias = []
    for g in range(N_GROUPS):
        first_block = pl.multiple_of(g * n_slc + ki_ref[step] * tile_blocks, tile_blocks)
        rows = sel_ref[pl.ds(first_block, tile_blocks), :]
        per_key = jnp.broadcast_to(rows[:, None, :], (tile_blocks, SLC_BLOCK, tq)).reshape(tk, tq)
        bias.append(jnp.minimum(per_key, causal_bias))

    k_half = _split_heads(k_ref[...])
    vt = vt_ref[...]
    ones = jnp.ones((ACC_ROWS - HEAD_DIM, tk), BF16)
    v_aug = [jnp.concatenate([vt[g * HEAD_DIM:(g + 1) * HEAD_DIM, :], ones], axis=0) for g in range(N_GROUPS)]

    units = [(h, slice(qs * QSUB, (qs + 1) * QSUB)) for qs in range(tq // QSUB) for h in range(N_HEADS)]

    def scores(unit):
        h, cols = unit
        p, e = divmod(h, 2)
        g = p // PAIRS_PER_GROUP
        s = (_dot_nt(k_half[g][e], q_ref[cols, p * LANES:(p + 1) * LANES]) + bias[g][:, cols]).astype(BF16)
        return s, jnp.max(s, axis=0, keepdims=True)

    pending = [scores(u) for u in units[:SLC_LOOKAHEAD]]
    for n, (h, cols) in enumerate(units):
        g = h // HEADS_PER_GROUP
        if n + SLC_LOOKAHEAD < len(units):
            pending.append(scores(units[n + SLC_LOOKAHEAD]))
        s, s_max = pending.pop(0)
        m_prev = m_ref[h:h + 1, cols]
        m_new = jnp.maximum(m_prev, s_max.astype(F32))
        alpha = jnp.exp2(m_prev - m_new)
        pr = jnp.exp2(s - m_new.astype(BF16))
        m_ref[h:h + 1, cols] = m_new
        acc_ref[h, :, cols] = acc_ref[h, :, cols] * alpha + _dot(v_aug[g], pr)

    @pl.when((flags & 2) != 0)
    def _():
        gates = _sigmoid(g3t_ref[...].astype(F32))
        for h in range(N_HEADS):
            acc = acc_ref[h]
            scale = gates[3 * h + 1:3 * h + 2, :] / acc[HEAD_DIM:HEAD_DIM + 1, :]
            o_ref[h * HEAD_DIM:(h + 1) * HEAD_DIM, :] = (acc[:HEAD_DIM, :] * scale).astype(BF16)


def _causal_tile_tables(s, tq, tk):
    qi, ki, fl = [], [], []
    for i in range(s // tq):
        last = (i * tq + tq - 1) // tk
        for k in range(last + 1):
            qi.append(i)
            ki.append(k)
            fl.append((1 if k == 0 else 0) | (2 if k == last else 0))
    return (jnp.asarray(qi, jnp.int32), jnp.asarray(ki, jnp.int32), jnp.asarray(fl, jnp.int32))


def _selected_attention(p_arr, vt, g3t, sel_t):
    s = p_arr.shape[0]
    n_slc = s // SLC_BLOCK
    tq, tk = SLC_TQ, SLC_TK
    qi, ki, fl = _causal_tile_tables(s, tq, tk)
    grid_spec = pltpu.PrefetchScalarGridSpec(
        num_scalar_prefetch=3,
        grid=(qi.shape[0],),
        in_specs=[
            pl.BlockSpec((tq, D_MODEL), lambda n, qi, ki, fl: (qi[n], P_Q // 8)),
            pl.BlockSpec((tk, LANES), lambda n, qi, ki, fl: (ki[n], P_KS)),
            pl.BlockSpec((LANES, tk), lambda n, qi, ki, fl: (0, ki[n])),
            pl.BlockSpec((LANES, tq), lambda n, qi, ki, fl: (0, qi[n])),
            pl.BlockSpec((N_GROUPS * n_slc, tq), lambda n, qi, ki, fl: (0, qi[n])),
        ],
        out_specs=pl.BlockSpec((D_MODEL, tq), lambda n, qi, ki, fl: (0, qi[n])),
        scratch_shapes=[
            pltpu.VMEM((N_HEADS, tq), F32),
            pltpu.VMEM((N_HEADS, ACC_ROWS, tq), F32),
        ],
    )
    return pl.pallas_call(
        functools.partial(_selected_kernel, n_slc=n_slc),
        grid_spec=grid_spec,
        out_shape=jax.ShapeDtypeStruct((D_MODEL, s), BF16),
        compiler_params=_params(("arbitrary",)),
        name="selected_attention",
    )(qi, ki, fl, p_arr, p_arr, vt, g3t, sel_t)


WINDOW_TQ = 1024
WINDOW_BLOCK = 256
WINDOW_LOOKAHEAD = 4


def _window_kernel(q_ref, *refs, n_blocks):
    k_refs, vt_refs = refs[:n_blocks], refs[n_blocks:2 * n_blocks]
    g3t_ref, o_ref = refs[2 * n_blocks:]
    tq = q_ref.shape[0]
    span = WINDOW + QSUB
    k_half = _split_heads(jnp.concatenate([r[...] for r in k_refs], axis=0))
    vt = jnp.concatenate([r[...] for r in vt_refs], axis=1)
    ones = jnp.ones((BF16_ROWS, vt.shape[1]), BF16)
    v_aug = [jnp.concatenate([vt[g * HEAD_DIM:(g + 1) * HEAD_DIM, :], ones], axis=0) for g in range(N_GROUPS)]
    gates = _sigmoid(g3t_ref[...].astype(F32))

    r = lax.broadcasted_iota(jnp.int32, (span, 1), 0)
    c = lax.broadcasted_iota(jnp.int32, (1, QSUB), 1)
    in_band = (r > c) & (r <= c + WINDOW)
    first_key = pl.program_id(0) * tq - WINDOW
    n_sub = tq // QSUB
    bias = [jnp.where(in_band & (first_key + qs * QSUB + r >= 0), 0.0, NEG) for qs in range(n_sub)]

    units = [(h, qs) for qs in range(n_sub) for h in range(N_HEADS)]

    def scores(unit):
        h, qs = unit
        p, e = divmod(h, 2)
        g = p // PAIRS_PER_GROUP
        q = q_ref[qs * QSUB:(qs + 1) * QSUB, p * LANES:(p + 1) * LANES]
        s = (_dot_nt(k_half[g][e][qs * QSUB:qs * QSUB + span, :], q) + bias[qs]).astype(BF16)
        return s, jnp.max(s, axis=0, keepdims=True)

    pending = [scores(u) for u in units[:WINDOW_LOOKAHEAD]]
    for n, (h, qs) in enumerate(units):
        g = h // HEADS_PER_GROUP
        cols = slice(qs * QSUB, (qs + 1) * QSUB)
        if n + WINDOW_LOOKAHEAD < len(units):
            pending.append(scores(units[n + WINDOW_LOOKAHEAD]))
        s, m = pending.pop(0)
        res = _dot(v_aug[g][:, qs * QSUB:qs * QSUB + span], jnp.exp2(s - m))
        scale = gates[3 * h + 2:3 * h + 3, cols] / res[HEAD_DIM:HEAD_DIM + 1, :]
        o_ref[h * HEAD_DIM:(h + 1) * HEAD_DIM, cols] = (res[:HEAD_DIM, :] * scale).astype(BF16)


def _window_attention(p_arr, vt, g3t):
    s = p_arr.shape[0]
    tq = WINDOW_TQ
    n_blocks = (WINDOW + tq) // WINDOW_BLOCK
    back = WINDOW // WINDOW_BLOCK
    per_tile = tq // WINDOW_BLOCK

    def block(j):
        return lambda i: jnp.maximum(i * per_tile - back + j, 0)

    k_specs = [pl.BlockSpec((WINDOW_BLOCK, LANES), lambda i, b=block(j): (b(i), P_KW)) for j in range(n_blocks)]
    v_specs = [pl.BlockSpec((LANES, WINDOW_BLOCK), lambda i, b=block(j): (0, b(i))) for j in range(n_blocks)]
    return pl.pallas_call(
        functools.partial(_window_kernel, n_blocks=n_blocks),
        grid=(s // tq,),
        in_specs=[pl.BlockSpec((tq, D_MODEL), lambda i: (i, P_Q // 8))] + k_specs + v_specs
        + [pl.BlockSpec((LANES, tq), lambda i: (0, i))],
        out_specs=pl.BlockSpec((D_MODEL, tq), lambda i: (0, i)),
        out_shape=jax.ShapeDtypeStruct((D_MODEL, s), BF16),
        compiler_params=_params(("parallel",)),
        name="window_attention",
    )(p_arr, *([p_arr] * n_blocks), *([vt] * n_blocks), g3t)


OUT_TM = 512
PREV_ROWS = 16


def _out_kernel(x_ref, oc_ref, os_ref, ow_ref, cb_ref, cc_ref, cx_ref, pc_ref, px_ref, ga_ref, gb_ref,
                cw_ref, wa_ref, wb_ref, wo_ref, o_ref, u_ref):
    tm = x_ref.shape[0]
    prev = pc_ref[...].astype(F32) * px_ref[...].astype(F32)
    u_ref[0:PREV_ROWS, :] = jnp.where(pl.program_id(0) == 0, 0.0, prev)
    u_ref[PREV_ROWS:, :] = cc_ref[...].astype(F32) * cx_ref[...].astype(F32)
    cw = cw_ref[...]
    conv = (cw[0:1, :] * u_ref[pl.ds(PREV_ROWS - 2, tm), :]
            + cw[1:2, :] * u_ref[pl.ds(PREV_ROWS - 1, tm), :]
            + cw[2:3, :] * u_ref[pl.ds(PREV_ROWS, tm), :])
    y_b = _dot((cb_ref[...].astype(F32) * conv).astype(BF16), wb_ref[...])

    o_nsa_t = oc_ref[...].astype(F32) + os_ref[...].astype(F32) + ow_ref[...].astype(F32)
    y_a = _dot(o_nsa_t.T.astype(BF16), wa_ref[...])

    merged = _sigmoid(ga_ref[...].astype(F32)) * y_a + _sigmoid(gb_ref[...].astype(F32)) * y_b
    o_ref[...] = x_ref[...] + _dot(merged.astype(BF16), wo_ref[...])


def _out(x1, o_cmp_t, o_slc_t, o_win_t, p_arr, conv_w, w_nsa_out, w_conv_out, w_out):
    s = x1.shape[0]
    tm = OUT_TM
    row = lambda i: (i, 0)
    col = lambda i: (0, i)
    prev_blocks = tm // PREV_ROWS

    def seg(c):
        return pl.BlockSpec((tm, D_MODEL), lambda i: (i, c // 8))

    def seg_prev(c):
        return pl.BlockSpec((PREV_ROWS, D_MODEL), lambda i: (jnp.maximum(i * prev_blocks - 1, 0), c // 8))

    full = lambda shape: pl.BlockSpec(shape, lambda i: (0, 0))
    return pl.pallas_call(
        _out_kernel,
        grid=(s // tm,),
        in_specs=[
            pl.BlockSpec((tm, D_MODEL), row), pl.BlockSpec((D_MODEL, tm), col),
            pl.BlockSpec((D_MODEL, tm), col), pl.BlockSpec((D_MODEL, tm), col),
            seg(P_CB), seg(P_CC), seg(P_CX), seg_prev(P_CC), seg_prev(P_CX), seg(P_GA), seg(P_GB),
            full((CONV_K, CONV_WIDTH)), full((D_MODEL, D_MODEL)), full((CONV_WIDTH, D_MODEL)),
            full((D_MODEL, D_MODEL)),
        ],
        out_specs=pl.BlockSpec((tm, D_MODEL), row),
        out_shape=jax.ShapeDtypeStruct((s, D_MODEL), F32),
        scratch_shapes=[pltpu.VMEM((tm + PREV_ROWS, CONV_WIDTH), F32)],
        compiler_params=_params(("parallel",)),
        name="merge_out",
    )(x1, o_cmp_t, o_slc_t, o_win_t, p_arr, p_arr, p_arr, p_arr, p_arr, p_arr, p_arr,
      conv_w, w_nsa_out, w_conv_out, w_out)


def _layer(x, ffn1_norm, ffn1_w_gate, ffn1_w_up, ffn1_w_down, mix_norm, w_in, cmp_pe_k, cmp_pe_v,
           cmp_k_w1, cmp_k_w2, cmp_v_w1, cmp_v_w2, conv_w, w_nsa_out, w_conv_out, w_out,
           ffn2_norm, ffn2_w_gate, ffn2_w_up, ffn2_w_down, final_g, final_norm):
    s = x.shape[0]
    n_cmp = s // CMP_STRIDE
    n_slc = s // SLC_BLOCK
    row = lambda v: v.reshape(1, -1).astype(F32)

    x1 = _ffn(x, row(ffn1_norm), ffn1_w_gate.astype(BF16), ffn1_w_up.astype(BF16),
              ffn1_w_down.astype(BF16), row(final_g), False)
    p_arr = _proj(x1, row(mix_norm), _proj_weight(w_in))

    col = lambda c: p_arr[:, c * LANES:(c + 1) * LANES]
    chunks = lambda c: col(c).reshape(n_cmp, CMP_STRIDE * LANES)
    kc = _compress(chunks(P_KC), *_compress_weights(cmp_pe_k, cmp_k_w1, cmp_k_w2, False), False)
    vct = _compress(chunks(P_VC), *_compress_weights(cmp_pe_v, cmp_v_w1, cmp_v_w2, True), True)

    g3t = col(P_G3).T
    o_cmp_t, sel_t = _cmp_attention(p_arr, kc, vct, _overlap_matrix_t(n_cmp, n_slc), g3t)
    o_slc_t = _selected_attention(p_arr, col(P_VS).T, g3t, sel_t)
    o_win_t = _window_attention(p_arr, col(P_VW).T, g3t)
    x2 = _out(x1, o_cmp_t, o_slc_t, o_win_t, p_arr, conv_w.astype(F32), w_nsa_out.astype(BF16),
              w_conv_out.astype(BF16), w_out.astype(BF16))
    return _ffn(x2, row(ffn2_norm), ffn2_w_gate.astype(BF16), ffn2_w_up.astype(BF16),
                ffn2_w_down.astype(BF16), row(final_g), final_norm)


def kernel(x, ffn1_norm, ffn1_w_gate, ffn1_w_up, ffn1_w_down, mix_norm, w_in, cmp_pe_k, cmp_pe_v,
           cmp_k_w1, cmp_k_w2, cmp_v_w1, cmp_v_w2, conv_w, w_nsa_out, w_conv_out, w_out, ffn2_norm,
           ffn2_w_gate, ffn2_w_up, ffn2_w_down, final_norm):
    batch, _, _ = x.shape
    depth = ffn1_norm.shape[0]
    outs = []
    for b in range(batch):
        xb = x[b]
        for l in range(depth):
            xb = _layer(xb, ffn1_norm[l], ffn1_w_gate[l], ffn1_w_up[l], ffn1_w_down[l], mix_norm[l],
                        w_in[l], cmp_pe_k[l], cmp_pe_v[l], cmp_k_w1[l], cmp_k_w2[l], cmp_v_w1[l],
                        cmp_v_w2[l], conv_w[l], w_nsa_out[l], w_conv_out[l], w_out[l], ffn2_norm[l],
                        ffn2_w_gate[l], ffn2_w_up[l], ffn2_w_down[l], final_norm, l == depth - 1)
        outs.append(xb)
    return jnp.stack(outs)
```

```python
import functools

import jax
import jax.numpy as jnp
import numpy as np
from jax import lax
from jax.experimental import pallas as pl
from jax.experimental.pallas import tpu as pltpu

D_MODEL = 1024
N_HEADS = 16
HEAD_DIM = 64
N_GROUPS = 2
HEADS_PER_GROUP = N_HEADS // N_GROUPS
N_PAIRS = N_HEADS // 2
PAIRS_PER_GROUP = N_PAIRS // N_GROUPS
CMP_BLOCK = 32
CMP_STRIDE = 16
CMP_HIDDEN = 256
SLC_BLOCK = 64
N_SELECT = 16
WINDOW = 512
CONV_WIDTH = 1024
CONV_K = 3
D_FF = 2816
EPS = 1e-6
NEG = -(2.0 ** 100)
N_FORCED = 3
LOG2E = float(np.log2(np.e))
LANES = 128
BF16_ROWS = 16

P_Q, P_CB, P_CC, P_CX, P_GA, P_GB = 0, 8, 16, 24, 32, 40
P_KC, P_VC, P_KS, P_VS, P_KW, P_VW, P_G3 = 48, 49, 50, 51, 52, 53, 54
P_BLOCKS = 56
P_COLS = P_BLOCKS * LANES

F32 = jnp.float32
BF16 = jnp.bfloat16

VMEM_LIMIT = 52 * 1024 * 1024


def _dot(a, b):
    return jnp.dot(a, b, preferred_element_type=F32)


def _dot_nt(a, b):
    return lax.dot_general(a, b, (((1,), (1,)), ((), ())), preferred_element_type=F32)


def _sigmoid(x):
    return 1.0 / (1.0 + jnp.exp(-x))


def _rms(x, g):
    return x * lax.rsqrt(jnp.mean(x * x, axis=-1, keepdims=True) + EPS) * g


def _params(sem, limit=VMEM_LIMIT, flags=None):
    return pltpu.CompilerParams(dimension_semantics=sem, vmem_limit_bytes=limit, flags=flags)


FFN_TM = 512
FFN_CHUNK = 256


def _ffn_kernel(x_ref, g_ref, wg_ref, wu_ref, wd_ref, fg_ref, o_ref, *, final_norm):
    x = x_ref[...]
    h = _rms(x, g_ref[...]).astype(BF16)
    acc = None
    for c in range(D_FF // FFN_CHUNK):
        cols = slice(c * FFN_CHUNK, (c + 1) * FFN_CHUNK)
        gate = _dot(h, wg_ref[:, cols])
        up = _dot(h, wu_ref[:, cols])
        a = (gate * _sigmoid(gate) * up).astype(BF16)
        part = _dot(a, wd_ref[cols, :])
        acc = part if acc is None else acc + part
    y = x + 0.5 * acc
    if final_norm:
        y = _rms(y, fg_ref[...])
    o_ref[...] = y


def _ffn(x, norm_g, wg, wu, wd, final_g, final_norm):
    s = x.shape[0]
    resident = lambda shape: pl.BlockSpec(shape, lambda i: (0, 0), pipeline_mode=pl.Buffered(1))
    return pl.pallas_call(
        functools.partial(_ffn_kernel, final_norm=final_norm),
        grid=(s // FFN_TM,),
        in_specs=[
            pl.BlockSpec((FFN_TM, D_MODEL), lambda i: (i, 0)),
            resident((1, D_MODEL)),
            resident((D_MODEL, D_FF)),
            resident((D_MODEL, D_FF)),
            resident((D_FF, D_MODEL)),
            resident((1, D_MODEL)),
        ],
        out_specs=pl.BlockSpec((FFN_TM, D_MODEL), lambda i: (i, 0)),
        out_shape=jax.ShapeDtypeStruct((s, D_MODEL), F32),
        compiler_params=_params(("parallel",)),
        name="ffn_final" if final_norm else "ffn",
    )(x, norm_g, wg, wu, wd, final_g)


PROJ_TM = 512
PROJ_CHUNK = 512


P_TRANSPOSED = (P_VS, P_VW, P_G3)


def _proj_kernel(x_ref, g_ref, w_ref, o_ref, *t_refs):
    h = _rms(x_ref[...], g_ref[...]).astype(BF16)
    chunk_blocks = PROJ_CHUNK // LANES
    for c in range(P_COLS // PROJ_CHUNK):
        cols = slice(c * PROJ_CHUNK, (c + 1) * PROJ_CHUNK)
        y = _dot(h, w_ref[:, cols])
        o_ref[:, cols] = y.astype(BF16)
        for t_ref, blk in zip(t_refs, P_TRANSPOSED):
            if blk // chunk_blocks == c:
                lo = (blk % chunk_blocks) * LANES
                t_ref[...] = y[:, lo:lo + LANES].T.astype(BF16)


def _proj(x, norm_g, w):
    s = x.shape[0]
    resident = lambda shape: pl.BlockSpec(shape, lambda i: (0, 0), pipeline_mode=pl.Buffered(1))
    return pl.pallas_call(
        _proj_kernel,
        grid=(s // PROJ_TM,),
        in_specs=[
            pl.BlockSpec((PROJ_TM, D_MODEL), lambda i: (i, 0)),
            resident((1, D_MODEL)),
            resident((D_MODEL, P_COLS)),
        ],
        out_specs=[pl.BlockSpec((PROJ_TM, P_COLS), lambda i: (i, 0))]
        + [pl.BlockSpec((LANES, PROJ_TM), lambda i: (0, i)) for _ in P_TRANSPOSED],
        out_shape=[jax.ShapeDtypeStruct((s, P_COLS), BF16)]
        + [jax.ShapeDtypeStruct((LANES, s), BF16) for _ in P_TRANSPOSED],
        compiler_params=_params(("parallel",)),
        name="proj",
    )(x, norm_g, w)


def _proj_weight(w_in):
    q, kc, vc, ks, vs, kw, vw, g3, cb, cc, cx, ga, gb = jnp.split(
        w_in, np.cumsum([1024, 128, 128, 128, 128, 128, 128, 48, 1024, 1024, 1024, 1024]).tolist(), axis=1)
    g3 = jnp.pad(g3, ((0, 0), (0, (P_BLOCKS - P_G3) * LANES - g3.shape[1])))
    w = jnp.concatenate([q * (HEAD_DIM ** -0.5 * LOG2E), cb, cc, cx, ga, gb, kc, vc, ks, vs, kw, vw, g3], axis=1)
    return w.astype(BF16)


def _gelu_tanh(x):
    return 0.5 * x * (1.0 + jnp.tanh(np.sqrt(2.0 / np.pi).astype(np.float32) * (x + 0.044715 * (x * x * x))))


def _compress_kernel(cf_ref, pe_ref, w1_ref, w2_ref, o_ref, *, transposed):
    cf = cf_ref[...].astype(F32)
    top = (cf + pe_ref[0]).astype(BF16)
    bot = (cf + pe_ref[1]).astype(BF16)
    a = _dot(top, w1_ref[0, 0])
    b = _dot(bot, w1_ref[0, 1])
    n = a.shape[0]
    pre = a + pltpu.roll(b, n - 1, 0)
    h = _gelu_tanh(pre).astype(BF16)
    if transposed:
        o_ref[0] = _dot_nt(w2_ref[...], h).astype(BF16)
    else:
        o_ref[0, 0] = _dot(h, w2_ref[0]).astype(BF16)
        o_ref[0, 1] = _dot(h, w2_ref[1]).astype(BF16)


def _compress(cf, pe, w1, w2, transposed):
    n = cf.shape[0]
    width = CMP_STRIDE * LANES
    if transposed:
        w2_spec = pl.BlockSpec((HEAD_DIM, CMP_HIDDEN), lambda g: (0, 0))
        out_spec = pl.BlockSpec((1, HEAD_DIM, n), lambda g: (g, 0, 0))
        out_shape = jax.ShapeDtypeStruct((N_GROUPS, HEAD_DIM, n), BF16)
    else:
        w2_spec = pl.BlockSpec((2, CMP_HIDDEN, LANES), lambda g: (0, 0, 0))
        out_spec = pl.BlockSpec((1, 2, n, LANES), lambda g: (g, 0, 0, 0))
        out_shape = jax.ShapeDtypeStruct((N_GROUPS, 2, n, LANES), BF16)
    return pl.pallas_call(
        functools.partial(_compress_kernel, transposed=transposed),
        grid=(N_GROUPS,),
        in_specs=[
            pl.BlockSpec((n, width), lambda g: (0, 0)),
            pl.BlockSpec((2, 1, width), lambda g: (0, 0, 0)),
            pl.BlockSpec((1, 2, width, CMP_HIDDEN), lambda g: (g, 0, 0, 0)),
            w2_spec,
        ],
        out_specs=out_spec,
        out_shape=out_shape,
        compiler_params=_params(("parallel",)),
        name="compress_v" if transposed else "compress_k",
    )(cf, pe, w1, w2)


def _compress_weights(pe, w1, w2, transposed):
    w1h = w1.reshape(2, CMP_STRIDE, HEAD_DIM, CMP_HIDDEN)
    zeros = jnp.zeros_like(w1h)
    w1g = jnp.stack([jnp.concatenate([w1h, zeros], axis=2), jnp.concatenate([zeros, w1h], axis=2)])
    w1g = w1g.reshape(N_GROUPS, 2, CMP_STRIDE * LANES, CMP_HIDDEN).astype(BF16)
    peh = pe.reshape(2, CMP_STRIDE, HEAD_DIM)
    pe2 = jnp.concatenate([peh, peh], axis=2).reshape(2, 1, CMP_STRIDE * LANES)
    if transposed:
        w2p = w2.T.astype(BF16)
    else:
        z2 = jnp.zeros_like(w2)
        w2p = jnp.stack([jnp.concatenate([w2, z2], axis=1), jnp.concatenate([z2, w2], axis=1)]).astype(BF16)
    return pe2, w1g, w2p


def _split_heads(kt):
    lo = lax.broadcasted_iota(jnp.int32, kt.shape, 1) < HEAD_DIM
    kr = pltpu.roll(kt, HEAD_DIM, 1)
    zero = jnp.zeros_like(kt)
    return [[jnp.where(lo, kt, zero), jnp.where(lo, zero, kr)],
            [jnp.where(lo, kr, zero), jnp.where(lo, zero, kt)]]


QSUB = 256
CMP_TQ = 256
CMP_KEY_CHUNK = 256
CMP_LOOKAHEAD = 4


def _cmp_kernel(q_ref, kc_ref, vct_ref, ovt_ref, g3t_ref, o_ref, sel_ref, *, n_slc):
    tq = q_ref.shape[0]
    n_cmp = kc_ref.shape[2]
    q0 = pl.program_id(0) * tq
    t = q0 + lax.broadcasted_iota(jnp.int32, (1, tq), 1)
    any_valid = t >= CMP_BLOCK - 1
    cur = t // SLC_BLOCK

    def body(n_keys, n_blk):
        c_end = lax.broadcasted_iota(jnp.int32, (n_keys, 1), 0) * CMP_STRIDE + (CMP_BLOCK - 1)
        bias = jnp.where(c_end <= t, 0.0, NEG)
        gates = _sigmoid(g3t_ref[...].astype(F32))

        ovt = ovt_ref[:n_blk, :n_keys]
        ones = jnp.ones((BF16_ROWS, n_keys), BF16)
        lhs = [jnp.concatenate([vct_ref[g, :, :n_keys], ones, ovt], axis=0) for g in range(N_GROUPS)]
        imp_row = HEAD_DIM + BF16_ROWS
        blk = lax.broadcasted_iota(jnp.int32, (n_blk, QSUB), 0)
        blk_f = blk.astype(F32)

        def select(cols, importance):
            forced = (blk == 0) | (blk == cur[:, cols]) | (blk == cur[:, cols] - 1)
            causal = blk * SLC_BLOCK <= t[:, cols]
            for g in range(N_GROUPS):
                work = jnp.where(causal & ~forced, importance[g], -1.0)
                sel = jnp.where(forced, 1.0, 0.0)
                for _ in range(N_SELECT - N_FORCED):
                    m = jnp.max(work, axis=0, keepdims=True)
                    first = jnp.min(jnp.where(work == m, blk_f, float(n_blk)), axis=0, keepdims=True)
                    hit = blk_f == first
                    sel = jnp.where(hit, 1.0, sel)
                    work = jnp.where(hit, -jnp.inf, work)
                sel_ref[g * n_slc:g * n_slc + n_blk, cols] = jnp.where(sel > 0.0, 0.0, NEG)
                if n_blk < n_slc:
                    sel_ref[g * n_slc + n_blk:(g + 1) * n_slc, cols] = jnp.full((n_slc - n_blk, QSUB), NEG, F32)

        units = [(h, slice(qs * QSUB, (qs + 1) * QSUB)) for qs in range(tq // QSUB) for h in range(N_HEADS)]

        def scores(unit):
            h, cols = unit
            p, e = divmod(h, 2)
            g = p // PAIRS_PER_GROUP
            s = _dot_nt(kc_ref[g, e, :n_keys, :], q_ref[cols, p * LANES:(p + 1) * LANES]) + bias[:, cols]
            return s, jnp.max(s, axis=0, keepdims=True)

        importance = [None] * N_GROUPS
        pending = [scores(u) for u in units[:CMP_LOOKAHEAD]]
        for n, (h, cols) in enumerate(units):
            g = h // HEADS_PER_GROUP
            if n + CMP_LOOKAHEAD < len(units):
                pending.append(scores(units[n + CMP_LOOKAHEAD]))
            s, m = pending.pop(0)
            pr = jnp.exp2(s - m).astype(BF16)
            res = _dot(lhs[g], pr)
            inv_l = jnp.where(any_valid[:, cols], 1.0 / res[HEAD_DIM:HEAD_DIM + 1, :], 0.0)
            o_h = res[:HEAD_DIM, :] * (inv_l * gates[3 * h:3 * h + 1, cols])
            o_ref[h * HEAD_DIM:(h + 1) * HEAD_DIM, cols] = o_h.astype(BF16)
            imp_h = res[imp_row:, :] * inv_l
            importance[g] = imp_h if importance[g] is None else importance[g] + imp_h
            if h == N_HEADS - 1:
                select(cols, importance)
                importance = [None] * N_GROUPS

    chunk_tokens = CMP_KEY_CHUNK * CMP_STRIDE
    variant = (q0 + tq - 1) // chunk_tokens
    for v in range(n_cmp // CMP_KEY_CHUNK):
        pl.when(variant == v)(functools.partial(body, (v + 1) * CMP_KEY_CHUNK,
                                                (v + 1) * chunk_tokens // SLC_BLOCK))


def _cmp_attention(p_arr, kc, vct, overlap_t, g3t):
    s = p_arr.shape[0]
    n_cmp = kc.shape[2]
    n_slc = s // SLC_BLOCK
    return pl.pallas_call(
        functools.partial(_cmp_kernel, n_slc=n_slc),
        grid=(s // CMP_TQ,),
        in_specs=[
            pl.BlockSpec((CMP_TQ, D_MODEL), lambda i: (i, P_Q // 8)),
            pl.BlockSpec((N_GROUPS, 2, n_cmp, LANES), lambda i: (0, 0, 0, 0)),
            pl.BlockSpec((N_GROUPS, HEAD_DIM, n_cmp), lambda i: (0, 0, 0)),
            pl.BlockSpec((n_slc, n_cmp), lambda i: (0, 0)),
            pl.BlockSpec((LANES, CMP_TQ), lambda i: (0, i)),
        ],
        out_specs=[
            pl.BlockSpec((D_MODEL, CMP_TQ), lambda i: (0, i)),
            pl.BlockSpec((N_GROUPS * n_slc, CMP_TQ), lambda i: (0, i)),
        ],
        out_shape=[
            jax.ShapeDtypeStruct((D_MODEL, s), BF16),
            jax.ShapeDtypeStruct((N_GROUPS * n_slc, s), F32),
        ],
        compiler_params=_params(("parallel",)),
        name="cmp_attention",
    )(p_arr, kc, vct, overlap_t, g3t)


def _overlap_matrix_t(n_cmp, n_slc):
    c0 = np.arange(n_cmp)[None, :] * CMP_STRIDE
    s0 = np.arange(n_slc)[:, None] * SLC_BLOCK
    ov = (c0 < s0 + SLC_BLOCK) & (c0 + CMP_BLOCK > s0)
    return jnp.asarray(ov, dtype=BF16)


SLC_TQ = 1024
SLC_TK = 512
SLC_LOOKAHEAD = 4
ACC_ROWS = HEAD_DIM + BF16_ROWS


def _selected_kernel(qi_ref, ki_ref, fl_ref, q_ref, k_ref, vt_ref, g3t_ref, sel_ref, o_ref, m_ref, acc_ref, *,
                     n_slc):
    step = pl.program_id(0)
    tq = q_ref.shape[0]
    tk = k_ref.shape[0]
    flags = fl_ref[step]

    @pl.when((flags & 1) != 0)
    def _():
        m_ref[...] = jnp.full_like(m_ref, NEG)
        acc_ref[...] = jnp.zeros_like(acc_ref)

    def attend(first_sub):
        c0 = first_sub * QSUB
        t = qi_ref[step] * tq + c0 + lax.broadcasted_iota(jnp.int32, (1, tq - c0), 1)
        key = ki_ref[step] * tk + lax.broadcasted_iota(jnp.int32, (tk, 1), 0)

        tile_blocks = tk // SLC_BLOCK
        causal_bias = jnp.where(key <= t, 0.0, NEG)
        bias = []
        for g in range(N_GROUPS):
            first_block = pl.multiple_of(g * n_slc + ki_ref[step] * tile_blocks, tile_blocks)
            rows = sel_ref[pl.ds(first_block, tile_blocks), c0:]
            per_key = jnp.broadcast_to(rows[:, None, :], (tile_blocks, SLC_BLOCK, tq - c0)).reshape(tk, tq - c0)
            bias.append(jnp.minimum(per_key, causal_bias))

        k_half = _split_heads(k_ref[...])
        vt = vt_ref[...]
        ones = jnp.ones((ACC_ROWS - HEAD_DIM, tk), BF16)
        v_aug = [jnp.concatenate([vt[g * HEAD_DIM:(g + 1) * HEAD_DIM, :], ones], axis=0) for g in range(N_GROUPS)]

        units = [(h, qs) for qs in range(first_sub, tq // QSUB) for h in range(N_HEADS)]

        def scores(unit):
            h, qs = unit
            p, e = divmod(h, 2)
            g = p // PAIRS_PER_GROUP
            q = q_ref[qs * QSUB:(qs + 1) * QSUB, p * LANES:(p + 1) * LANES]
            s = (_dot_nt(k_half[g][e], q) + bias[g][:, qs * QSUB - c0:(qs + 1) * QSUB - c0]).astype(BF16)
            return s, jnp.max(s, axis=0, keepdims=True)

        pending = [scores(u) for u in units[:SLC_LOOKAHEAD]]
        for n, (h, qs) in enumerate(units):
            g = h // HEADS_PER_GROUP
            cols = slice(qs * QSUB, (qs + 1) * QSUB)
            if n + SLC_LOOKAHEAD < len(units):
                pending.append(scores(units[n + SLC_LOOKAHEAD]))
            s, s_max = pending.pop(0)
            m_prev = m_ref[h:h + 1, cols]
            m_new = jnp.maximum(m_prev, s_max.astype(F32))
            alpha = jnp.exp2(m_prev - m_new)
            pr = jnp.exp2(s - m_new.astype(BF16))
            m_ref[h:h + 1, cols] = m_new
            acc_ref[h, :, cols] = acc_ref[h, :, cols] * alpha + _dot(v_aug[g], pr)

    is_last = (flags & 2) != 0

    @pl.when(jnp.logical_not(is_last))
    def _():
        attend(0)

    @pl.when(is_last)
    def _():
        attend((tq - tk) // QSUB)
        gates = _sigmoid(g3t_ref[...].astype(F32))
        for h in range(N_HEADS):
            acc = acc_ref[h]
            scale = gates[3 * h + 1:3 * h + 2, :] / acc[HEAD_DIM:HEAD_DIM + 1, :]
            o_ref[h * HEAD_DIM:(h + 1) * HEAD_DIM, :] = (acc[:HEAD_DIM, :] * scale).astype(BF16)


def _causal_tile_tables(s, tq, tk):
    qi, ki, fl = [], [], []
    for i in range(s // tq):
        last = (i * tq + tq - 1) // tk
        for k in range(last + 1):
            qi.append(i)
            ki.append(k)
            fl.append((1 if k == 0 else 0) | (2 if k == last else 0))
    return (jnp.asarray(qi, jnp.int32), jnp.asarray(ki, jnp.int32), jnp.asarray(fl, jnp.int32))


def _selected_attention(p_arr, vt, g3t, sel_t):
    s = p_arr.shape[0]
    n_slc = s // SLC_BLOCK
    tq, tk = SLC_TQ, SLC_TK
    assert tq % tk == 0 and tk % QSUB == 0, "the kernel skips query slices that precede the last key tile"
    qi, ki, fl = _causal_tile_tables(s, tq, tk)
    grid_spec = pltpu.PrefetchScalarGridSpec(
        num_scalar_prefetch=3,
        grid=(qi.shape[0],),
        in_specs=[
            pl.BlockSpec((tq, D_MODEL), lambda n, qi, ki, fl: (qi[n], P_Q // 8)),
            pl.BlockSpec((tk, LANES), lambda n, qi, ki, fl: (ki[n], P_KS)),
            pl.BlockSpec((LANES, tk), lambda n, qi, ki, fl: (0, ki[n])),
            pl.BlockSpec((LANES, tq), lambda n, qi, ki, fl: (0, qi[n])),
            pl.BlockSpec((N_GROUPS * n_slc, tq), lambda n, qi, ki, fl: (0, qi[n])),
        ],
        out_specs=pl.BlockSpec((D_MODEL, tq), lambda n, qi, ki, fl: (0, qi[n])),
        scratch_shapes=[
            pltpu.VMEM((N_HEADS, tq), F32),
            pltpu.VMEM((N_HEADS, ACC_ROWS, tq), F32),
        ],
    )
    return pl.pallas_call(
        functools.partial(_selected_kernel, n_slc=n_slc),
        grid_spec=grid_spec,
        out_shape=jax.ShapeDtypeStruct((D_MODEL, s), BF16),
        compiler_params=_params(("arbitrary",)),
        name="selected_attention",
    )(qi, ki, fl, p_arr, p_arr, vt, g3t, sel_t)


WINDOW_TQ = 1024
WINDOW_BLOCK = 256
WINDOW_LOOKAHEAD = 4


def _window_kernel(q_ref, *refs, n_blocks):
    k_refs, vt_refs = refs[:n_blocks], refs[n_blocks:2 * n_blocks]
    g3t_ref, o_ref = refs[2 * n_blocks:]
    tq = q_ref.shape[0]
    span = WINDOW + QSUB
    k_half = _split_heads(jnp.concatenate([r[...] for r in k_refs], axis=0))
    vt = jnp.concatenate([r[...] for r in vt_refs], axis=1)
    ones = jnp.ones((BF16_ROWS, vt.shape[1]), BF16)
    v_aug = [jnp.concatenate([vt[g * HEAD_DIM:(g + 1) * HEAD_DIM, :], ones], axis=0) for g in range(N_GROUPS)]
    gates = _sigmoid(g3t_ref[...].astype(F32))

    r = lax.broadcasted_iota(jnp.int32, (span, 1), 0)
    c = lax.broadcasted_iota(jnp.int32, (1, QSUB), 1)
    in_band = (r > c) & (r <= c + WINDOW)
    first_key = pl.program_id(0) * tq - WINDOW
    n_sub = tq // QSUB
    bias = [jnp.where(in_band & (first_key + qs * QSUB + r >= 0), 0.0, NEG) for qs in range(n_sub)]

    units = [(h, qs) for qs in range(n_sub) for h in range(N_HEADS)]

    def scores(unit):
        h, qs = unit
        p, e = divmod(h, 2)
        g = p // PAIRS_PER_GROUP
        q = q_ref[qs * QSUB:(qs + 1) * QSUB, p * LANES:(p + 1) * LANES]
        s = (_dot_nt(k_half[g][e][qs * QSUB:qs * QSUB + span, :], q) + bias[qs]).astype(BF16)
        return s, jnp.max(s, axis=0, keepdims=True)

    pending = [scores(u) for u in units[:WINDOW_LOOKAHEAD]]
    for n, (h, qs) in enumerate(units):
        g = h // HEADS_PER_GROUP
        cols = slice(qs * QSUB, (qs + 1) * QSUB)
        if n + WINDOW_LOOKAHEAD < len(units):
            pending.append(scores(units[n + WINDOW_LOOKAHEAD]))
        s, m = pending.pop(0)
        res = _dot(v_aug[g][:, qs * QSUB:qs * QSUB + span], jnp.exp2(s - m))
        scale = gates[3 * h + 2:3 * h + 3, cols] / res[HEAD_DIM:HEAD_DIM + 1, :]
        o_ref[h * HEAD_DIM:(h + 1) * HEAD_DIM, cols] = (res[:HEAD_DIM, :] * scale).astype(BF16)


def _window_attention(p_arr, vt, g3t):
    s = p_arr.shape[0]
    tq = WINDOW_TQ
    n_blocks = (WINDOW + tq) // WINDOW_BLOCK
    back = WINDOW // WINDOW_BLOCK
    per_tile = tq // WINDOW_BLOCK

    def block(j):
        return lambda i: jnp.maximum(i * per_tile - back + j, 0)

    k_specs = [pl.BlockSpec((WINDOW_BLOCK, LANES), lambda i, b=block(j): (b(i), P_KW)) for j in range(n_blocks)]
    v_specs = [pl.BlockSpec((LANES, WINDOW_BLOCK), lambda i, b=block(j): (0, b(i))) for j in range(n_blocks)]
    return pl.pallas_call(
        functools.partial(_window_kernel, n_blocks=n_blocks),
        grid=(s // tq,),
        in_specs=[pl.BlockSpec((tq, D_MODEL), lambda i: (i, P_Q // 8))] + k_specs + v_specs
        + [pl.BlockSpec((LANES, tq), lambda i: (0, i))],
        out_specs=pl.BlockSpec((D_MODEL, tq), lambda i: (0, i)),
        out_shape=jax.ShapeDtypeStruct((D_MODEL, s), BF16),
        compiler_params=_params(("parallel",)),
        name="window_attention",
    )(p_arr, *([p_arr] * n_blocks), *([vt] * n_blocks), g3t)


OUT_TM = 512
PREV_ROWS = 16


def _out_kernel(x_ref, oc_ref, os_ref, ow_ref, cb_ref, cc_ref, cx_ref, pc_ref, px_ref, ga_ref, gb_ref,
                cw_ref, wa_ref, wb_ref, wo_ref, o_ref, u_ref):
    tm = x_ref.shape[0]
    prev = pc_ref[...].astype(F32) * px_ref[...].astype(F32)
    u_ref[0:PREV_ROWS, :] = jnp.where(pl.program_id(0) == 0, 0.0, prev)
    u_ref[PREV_ROWS:, :] = cc_ref[...].astype(F32) * cx_ref[...].astype(F32)
    cw = cw_ref[...]
    conv = (cw[0:1, :] * u_ref[pl.ds(PREV_ROWS - 2, tm), :]
            + cw[1:2, :] * u_ref[pl.ds(PREV_ROWS - 1, tm), :]
            + cw[2:3, :] * u_ref[pl.ds(PREV_ROWS, tm), :])
    y_b = _dot((cb_ref[...].astype(F32) * conv).astype(BF16), wb_ref[...])

    o_nsa_t = oc_ref[...].astype(F32) + os_ref[...].astype(F32) + ow_ref[...].astype(F32)
    y_a = _dot(o_nsa_t.T.astype(BF16), wa_ref[...])

    merged = _sigmoid(ga_ref[...].astype(F32)) * y_a + _sigmoid(gb_ref[...].astype(F32)) * y_b
    o_ref[...] = x_ref[...] + _dot(merged.astype(BF16), wo_ref[...])


def _out(x1, o_cmp_t, o_slc_t, o_win_t, p_arr, conv_w, w_nsa_out, w_conv_out, w_out):
    s = x1.shape[0]
    tm = OUT_TM
    row = lambda i: (i, 0)
    col = lambda i: (0, i)
    prev_blocks = tm // PREV_ROWS

    def seg(c):
        return pl.BlockSpec((tm, D_MODEL), lambda i: (i, c // 8))

    def seg_prev(c):
        return pl.BlockSpec((PREV_ROWS, D_MODEL), lambda i: (jnp.maximum(i * prev_blocks - 1, 0), c // 8))

    full = lambda shape: pl.BlockSpec(shape, lambda i: (0, 0))
    return pl.pallas_call(
        _out_kernel,
        grid=(s // tm,),
        in_specs=[
            pl.BlockSpec((tm, D_MODEL), row), pl.BlockSpec((D_MODEL, tm), col),
            pl.BlockSpec((D_MODEL, tm), col), pl.BlockSpec((D_MODEL, tm), col),
            seg(P_CB), seg(P_CC), seg(P_CX), seg_prev(P_CC), seg_prev(P_CX), seg(P_GA), seg(P_GB),
            full((CONV_K, CONV_WIDTH)), full((D_MODEL, D_MODEL)), full((CONV_WIDTH, D_MODEL)),
            full((D_MODEL, D_MODEL)),
        ],
        out_specs=pl.BlockSpec((tm, D_MODEL), row),
        out_shape=jax.ShapeDtypeStruct((s, D_MODEL), F32),
        scratch_shapes=[pltpu.VMEM((tm + PREV_ROWS, CONV_WIDTH), F32)],
        compiler_params=_params(("parallel",)),
        name="merge_out",
    )(x1, o_cmp_t, o_slc_t, o_win_t, p_arr, p_arr, p_arr, p_arr, p_arr, p_arr, p_arr,
      conv_w, w_nsa_out, w_conv_out, w_out)


def _layer(x, ffn1_norm, ffn1_w_gate, ffn1_w_up, ffn1_w_down, mix_norm, w_in, cmp_pe_k, cmp_pe_v,
           cmp_k_w1, cmp_k_w2, cmp_v_w1, cmp_v_w2, conv_w, w_nsa_out, w_conv_out, w_out,
           ffn2_norm, ffn2_w_gate, ffn2_w_up, ffn2_w_down, final_g, final_norm):
    s = x.shape[0]
    n_cmp = s // CMP_STRIDE
    n_slc = s // SLC_BLOCK
    row = lambda v: v.reshape(1, -1).astype(F32)

    x1 = _ffn(x, row(ffn1_norm), ffn1_w_gate.astype(BF16), ffn1_w_up.astype(BF16),
              ffn1_w_down.astype(BF16), row(final_g), False)
    p_arr, vst, vwt, g3t = _proj(x1, row(mix_norm), _proj_weight(w_in))

    col = lambda c: p_arr[:, c * LANES:(c + 1) * LANES]
    chunks = lambda c: col(c).reshape(n_cmp, CMP_STRIDE * LANES)
    kc = _compress(chunks(P_KC), *_compress_weights(cmp_pe_k, cmp_k_w1, cmp_k_w2, False), False)
    vct = _compress(chunks(P_VC), *_compress_weights(cmp_pe_v, cmp_v_w1, cmp_v_w2, True), True)

    o_cmp_t, sel_t = _cmp_attention(p_arr, kc, vct, _overlap_matrix_t(n_cmp, n_slc), g3t)
    o_slc_t = _selected_attention(p_arr, vst, g3t, sel_t)
    o_win_t = _window_attention(p_arr, vwt, g3t)
    x2 = _out(x1, o_cmp_t, o_slc_t, o_win_t, p_arr, conv_w.astype(F32), w_nsa_out.astype(BF16),
              w_conv_out.astype(BF16), w_out.astype(BF16))
    return _ffn(x2, row(ffn2_norm), ffn2_w_gate.astype(BF16), ffn2_w_up.astype(BF16),
                ffn2_w_down.astype(BF16), row(final_g), final_norm)


def kernel(x, ffn1_norm, ffn1_w_gate, ffn1_w_up, ffn1_w_down, mix_norm, w_in, cmp_pe_k, cmp_pe_v,
           cmp_k_w1, cmp_k_w2, cmp_v_w1, cmp_v_w2, conv_w, w_nsa_out, w_conv_out, w_out, ffn2_norm,
           ffn2_w_gate, ffn2_w_up, ffn2_w_down, final_norm):
    batch, _, _ = x.shape
    depth = ffn1_norm.shape[0]
    outs = []
    for b in range(batch):
        xb = x[b]
        for l in range(depth):
            xb = _layer(xb, ffn1_norm[l], ffn1_w_gate[l], ffn1_w_up[l], ffn1_w_down[l], mix_norm[l],
                        w_in[l], cmp_pe_k[l], cmp_pe_v[l], cmp_k_w1[l], cmp_k_w2[l], cmp_v_w1[l],
                        cmp_v_w2[l], conv_w[l], w_nsa_out[l], w_conv_out[l], w_out[l], ffn2_norm[l],
                        ffn2_w_gate[l], ffn2_w_up[l], ffn2_w_down[l], final_norm, l == depth - 1)
        outs.append(xb)
    return outs[0][None] if batch == 1 else jnp.stack(outs)
```

```python
import functools

import jax
import jax.numpy as jnp
import numpy as np
from jax import lax
from jax.experimental import pallas as pl
from jax.experimental.pallas import tpu as pltpu

D_MODEL = 1024
N_HEADS = 16
HEAD_DIM = 64
N_GROUPS = 2
HEADS_PER_GROUP = N_HEADS // N_GROUPS
N_PAIRS = N_HEADS // 2
PAIRS_PER_GROUP = N_PAIRS // N_GROUPS
CMP_BLOCK = 32
CMP_STRIDE = 16
CMP_HIDDEN = 256
SLC_BLOCK = 64
N_SELECT = 16
WINDOW = 512
CONV_WIDTH = 1024
CONV_K = 3
D_FF = 2816
EPS = 1e-6
NEG = -(2.0 ** 100)
N_FORCED = 3
LOG2E = float(np.log2(np.e))
LANES = 128
BF16_ROWS = 16

P_Q, P_CB, P_CC, P_CX, P_GA, P_GB = 0, 8, 16, 24, 32, 40
P_KC, P_VC, P_KS, P_VS, P_KW, P_VW, P_G3 = 48, 49, 50, 51, 52, 53, 54
P_BLOCKS = 56
P_COLS = P_BLOCKS * LANES
SEG_BLOCKS = D_MODEL // LANES

F32 = jnp.float32
BF16 = jnp.bfloat16

VMEM_LIMIT = 52 * 1024 * 1024


def _dot(a, b):
    return jnp.dot(a, b, preferred_element_type=F32)


def _dot_nt(a, b):
    return lax.dot_general(a, b, (((1,), (1,)), ((), ())), preferred_element_type=F32)


def _sigmoid(x):
    return 1.0 / (1.0 + jnp.exp(-x))


def _rms(x, g):
    return x * lax.rsqrt(jnp.mean(x * x, axis=-1, keepdims=True) + EPS) * g


def _params(sem, limit=VMEM_LIMIT, flags=None):
    return pltpu.CompilerParams(dimension_semantics=sem, vmem_limit_bytes=limit, flags=flags)


FFN_TM = 512
FFN_CHUNK = 256


def _ffn_kernel(x_ref, g_ref, wg_ref, wu_ref, wd_ref, fg_ref, o_ref, *, final_norm):
    x = x_ref[...]
    h = _rms(x, g_ref[...]).astype(BF16)
    acc = None
    for c in range(D_FF // FFN_CHUNK):
        cols = slice(c * FFN_CHUNK, (c + 1) * FFN_CHUNK)
        gate = _dot(h, wg_ref[:, cols])
        up = _dot(h, wu_ref[:, cols])
        a = (gate * _sigmoid(gate) * up).astype(BF16)
        part = _dot(a, wd_ref[cols, :])
        acc = part if acc is None else acc + part
    y = x + 0.5 * acc
    if final_norm:
        y = _rms(y, fg_ref[...])
    o_ref[...] = y


def _ffn(x, norm_g, wg, wu, wd, final_g, final_norm):
    s = x.shape[0]
    resident = lambda shape: pl.BlockSpec(shape, lambda i: (0, 0), pipeline_mode=pl.Buffered(1))
    return pl.pallas_call(
        functools.partial(_ffn_kernel, final_norm=final_norm),
        grid=(s // FFN_TM,),
        in_specs=[
            pl.BlockSpec((FFN_TM, D_MODEL), lambda i: (i, 0)),
            resident((1, D_MODEL)),
            resident((D_MODEL, D_FF)),
            resident((D_MODEL, D_FF)),
            resident((D_FF, D_MODEL)),
            resident((1, D_MODEL)),
        ],
        out_specs=pl.BlockSpec((FFN_TM, D_MODEL), lambda i: (i, 0)),
        out_shape=jax.ShapeDtypeStruct((s, D_MODEL), F32),
        compiler_params=_params(("parallel",)),
        name="ffn_final" if final_norm else "ffn",
    )(x, norm_g, wg, wu, wd, final_g)


PROJ_TM = 512
PROJ_CHUNK = 512


P_TRANSPOSED = (P_VS, P_VW, P_G3)


def _proj_kernel(x_ref, g_ref, w_ref, o_ref, *t_refs):
    h = _rms(x_ref[...], g_ref[...]).astype(BF16)
    chunk_blocks = PROJ_CHUNK // LANES
    for c in range(P_COLS // PROJ_CHUNK):
        cols = slice(c * PROJ_CHUNK, (c + 1) * PROJ_CHUNK)
        y = _dot(h, w_ref[:, cols])
        o_ref[:, cols] = y.astype(BF16)
        for t_ref, blk in zip(t_refs, P_TRANSPOSED):
            if blk // chunk_blocks == c:
                lo = (blk % chunk_blocks) * LANES
                t_ref[...] = y[:, lo:lo + LANES].T.astype(BF16)


def _proj(x, norm_g, w):
    s = x.shape[0]
    resident = lambda shape: pl.BlockSpec(shape, lambda i: (0, 0), pipeline_mode=pl.Buffered(1))
    return pl.pallas_call(
        _proj_kernel,
        grid=(s // PROJ_TM,),
        in_specs=[
            pl.BlockSpec((PROJ_TM, D_MODEL), lambda i: (i, 0)),
            resident((1, D_MODEL)),
            resident((D_MODEL, P_COLS)),
        ],
        out_specs=[pl.BlockSpec((PROJ_TM, P_COLS), lambda i: (i, 0))]
        + [pl.BlockSpec((LANES, PROJ_TM), lambda i: (0, i)) for _ in P_TRANSPOSED],
        out_shape=[jax.ShapeDtypeStruct((s, P_COLS), BF16)]
        + [jax.ShapeDtypeStruct((LANES, s), BF16) for _ in P_TRANSPOSED],
        compiler_params=_params(("parallel",)),
        name="proj",
    )(x, norm_g, w)


def _proj_weight(w_in):
    q, kc, vc, ks, vs, kw, vw, g3, cb, cc, cx, ga, gb = jnp.split(
        w_in.T, np.cumsum([1024, 128, 128, 128, 128, 128, 128, 48, 1024, 1024, 1024, 1024]).tolist(), axis=0)
    g3 = jnp.pad(g3, ((0, (P_BLOCKS - P_G3) * LANES - g3.shape[0]), (0, 0)))
    w_t = jnp.concatenate([q * (HEAD_DIM ** -0.5 * LOG2E), cb, cc, cx, ga, gb, kc, vc, ks, vs, kw, vw, g3], axis=0)
    return w_t.astype(BF16).T


def _gelu_tanh(x):
    return 0.5 * x * (1.0 + jnp.tanh(np.sqrt(2.0 / np.pi).astype(np.float32) * (x + 0.044715 * (x * x * x))))


def _compress_kernel(cf_ref, pe_ref, w1_ref, w2_ref, o_ref, *, transposed):
    cf = cf_ref[...].astype(F32)
    top = (cf + pe_ref[0]).astype(BF16)
    bot = (cf + pe_ref[1]).astype(BF16)
    a = _dot(top, w1_ref[0, 0])
    b = _dot(bot, w1_ref[0, 1])
    n = a.shape[0]
    pre = a + pltpu.roll(b, n - 1, 0)
    h = _gelu_tanh(pre).astype(BF16)
    if transposed:
        o_ref[0] = _dot_nt(w2_ref[...], h).astype(BF16)
    else:
        o_ref[0, 0] = _dot(h, w2_ref[0]).astype(BF16)
        o_ref[0, 1] = _dot(h, w2_ref[1]).astype(BF16)


def _compress(cf, pe, w1, w2, transposed):
    n = cf.shape[0]
    width = CMP_STRIDE * LANES
    if transposed:
        w2_spec = pl.BlockSpec((HEAD_DIM, CMP_HIDDEN), lambda g: (0, 0))
        out_spec = pl.BlockSpec((1, HEAD_DIM, n), lambda g: (g, 0, 0))
        out_shape = jax.ShapeDtypeStruct((N_GROUPS, HEAD_DIM, n), BF16)
    else:
        w2_spec = pl.BlockSpec((2, CMP_HIDDEN, LANES), lambda g: (0, 0, 0))
        out_spec = pl.BlockSpec((1, 2, n, LANES), lambda g: (g, 0, 0, 0))
        out_shape = jax.ShapeDtypeStruct((N_GROUPS, 2, n, LANES), BF16)
    return pl.pallas_call(
        functools.partial(_compress_kernel, transposed=transposed),
        grid=(N_GROUPS,),
        in_specs=[
            pl.BlockSpec((n, width), lambda g: (0, 0)),
            pl.BlockSpec((2, 1, width), lambda g: (0, 0, 0)),
            pl.BlockSpec((1, 2, width, CMP_HIDDEN), lambda g: (g, 0, 0, 0)),
            w2_spec,
        ],
        out_specs=out_spec,
        out_shape=out_shape,
        compiler_params=_params(("parallel",)),
        name="compress_v" if transposed else "compress_k",
    )(cf, pe, w1, w2)


def _compress_weights(pe, w1, w2, transposed):
    w1h = w1.reshape(2, CMP_STRIDE, HEAD_DIM, CMP_HIDDEN)
    zeros = jnp.zeros_like(w1h)
    w1g = jnp.stack([jnp.concatenate([w1h, zeros], axis=2), jnp.concatenate([zeros, w1h], axis=2)])
    w1g = w1g.reshape(N_GROUPS, 2, CMP_STRIDE * LANES, CMP_HIDDEN).astype(BF16)
    peh = pe.reshape(2, CMP_STRIDE, HEAD_DIM)
    pe2 = jnp.concatenate([peh, peh], axis=2).reshape(2, 1, CMP_STRIDE * LANES)
    if transposed:
        w2p = w2.T.astype(BF16)
    else:
        z2 = jnp.zeros_like(w2)
        w2p = jnp.stack([jnp.concatenate([w2, z2], axis=1), jnp.concatenate([z2, w2], axis=1)]).astype(BF16)
    return pe2, w1g, w2p


def _split_heads(kt):
    lo = lax.broadcasted_iota(jnp.int32, kt.shape, 1) < HEAD_DIM
    kr = pltpu.roll(kt, HEAD_DIM, 1)
    zero = jnp.zeros_like(kt)
    return [[jnp.where(lo, kt, zero), jnp.where(lo, zero, kr)],
            [jnp.where(lo, kr, zero), jnp.where(lo, zero, kt)]]


QSUB = 256
CMP_TQ = 256
CMP_KEY_CHUNK = 256
CMP_LOOKAHEAD = 4


def _cmp_kernel(q_ref, kc_ref, vct_ref, ovt_ref, g3t_ref, o_ref, sel_ref, *, n_slc):
    tq = q_ref.shape[0]
    n_cmp = kc_ref.shape[2]
    q0 = pl.program_id(0) * tq
    t = q0 + lax.broadcasted_iota(jnp.int32, (1, tq), 1)
    any_valid = t >= CMP_BLOCK - 1
    cur = t // SLC_BLOCK

    def body(n_keys, n_blk):
        c_end = lax.broadcasted_iota(jnp.int32, (n_keys, 1), 0) * CMP_STRIDE + (CMP_BLOCK - 1)
        bias = jnp.where(c_end <= t, 0.0, NEG)
        gates = _sigmoid(g3t_ref[...].astype(F32))

        ovt = ovt_ref[:n_blk, :n_keys]
        ones = jnp.ones((BF16_ROWS, n_keys), BF16)
        lhs = [jnp.concatenate([vct_ref[g, :, :n_keys], ones, ovt], axis=0) for g in range(N_GROUPS)]
        imp_row = HEAD_DIM + BF16_ROWS
        blk = lax.broadcasted_iota(jnp.int32, (n_blk, QSUB), 0)
        blk_f = blk.astype(F32)

        def select(cols, importance):
            forced = (blk == 0) | (blk == cur[:, cols]) | (blk == cur[:, cols] - 1)
            causal = blk * SLC_BLOCK <= t[:, cols]
            for g in range(N_GROUPS):
                work = jnp.where(causal & ~forced, importance[g], -1.0)
                sel = jnp.where(forced, 1.0, 0.0)
                for _ in range(N_SELECT - N_FORCED):
                    m = jnp.max(work, axis=0, keepdims=True)
                    first = jnp.min(jnp.where(work == m, blk_f, float(n_blk)), axis=0, keepdims=True)
                    hit = blk_f == first
                    sel = jnp.where(hit, 1.0, sel)
                    work = jnp.where(hit, -jnp.inf, work)
                sel_ref[g * n_slc:g * n_slc + n_blk, cols] = jnp.where(sel > 0.0, 0.0, NEG)
                if n_blk < n_slc:
                    sel_ref[g * n_slc + n_blk:(g + 1) * n_slc, cols] = jnp.full((n_slc - n_blk, QSUB), NEG, F32)

        units = [(h, slice(qs * QSUB, (qs + 1) * QSUB)) for qs in range(tq // QSUB) for h in range(N_HEADS)]

        def scores(unit):
            h, cols = unit
            p, e = divmod(h, 2)
            g = p // PAIRS_PER_GROUP
            s = _dot_nt(kc_ref[g, e, :n_keys, :], q_ref[cols, p * LANES:(p + 1) * LANES]) + bias[:, cols]
            return s, jnp.max(s, axis=0, keepdims=True)

        importance = [None] * N_GROUPS
        pending = [scores(u) for u in units[:CMP_LOOKAHEAD]]
        for n, (h, cols) in enumerate(units):
            g = h // HEADS_PER_GROUP
            if n + CMP_LOOKAHEAD < len(units):
                pending.append(scores(units[n + CMP_LOOKAHEAD]))
            s, m = pending.pop(0)
            pr = jnp.exp2(s - m).astype(BF16)
            res = _dot(lhs[g], pr)
            inv_l = jnp.where(any_valid[:, cols], 1.0 / res[HEAD_DIM:HEAD_DIM + 1, :], 0.0)
            o_h = res[:HEAD_DIM, :] * (inv_l * gates[3 * h:3 * h + 1, cols])
            o_ref[h * HEAD_DIM:(h + 1) * HEAD_DIM, cols] = o_h.astype(BF16)
            imp_h = res[imp_row:, :] * inv_l
            importance[g] = imp_h if importance[g] is None else importance[g] + imp_h
            if h == N_HEADS - 1:
                select(cols, importance)
                importance = [None] * N_GROUPS

    chunk_tokens = CMP_KEY_CHUNK * CMP_STRIDE
    variant = (q0 + tq - 1) // chunk_tokens
    for v in range(n_cmp // CMP_KEY_CHUNK):
        pl.when(variant == v)(functools.partial(body, (v + 1) * CMP_KEY_CHUNK,
                                                (v + 1) * chunk_tokens // SLC_BLOCK))


def _cmp_attention(p_arr, kc, vct, overlap_t, g3t):
    s = p_arr.shape[0]
    n_cmp = kc.shape[2]
    n_slc = s // SLC_BLOCK
    return pl.pallas_call(
        functools.partial(_cmp_kernel, n_slc=n_slc),
        grid=(s // CMP_TQ,),
        in_specs=[
            pl.BlockSpec((CMP_TQ, D_MODEL), lambda i: (i, P_Q // SEG_BLOCKS)),
            pl.BlockSpec((N_GROUPS, 2, n_cmp, LANES), lambda i: (0, 0, 0, 0)),
            pl.BlockSpec((N_GROUPS, HEAD_DIM, n_cmp), lambda i: (0, 0, 0)),
            pl.BlockSpec((n_slc, n_cmp), lambda i: (0, 0)),
            pl.BlockSpec((LANES, CMP_TQ), lambda i: (0, i)),
        ],
        out_specs=[
            pl.BlockSpec((D_MODEL, CMP_TQ), lambda i: (0, i)),
            pl.BlockSpec((N_GROUPS * n_slc, CMP_TQ), lambda i: (0, i)),
        ],
        out_shape=[
            jax.ShapeDtypeStruct((D_MODEL, s), BF16),
            jax.ShapeDtypeStruct((N_GROUPS * n_slc, s), F32),
        ],
        compiler_params=_params(("parallel",)),
        name="cmp_attention",
    )(p_arr, kc, vct, overlap_t, g3t)


def _overlap_matrix_t(n_cmp, n_slc):
    c0 = np.arange(n_cmp)[None, :] * CMP_STRIDE
    s0 = np.arange(n_slc)[:, None] * SLC_BLOCK
    ov = (c0 < s0 + SLC_BLOCK) & (c0 + CMP_BLOCK > s0)
    return jnp.asarray(ov, dtype=BF16)


SLC_TQ = 1024
SLC_TK = 512
SLC_LOOKAHEAD = 4
ACC_ROWS = HEAD_DIM + BF16_ROWS


def _selected_kernel(qi_ref, ki_ref, fl_ref, q_ref, k_ref, vt_ref, g3t_ref, sel_ref, o_ref, m_ref, acc_ref, *,
                     n_slc):
    step = pl.program_id(0)
    tq = q_ref.shape[0]
    tk = k_ref.shape[0]
    flags = fl_ref[step]

    @pl.when((flags & 1) != 0)
    def _():
        m_ref[...] = jnp.full_like(m_ref, NEG)
        acc_ref[...] = jnp.zeros_like(acc_ref)

    def attend(first_sub):
        c0 = first_sub * QSUB
        t = qi_ref[step] * tq + c0 + lax.broadcasted_iota(jnp.int32, (1, tq - c0), 1)
        key = ki_ref[step] * tk + lax.broadcasted_iota(jnp.int32, (tk, 1), 0)

        tile_blocks = tk // SLC_BLOCK
        causal_bias = jnp.where(key <= t, 0.0, NEG)
        bias = []
        for g in range(N_GROUPS):
            first_block = pl.multiple_of(g * n_slc + ki_ref[step] * tile_blocks, tile_blocks)
            rows = sel_ref[pl.ds(first_block, tile_blocks), c0:]
            per_key = jnp.broadcast_to(rows[:, None, :], (tile_blocks, SLC_BLOCK, tq - c0)).reshape(tk, tq - c0)
            bias.append(jnp.minimum(per_key, causal_bias))

        k_half = _split_heads(k_ref[...])
        vt = vt_ref[...]
        ones = jnp.ones((ACC_ROWS - HEAD_DIM, tk), BF16)
        v_aug = [jnp.concatenate([vt[g * HEAD_DIM:(g + 1) * HEAD_DIM, :], ones], axis=0) for g in range(N_GROUPS)]

        units = [(h, qs) for qs in range(first_sub, tq // QSUB) for h in range(N_HEADS)]

        def scores(unit):
            h, qs = unit
            p, e = divmod(h, 2)
            g = p // PAIRS_PER_GROUP
            q = q_ref[qs * QSUB:(qs + 1) * QSUB, p * LANES:(p + 1) * LANES]
            s = (_dot_nt(k_half[g][e], q) + bias[g][:, qs * QSUB - c0:(qs + 1) * QSUB - c0]).astype(BF16)
            return s, jnp.max(s, axis=0, keepdims=True)

        pending = [scores(u) for u in units[:SLC_LOOKAHEAD]]
        for n, (h, qs) in enumerate(units):
            g = h // HEADS_PER_GROUP
            cols = slice(qs * QSUB, (qs + 1) * QSUB)
            if n + SLC_LOOKAHEAD < len(units):
                pending.append(scores(units[n + SLC_LOOKAHEAD]))
            s, s_max = pending.pop(0)
            m_prev = m_ref[h:h + 1, cols]
            m_new = jnp.maximum(m_prev, s_max.astype(F32))
            alpha = jnp.exp2(m_prev - m_new)
            pr = jnp.exp2(s - m_new.astype(BF16))
            m_ref[h:h + 1, cols] = m_new
            acc_ref[h, :, cols] = acc_ref[h, :, cols] * alpha + _dot(v_aug[g], pr)

    is_last = (flags & 2) != 0

    @pl.when(jnp.logical_not(is_last))
    def _():
        attend(0)

    @pl.when(is_last)
    def _():
        attend((tq - tk) // QSUB)
        gates = _sigmoid(g3t_ref[...].astype(F32))
        for h in range(N_HEADS):
            acc = acc_ref[h]
            scale = gates[3 * h + 1:3 * h + 2, :] / acc[HEAD_DIM:HEAD_DIM + 1, :]
            o_ref[h * HEAD_DIM:(h + 1) * HEAD_DIM, :] = (acc[:HEAD_DIM, :] * scale).astype(BF16)


def _causal_tile_tables(s, tq, tk):
    qi, ki, fl = [], [], []
    for i in range(s // tq):
        last = (i * tq + tq - 1) // tk
        for k in range(last + 1):
            qi.append(i)
            ki.append(k)
            fl.append((1 if k == 0 else 0) | (2 if k == last else 0))
    return (jnp.asarray(qi, jnp.int32), jnp.asarray(ki, jnp.int32), jnp.asarray(fl, jnp.int32))


def _selected_attention(p_arr, vt, g3t, sel_t):
    s = p_arr.shape[0]
    n_slc = s // SLC_BLOCK
    tq, tk = SLC_TQ, SLC_TK
    assert tq % tk == 0 and tk % QSUB == 0, "the kernel skips query slices that precede the last key tile"
    qi, ki, fl = _causal_tile_tables(s, tq, tk)
    grid_spec = pltpu.PrefetchScalarGridSpec(
        num_scalar_prefetch=3,
        grid=(qi.shape[0],),
        in_specs=[
            pl.BlockSpec((tq, D_MODEL), lambda n, qi, ki, fl: (qi[n], P_Q // SEG_BLOCKS)),
            pl.BlockSpec((tk, LANES), lambda n, qi, ki, fl: (ki[n], P_KS)),
            pl.BlockSpec((LANES, tk), lambda n, qi, ki, fl: (0, ki[n])),
            pl.BlockSpec((LANES, tq), lambda n, qi, ki, fl: (0, qi[n])),
            pl.BlockSpec((N_GROUPS * n_slc, tq), lambda n, qi, ki, fl: (0, qi[n])),
        ],
        out_specs=pl.BlockSpec((D_MODEL, tq), lambda n, qi, ki, fl: (0, qi[n])),
        scratch_shapes=[
            pltpu.VMEM((N_HEADS, tq), F32),
            pltpu.VMEM((N_HEADS, ACC_ROWS, tq), F32),
        ],
    )
    return pl.pallas_call(
        functools.partial(_selected_kernel, n_slc=n_slc),
        grid_spec=grid_spec,
        out_shape=jax.ShapeDtypeStruct((D_MODEL, s), BF16),
        compiler_params=_params(("arbitrary",)),
        name="selected_attention",
    )(qi, ki, fl, p_arr, p_arr, vt, g3t, sel_t)


WINDOW_TQ = 1024
WINDOW_BLOCK = 256
WINDOW_LOOKAHEAD = 4


def _window_kernel(q_ref, *refs, n_blocks):
    k_refs, vt_refs = refs[:n_blocks], refs[n_blocks:2 * n_blocks]
    g3t_ref, o_ref = refs[2 * n_blocks:]
    tq = q_ref.shape[0]
    span = WINDOW + QSUB
    k_half = _split_heads(jnp.concatenate([r[...] for r in k_refs], axis=0))
    vt = jnp.concatenate([r[...] for r in vt_refs], axis=1)
    ones = jnp.ones((BF16_ROWS, vt.shape[1]), BF16)
    v_aug = [jnp.concatenate([vt[g * HEAD_DIM:(g + 1) * HEAD_DIM, :], ones], axis=0) for g in range(N_GROUPS)]
    gates = _sigmoid(g3t_ref[...].astype(F32))

    r = lax.broadcasted_iota(jnp.int32, (span, 1), 0)
    c = lax.broadcasted_iota(jnp.int32, (1, QSUB), 1)
    in_band = (r > c) & (r <= c + WINDOW)
    first_key = pl.program_id(0) * tq - WINDOW
    n_sub = tq // QSUB
    bias = [jnp.where(in_band & (first_key + qs * QSUB + r >= 0), 0.0, NEG) for qs in range(n_sub)]

    units = [(h, qs) for qs in range(n_sub) for h in range(N_HEADS)]

    def scores(unit):
        h, qs = unit
        p, e = divmod(h, 2)
        g = p // PAIRS_PER_GROUP
        q = q_ref[qs * QSUB:(qs + 1) * QSUB, p * LANES:(p + 1) * LANES]
        s = (_dot_nt(k_half[g][e][qs * QSUB:qs * QSUB + span, :], q) + bias[qs]).astype(BF16)
        return s, jnp.max(s, axis=0, keepdims=True)

    pending = [scores(u) for u in units[:WINDOW_LOOKAHEAD]]
    for n, (h, qs) in enumerate(units):
        g = h // HEADS_PER_GROUP
        cols = slice(qs * QSUB, (qs + 1) * QSUB)
        if n + WINDOW_LOOKAHEAD < len(units):
            pending.append(scores(units[n + WINDOW_LOOKAHEAD]))
        s, m = pending.pop(0)
        res = _dot(v_aug[g][:, qs * QSUB:qs * QSUB + span], jnp.exp2(s - m))
        scale = gates[3 * h + 2:3 * h + 3, cols] / res[HEAD_DIM:HEAD_DIM + 1, :]
        o_ref[h * HEAD_DIM:(h + 1) * HEAD_DIM, cols] = (res[:HEAD_DIM, :] * scale).astype(BF16)


def _window_attention(p_arr, vt, g3t):
    s = p_arr.shape[0]
    tq = WINDOW_TQ
    n_blocks = (WINDOW + tq) // WINDOW_BLOCK
    back = WINDOW // WINDOW_BLOCK
    per_tile = tq // WINDOW_BLOCK

    def block(j):
        return lambda i: jnp.maximum(i * per_tile - back + j, 0)

    k_specs = [pl.BlockSpec((WINDOW_BLOCK, LANES), lambda i, b=block(j): (b(i), P_KW)) for j in range(n_blocks)]
    v_specs = [pl.BlockSpec((LANES, WINDOW_BLOCK), lambda i, b=block(j): (0, b(i))) for j in range(n_blocks)]
    return pl.pallas_call(
        functools.partial(_window_kernel, n_blocks=n_blocks),
        grid=(s // tq,),
        in_specs=[pl.BlockSpec((tq, D_MODEL), lambda i: (i, P_Q // SEG_BLOCKS))] + k_specs + v_specs
        + [pl.BlockSpec((LANES, tq), lambda i: (0, i))],
        out_specs=pl.BlockSpec((D_MODEL, tq), lambda i: (0, i)),
        out_shape=jax.ShapeDtypeStruct((D_MODEL, s), BF16),
        compiler_params=_params(("parallel",)),
        name="window_attention",
    )(p_arr, *([p_arr] * n_blocks), *([vt] * n_blocks), g3t)


OUT_TM = 512
PREV_ROWS = BF16_ROWS


def _out_kernel(x_ref, oc_ref, os_ref, ow_ref, cb_ref, cc_ref, cx_ref, pc_ref, px_ref, ga_ref, gb_ref,
                cw_ref, wa_ref, wb_ref, wo_ref, o_ref, u_ref):
    tm = x_ref.shape[0]
    prev = pc_ref[...].astype(F32) * px_ref[...].astype(F32)
    u_ref[0:PREV_ROWS, :] = jnp.where(pl.program_id(0) == 0, 0.0, prev)
    u_ref[PREV_ROWS:, :] = cc_ref[...].astype(F32) * cx_ref[...].astype(F32)
    cw = cw_ref[...]
    conv = (cw[0:1, :] * u_ref[pl.ds(PREV_ROWS - 2, tm), :]
            + cw[1:2, :] * u_ref[pl.ds(PREV_ROWS - 1, tm), :]
            + cw[2:3, :] * u_ref[pl.ds(PREV_ROWS, tm), :])
    y_b = _dot((cb_ref[...].astype(F32) * conv).astype(BF16), wb_ref[...])

    o_nsa_t = oc_ref[...].astype(F32) + os_ref[...].astype(F32) + ow_ref[...].astype(F32)
    y_a = _dot(o_nsa_t.T.astype(BF16), wa_ref[...])

    merged = _sigmoid(ga_ref[...].astype(F32)) * y_a + _sigmoid(gb_ref[...].astype(F32)) * y_b
    o_ref[...] = x_ref[...] + _dot(merged.astype(BF16), wo_ref[...])


def _out(x1, o_cmp_t, o_slc_t, o_win_t, p_arr, conv_w, w_nsa_out, w_conv_out, w_out):
    s = x1.shape[0]
    tm = OUT_TM
    row = lambda i: (i, 0)
    col = lambda i: (0, i)
    prev_blocks = tm // PREV_ROWS

    def seg(c):
        return pl.BlockSpec((tm, D_MODEL), lambda i: (i, c // SEG_BLOCKS))

    def seg_prev(c):
        return pl.BlockSpec((PREV_ROWS, D_MODEL), lambda i: (jnp.maximum(i * prev_blocks - 1, 0), c // SEG_BLOCKS))

    full = lambda shape: pl.BlockSpec(shape, lambda i: (0, 0))
    return pl.pallas_call(
        _out_kernel,
        grid=(s // tm,),
        in_specs=[
            pl.BlockSpec((tm, D_MODEL), row), pl.BlockSpec((D_MODEL, tm), col),
            pl.BlockSpec((D_MODEL, tm), col), pl.BlockSpec((D_MODEL, tm), col),
            seg(P_CB), seg(P_CC), seg(P_CX), seg_prev(P_CC), seg_prev(P_CX), seg(P_GA), seg(P_GB),
            full((CONV_K, CONV_WIDTH)), full((D_MODEL, D_MODEL)), full((CONV_WIDTH, D_MODEL)),
            full((D_MODEL, D_MODEL)),
        ],
        out_specs=pl.BlockSpec((tm, D_MODEL), row),
        out_shape=jax.ShapeDtypeStruct((s, D_MODEL), F32),
        scratch_shapes=[pltpu.VMEM((tm + PREV_ROWS, CONV_WIDTH), F32)],
        compiler_params=_params(("parallel",)),
        name="merge_out",
    )(x1, o_cmp_t, o_slc_t, o_win_t, p_arr, p_arr, p_arr, p_arr, p_arr, p_arr, p_arr,
      conv_w, w_nsa_out, w_conv_out, w_out)


def _layer(x, ffn1_norm, ffn1_w_gate, ffn1_w_up, ffn1_w_down, mix_norm, w_in, cmp_pe_k, cmp_pe_v,
           cmp_k_w1, cmp_k_w2, cmp_v_w1, cmp_v_w2, conv_w, w_nsa_out, w_conv_out, w_out,
           ffn2_norm, ffn2_w_gate, ffn2_w_up, ffn2_w_down, final_g, final_norm):
    s = x.shape[0]
    n_cmp = s // CMP_STRIDE
    n_slc = s // SLC_BLOCK
    row = lambda v: v.reshape(1, -1).astype(F32)

    x1 = _ffn(x, row(ffn1_norm), ffn1_w_gate.astype(BF16), ffn1_w_up.astype(BF16),
              ffn1_w_down.astype(BF16), row(final_g), False)
    p_arr, vst, vwt, g3t = _proj(x1, row(mix_norm), _proj_weight(w_in))

    col = lambda c: p_arr[:, c * LANES:(c + 1) * LANES]
    chunks = lambda c: col(c).reshape(n_cmp, CMP_STRIDE * LANES)
    kc = _compress(chunks(P_KC), *_compress_weights(cmp_pe_k, cmp_k_w1, cmp_k_w2, False), False)
    vct = _compress(chunks(P_VC), *_compress_weights(cmp_pe_v, cmp_v_w1, cmp_v_w2, True), True)

    o_cmp_t, sel_t = _cmp_attention(p_arr, kc, vct, _overlap_matrix_t(n_cmp, n_slc), g3t)
    o_slc_t = _selected_attention(p_arr, vst, g3t, sel_t)
    o_win_t = _window_attention(p_arr, vwt, g3t)
    x2 = _out(x1, o_cmp_t, o_slc_t, o_win_t, p_arr, conv_w.astype(F32), w_nsa_out.astype(BF16),
              w_conv_out.astype(BF16), w_out.astype(BF16))
    return _ffn(x2, row(ffn2_norm), ffn2_w_gate.astype(BF16), ffn2_w_up.astype(BF16),
                ffn2_w_down.astype(BF16), row(final_g), final_norm)


def kernel(x, ffn1_norm, ffn1_w_gate, ffn1_w_up, ffn1_w_down, mix_norm, w_in, cmp_pe_k, cmp_pe_v,
           cmp_k_w1, cmp_k_w2, cmp_v_w1, cmp_v_w2, conv_w, w_nsa_out, w_conv_out, w_out, ffn2_norm,
           ffn2_w_gate, ffn2_w_up, ffn2_w_down, final_norm):
    batch, _, _ = x.shape
    depth = ffn1_norm.shape[0]
    outs = []
    for b in range(batch):
        xb = x[b]
        for l in range(depth):
            xb = _layer(xb, ffn1_norm[l], ffn1_w_gate[l], ffn1_w_up[l], ffn1_w_down[l], mix_norm[l],
                        w_in[l], cmp_pe_k[l], cmp_pe_v[l], cmp_k_w1[l], cmp_k_w2[l], cmp_v_w1[l],
                        cmp_v_w2[l], conv_w[l], w_nsa_out[l], w_conv_out[l], w_out[l], ffn2_norm[l],
                        ffn2_w_gate[l], ffn2_w_up[l], ffn2_w_down[l], final_norm, l == depth - 1)
        outs.append(xb)
    return outs[0][None] if batch == 1 else jnp.stack(outs)
```

```python
import functools

import jax
import jax.numpy as jnp
import numpy as np
from jax import lax
from jax.experimental import pallas as pl
from jax.experimental.pallas import tpu as pltpu

D_MODEL = 1024
N_HEADS = 16
HEAD_DIM = 64
N_GROUPS = 2
HEADS_PER_GROUP = N_HEADS // N_GROUPS
N_PAIRS = N_HEADS // 2
PAIRS_PER_GROUP = N_PAIRS // N_GROUPS
CMP_BLOCK = 32
CMP_STRIDE = 16
CMP_HIDDEN = 256
SLC_BLOCK = 64
N_SELECT = 16
WINDOW = 512
CONV_WIDTH = 1024
CONV_K = 3
D_FF = 2816
EPS = 1e-6
NEG = -(2.0 ** 100)
N_FORCED = 3
LOG2E = float(np.log2(np.e))
LANES = 128
BF16_ROWS = 16

P_Q, P_CB, P_CC, P_CX, P_GA, P_GB = 0, 8, 16, 24, 32, 40
P_KC, P_VC, P_KS, P_VS, P_KW, P_VW, P_G3 = 48, 49, 50, 51, 52, 53, 54
P_BLOCKS = 56
P_COLS = P_BLOCKS * LANES
SEG_BLOCKS = D_MODEL // LANES

F32 = jnp.float32
BF16 = jnp.bfloat16

VMEM_LIMIT = 52 * 1024 * 1024


def _dot(a, b):
    return jnp.dot(a, b, preferred_element_type=F32)


def _dot_nt(a, b):
    return lax.dot_general(a, b, (((1,), (1,)), ((), ())), preferred_element_type=F32)


def _sigmoid(x):
    return 1.0 / (1.0 + jnp.exp(-x))


def _rms(x, g):
    return x * lax.rsqrt(jnp.mean(x * x, axis=-1, keepdims=True) + EPS) * g


def _params(sem, limit=VMEM_LIMIT, flags=None):
    return pltpu.CompilerParams(dimension_semantics=sem, vmem_limit_bytes=limit, flags=flags)


FFN_TM = 512
FFN_CHUNK = 256


def _ffn_kernel(x_ref, g_ref, wg_ref, wu_ref, wd_ref, fg_ref, o_ref, *, final_norm):
    x = x_ref[...]
    h = _rms(x, g_ref[...]).astype(BF16)
    acc = None
    for c in range(D_FF // FFN_CHUNK):
        cols = slice(c * FFN_CHUNK, (c + 1) * FFN_CHUNK)
        gate = _dot(h, wg_ref[:, cols])
        up = _dot(h, wu_ref[:, cols])
        a = (gate * _sigmoid(gate) * up).astype(BF16)
        part = _dot(a, wd_ref[cols, :])
        acc = part if acc is None else acc + part
    y = x + 0.5 * acc
    if final_norm:
        y = _rms(y, fg_ref[...])
    o_ref[...] = y


def _ffn(x, norm_g, wg, wu, wd, final_g, final_norm):
    s = x.shape[0]
    resident = lambda shape: pl.BlockSpec(shape, lambda i: (0, 0), pipeline_mode=pl.Buffered(1))
    return pl.pallas_call(
        functools.partial(_ffn_kernel, final_norm=final_norm),
        grid=(s // FFN_TM,),
        in_specs=[
            pl.BlockSpec((FFN_TM, D_MODEL), lambda i: (i, 0)),
            resident((1, D_MODEL)),
            resident((D_MODEL, D_FF)),
            resident((D_MODEL, D_FF)),
            resident((D_FF, D_MODEL)),
            resident((1, D_MODEL)),
        ],
        out_specs=pl.BlockSpec((FFN_TM, D_MODEL), lambda i: (i, 0)),
        out_shape=jax.ShapeDtypeStruct((s, D_MODEL), F32),
        compiler_params=_params(("parallel",)),
        name="ffn_final" if final_norm else "ffn",
    )(x, norm_g, wg, wu, wd, final_g)


PROJ_TM = 512
PROJ_CHUNK = 512


P_TRANSPOSED = (P_VS, P_VW, P_G3)


def _proj_kernel(x_ref, g_ref, w_ref, o_ref, *t_refs):
    h = _rms(x_ref[...], g_ref[...]).astype(BF16)
    chunk_blocks = PROJ_CHUNK // LANES
    for c in range(P_COLS // PROJ_CHUNK):
        cols = slice(c * PROJ_CHUNK, (c + 1) * PROJ_CHUNK)
        y = _dot(h, w_ref[:, cols])
        o_ref[:, cols] = y.astype(BF16)
        for t_ref, blk in zip(t_refs, P_TRANSPOSED):
            if blk // chunk_blocks == c:
                lo = (blk % chunk_blocks) * LANES
                t_ref[...] = y[:, lo:lo + LANES].T.astype(BF16)


def _proj(x, norm_g, w):
    s = x.shape[0]
    resident = lambda shape: pl.BlockSpec(shape, lambda i: (0, 0), pipeline_mode=pl.Buffered(1))
    return pl.pallas_call(
        _proj_kernel,
        grid=(s // PROJ_TM,),
        in_specs=[
            pl.BlockSpec((PROJ_TM, D_MODEL), lambda i: (i, 0)),
            resident((1, D_MODEL)),
            resident((D_MODEL, P_COLS)),
        ],
        out_specs=[pl.BlockSpec((PROJ_TM, P_COLS), lambda i: (i, 0))]
        + [pl.BlockSpec((LANES, PROJ_TM), lambda i: (0, i)) for _ in P_TRANSPOSED],
        out_shape=[jax.ShapeDtypeStruct((s, P_COLS), BF16)]
        + [jax.ShapeDtypeStruct((LANES, s), BF16) for _ in P_TRANSPOSED],
        compiler_params=_params(("parallel",)),
        name="proj",
    )(x, norm_g, w)


def _proj_weight(w_in):
    q, kc, vc, ks, vs, kw, vw, g3, cb, cc, cx, ga, gb = jnp.split(
        w_in.T, np.cumsum([1024, 128, 128, 128, 128, 128, 128, 48, 1024, 1024, 1024, 1024]).tolist(), axis=0)
    g3 = jnp.pad(g3, ((0, (P_BLOCKS - P_G3) * LANES - g3.shape[0]), (0, 0)))
    w_t = jnp.concatenate([q * (HEAD_DIM ** -0.5 * LOG2E), cb, cc, cx, ga, gb, kc, vc, ks, vs, kw, vw, g3], axis=0)
    return w_t.astype(BF16).T


def _gelu_tanh(x):
    return 0.5 * x * (1.0 + jnp.tanh(np.sqrt(2.0 / np.pi).astype(np.float32) * (x + 0.044715 * (x * x * x))))


def _compress_kernel(cf_ref, pe_ref, w1_ref, w2_ref, o_ref, *, transposed):
    cf = cf_ref[...].astype(F32)
    top = (cf + pe_ref[0]).astype(BF16)
    bot = (cf + pe_ref[1]).astype(BF16)
    a = _dot(top, w1_ref[0, 0])
    b = _dot(bot, w1_ref[0, 1])
    n = a.shape[0]
    pre = a + pltpu.roll(b, n - 1, 0)
    h = _gelu_tanh(pre).astype(BF16)
    if transposed:
        o_ref[0] = _dot_nt(w2_ref[...], h).astype(BF16)
    else:
        o_ref[0, 0] = _dot(h, w2_ref[0]).astype(BF16)
        o_ref[0, 1] = _dot(h, w2_ref[1]).astype(BF16)


def _compress(cf, pe, w1, w2, transposed):
    n = cf.shape[0]
    width = CMP_STRIDE * LANES
    if transposed:
        w2_spec = pl.BlockSpec((HEAD_DIM, CMP_HIDDEN), lambda g: (0, 0))
        out_spec = pl.BlockSpec((1, HEAD_DIM, n), lambda g: (g, 0, 0))
        out_shape = jax.ShapeDtypeStruct((N_GROUPS, HEAD_DIM, n), BF16)
    else:
        w2_spec = pl.BlockSpec((2, CMP_HIDDEN, LANES), lambda g: (0, 0, 0))
        out_spec = pl.BlockSpec((1, 2, n, LANES), lambda g: (g, 0, 0, 0))
        out_shape = jax.ShapeDtypeStruct((N_GROUPS, 2, n, LANES), BF16)
    return pl.pallas_call(
        functools.partial(_compress_kernel, transposed=transposed),
        grid=(N_GROUPS,),
        in_specs=[
            pl.BlockSpec((n, width), lambda g: (0, 0)),
            pl.BlockSpec((2, 1, width), lambda g: (0, 0, 0)),
            pl.BlockSpec((1, 2, width, CMP_HIDDEN), lambda g: (g, 0, 0, 0)),
            w2_spec,
        ],
        out_specs=out_spec,
        out_shape=out_shape,
        compiler_params=_params(("parallel",)),
        name="compress_v" if transposed else "compress_k",
    )(cf, pe, w1, w2)


def _compress_weights(pe, w1, w2, transposed):
    w1h = w1.reshape(2, CMP_STRIDE, HEAD_DIM, CMP_HIDDEN)
    zeros = jnp.zeros_like(w1h)
    w1g = jnp.stack([jnp.concatenate([w1h, zeros], axis=2), jnp.concatenate([zeros, w1h], axis=2)])
    w1g = w1g.reshape(N_GROUPS, 2, CMP_STRIDE * LANES, CMP_HIDDEN).astype(BF16)
    peh = pe.reshape(2, CMP_STRIDE, HEAD_DIM)
    pe2 = jnp.concatenate([peh, peh], axis=2).reshape(2, 1, CMP_STRIDE * LANES)
    if transposed:
        w2p = w2.T.astype(BF16)
    else:
        z2 = jnp.zeros_like(w2)
        w2p = jnp.stack([jnp.concatenate([w2, z2], axis=1), jnp.concatenate([z2, w2], axis=1)]).astype(BF16)
    return pe2, w1g, w2p


def _split_heads(kt):
    lo = lax.broadcasted_iota(jnp.int32, kt.shape, 1) < HEAD_DIM
    kr = pltpu.roll(kt, HEAD_DIM, 1)
    zero = jnp.zeros_like(kt)
    return [[jnp.where(lo, kt, zero), jnp.where(lo, zero, kr)],
            [jnp.where(lo, kr, zero), jnp.where(lo, zero, kt)]]


QSUB = 256
CMP_TQ = 256
CMP_KEY_CHUNK = 256
CMP_LOOKAHEAD = 4


def _cmp_kernel(q_ref, kc_ref, vct_ref, ovt_ref, g3t_ref, o_ref, sel_ref, *, n_slc):
    tq = q_ref.shape[0]
    n_cmp = kc_ref.shape[2]
    q0 = pl.program_id(0) * tq
    t = q0 + lax.broadcasted_iota(jnp.int32, (1, tq), 1)
    any_valid = t >= CMP_BLOCK - 1
    cur = t // SLC_BLOCK

    def body(n_keys, n_blk):
        c_end = lax.broadcasted_iota(jnp.int32, (n_keys, 1), 0) * CMP_STRIDE + (CMP_BLOCK - 1)
        bias = jnp.where(c_end <= t, 0.0, NEG)
        gates = _sigmoid(g3t_ref[...].astype(F32))

        ovt = ovt_ref[:n_blk, :n_keys]
        ones = jnp.ones((BF16_ROWS, n_keys), BF16)
        lhs = [jnp.concatenate([vct_ref[g, :, :n_keys], ones, ovt], axis=0) for g in range(N_GROUPS)]
        imp_row = HEAD_DIM + BF16_ROWS
        blk = lax.broadcasted_iota(jnp.int32, (n_blk, QSUB), 0)
        blk_f = blk.astype(F32)

        def select(cols, importance):
            forced = (blk == 0) | (blk == cur[:, cols]) | (blk == cur[:, cols] - 1)
            causal = blk * SLC_BLOCK <= t[:, cols]
            for g in range(N_GROUPS):
                work = jnp.where(causal & ~forced, importance[g], -1.0)
                sel = jnp.where(forced, 1.0, 0.0)
                for _ in range(N_SELECT - N_FORCED):
                    m = jnp.max(work, axis=0, keepdims=True)
                    first = jnp.min(jnp.where(work == m, blk_f, float(n_blk)), axis=0, keepdims=True)
                    hit = blk_f == first
                    sel = jnp.where(hit, 1.0, sel)
                    work = jnp.where(hit, -jnp.inf, work)
                sel_ref[g * n_slc:g * n_slc + n_blk, cols] = jnp.where(sel > 0.0, 0.0, NEG)
                if n_blk < n_slc:
                    sel_ref[g * n_slc + n_blk:(g + 1) * n_slc, cols] = jnp.full((n_slc - n_blk, QSUB), NEG, F32)

        units = [(h, slice(qs * QSUB, (qs + 1) * QSUB)) for qs in range(tq // QSUB) for h in range(N_HEADS)]

        def scores(unit):
            h, cols = unit
            p, e = divmod(h, 2)
            g = p // PAIRS_PER_GROUP
            s = _dot_nt(kc_ref[g, e, :n_keys, :], q_ref[cols, p * LANES:(p + 1) * LANES]) + bias[:, cols]
            return s, jnp.max(s, axis=0, keepdims=True)

        importance = [None] * N_GROUPS
        pending = [scores(u) for u in units[:CMP_LOOKAHEAD]]
        for n, (h, cols) in enumerate(units):
            g = h // HEADS_PER_GROUP
            if n + CMP_LOOKAHEAD < len(units):
                pending.append(scores(units[n + CMP_LOOKAHEAD]))
            s, m = pending.pop(0)
            pr = jnp.exp2(s - m).astype(BF16)
            res = _dot(lhs[g], pr)
            inv_l = jnp.where(any_valid[:, cols], 1.0 / res[HEAD_DIM:HEAD_DIM + 1, :], 0.0)
            o_h = res[:HEAD_DIM, :] * (inv_l * gates[3 * h:3 * h + 1, cols])
            o_ref[h * HEAD_DIM:(h + 1) * HEAD_DIM, cols] = o_h.astype(BF16)
            imp_h = res[imp_row:, :] * inv_l
            importance[g] = imp_h if importance[g] is None else importance[g] + imp_h
            if h == N_HEADS - 1:
                select(cols, importance)
                importance = [None] * N_GROUPS

    chunk_tokens = CMP_KEY_CHUNK * CMP_STRIDE
    variant = (q0 + tq - 1) // chunk_tokens
    for v in range(n_cmp // CMP_KEY_CHUNK):
        pl.when(variant == v)(functools.partial(body, (v + 1) * CMP_KEY_CHUNK,
                                                (v + 1) * chunk_tokens // SLC_BLOCK))


def _cmp_attention(p_arr, kc, vct, overlap_t, g3t):
    s = p_arr.shape[0]
    n_cmp = kc.shape[2]
    n_slc = s // SLC_BLOCK
    return pl.pallas_call(
        functools.partial(_cmp_kernel, n_slc=n_slc),
        grid=(s // CMP_TQ,),
        in_specs=[
            pl.BlockSpec((CMP_TQ, D_MODEL), lambda i: (i, P_Q // SEG_BLOCKS)),
            pl.BlockSpec((N_GROUPS, 2, n_cmp, LANES), lambda i: (0, 0, 0, 0)),
            pl.BlockSpec((N_GROUPS, HEAD_DIM, n_cmp), lambda i: (0, 0, 0)),
            pl.BlockSpec((n_slc, n_cmp), lambda i: (0, 0)),
            pl.BlockSpec((LANES, CMP_TQ), lambda i: (0, i)),
        ],
        out_specs=[
            pl.BlockSpec((D_MODEL, CMP_TQ), lambda i: (0, i)),
            pl.BlockSpec((N_GROUPS * n_slc, CMP_TQ), lambda i: (0, i)),
        ],
        out_shape=[
            jax.ShapeDtypeStruct((D_MODEL, s), BF16),
            jax.ShapeDtypeStruct((N_GROUPS * n_slc, s), F32),
        ],
        compiler_params=_params(("parallel",)),
        name="cmp_attention",
    )(p_arr, kc, vct, overlap_t, g3t)


def _overlap_matrix_t(n_cmp, n_slc):
    c0 = np.arange(n_cmp)[None, :] * CMP_STRIDE
    s0 = np.arange(n_slc)[:, None] * SLC_BLOCK
    ov = (c0 < s0 + SLC_BLOCK) & (c0 + CMP_BLOCK > s0)
    return jnp.asarray(ov, dtype=BF16)


SLC_TQ = 1024
SLC_TK = 1024
KSUB = 512
SLC_LOOKAHEAD = 4
ACC_ROWS = HEAD_DIM + BF16_ROWS


def _selected_kernel(qi_ref, ki_ref, fl_ref, q_ref, k_ref, vt_ref, g3t_ref, sel_ref, o_ref, m_ref, acc_ref, *,
                     n_slc):
    step = pl.program_id(0)
    tq = q_ref.shape[0]
    tk = k_ref.shape[0]
    flags = fl_ref[step]

    @pl.when((flags & 1) != 0)
    def _():
        m_ref[...] = jnp.full_like(m_ref, NEG)
        acc_ref[...] = jnp.zeros_like(acc_ref)

    def attend(first_subs):
        vt = vt_ref[...]
        ones = jnp.ones((ACC_ROWS - HEAD_DIM, KSUB), BF16)
        sub_blocks = KSUB // SLC_BLOCK
        k_half, v_aug, bias, units = [], [], [], []
        for ks, first_sub in enumerate(first_subs):
            c0 = first_sub * QSUB
            rows_k = slice(ks * KSUB, (ks + 1) * KSUB)
            t = qi_ref[step] * tq + c0 + lax.broadcasted_iota(jnp.int32, (1, tq - c0), 1)
            key = ki_ref[step] * tk + ks * KSUB + lax.broadcasted_iota(jnp.int32, (KSUB, 1), 0)
            causal_bias = jnp.where(key <= t, 0.0, NEG)
            bias_ks = []
            for g in range(N_GROUPS):
                first_block = pl.multiple_of(
                    g * n_slc + ki_ref[step] * (tk // SLC_BLOCK) + ks * sub_blocks, sub_blocks)
                rows = sel_ref[pl.ds(first_block, sub_blocks), c0:]
                per_key = jnp.broadcast_to(rows[:, None, :], (sub_blocks, SLC_BLOCK, tq - c0)).reshape(KSUB, tq - c0)
                bias_ks.append(jnp.minimum(per_key, causal_bias))
            bias.append(bias_ks)
            k_half.append(_split_heads(k_ref[rows_k, :]))
            v_aug.append([jnp.concatenate([vt[g * HEAD_DIM:(g + 1) * HEAD_DIM, rows_k], ones], axis=0)
                          for g in range(N_GROUPS)])
            units += [(ks, h, qs) for qs in range(first_sub, tq // QSUB) for h in range(N_HEADS)]

        def scores(unit):
            ks, h, qs = unit
            p, e = divmod(h, 2)
            g = p // PAIRS_PER_GROUP
            c0 = first_subs[ks] * QSUB
            q = q_ref[qs * QSUB:(qs + 1) * QSUB, p * LANES:(p + 1) * LANES]
            s = (_dot_nt(k_half[ks][g][e], q) + bias[ks][g][:, qs * QSUB - c0:(qs + 1) * QSUB - c0]).astype(BF16)
            return s, jnp.max(s, axis=0, keepdims=True)

        pending = [scores(u) for u in units[:SLC_LOOKAHEAD]]
        for n, (ks, h, qs) in enumerate(units):
            g = h // HEADS_PER_GROUP
            cols = slice(qs * QSUB, (qs + 1) * QSUB)
            if n + SLC_LOOKAHEAD < len(units):
                pending.append(scores(units[n + SLC_LOOKAHEAD]))
            s, s_max = pending.pop(0)
            m_prev = m_ref[h:h + 1, cols]
            m_new = jnp.maximum(m_prev, s_max.astype(F32))
            alpha = jnp.exp2(m_prev - m_new)
            pr = jnp.exp2(s - m_new.astype(BF16))
            m_ref[h:h + 1, cols] = m_new
            acc_ref[h, :, cols] = acc_ref[h, :, cols] * alpha + _dot(v_aug[ks][g], pr)

    n_ksub = tk // KSUB
    is_last = (flags & 2) != 0

    @pl.when(jnp.logical_not(is_last))
    def _():
        attend([0] * n_ksub)

    @pl.when(is_last)
    def _():
        attend([(tq - tk + ks * KSUB) // QSUB for ks in range(n_ksub)])
        gates = _sigmoid(g3t_ref[...].astype(F32))
        for h in range(N_HEADS):
            acc = acc_ref[h]
            scale = gates[3 * h + 1:3 * h + 2, :] / acc[HEAD_DIM:HEAD_DIM + 1, :]
            o_ref[h * HEAD_DIM:(h + 1) * HEAD_DIM, :] = (acc[:HEAD_DIM, :] * scale).astype(BF16)


def _causal_tile_tables(s, tq, tk):
    qi, ki, fl = [], [], []
    for i in range(s // tq):
        last = (i * tq + tq - 1) // tk
        for k in range(last + 1):
            qi.append(i)
            ki.append(k)
            fl.append((1 if k == 0 else 0) | (2 if k == last else 0))
    return (jnp.asarray(qi, jnp.int32), jnp.asarray(ki, jnp.int32), jnp.asarray(fl, jnp.int32))


def _selected_attention(p_arr, vt, g3t, sel_t):
    s = p_arr.shape[0]
    n_slc = s // SLC_BLOCK
    tq, tk = SLC_TQ, SLC_TK
    assert tq % tk == 0 and tk % KSUB == 0 and KSUB % QSUB == 0, "the kernel's slice skipping needs nested tiles"
    qi, ki, fl = _causal_tile_tables(s, tq, tk)
    grid_spec = pltpu.PrefetchScalarGridSpec(
        num_scalar_prefetch=3,
        grid=(qi.shape[0],),
        in_specs=[
            pl.BlockSpec((tq, D_MODEL), lambda n, qi, ki, fl: (qi[n], P_Q // SEG_BLOCKS)),
            pl.BlockSpec((tk, LANES), lambda n, qi, ki, fl: (ki[n], P_KS)),
            pl.BlockSpec((LANES, tk), lambda n, qi, ki, fl: (0, ki[n])),
            pl.BlockSpec((LANES, tq), lambda n, qi, ki, fl: (0, qi[n])),
            pl.BlockSpec((N_GROUPS * n_slc, tq), lambda n, qi, ki, fl: (0, qi[n])),
        ],
        out_specs=pl.BlockSpec((D_MODEL, tq), lambda n, qi, ki, fl: (0, qi[n])),
        scratch_shapes=[
            pltpu.VMEM((N_HEADS, tq), F32),
            pltpu.VMEM((N_HEADS, ACC_ROWS, tq), F32),
        ],
    )
    return pl.pallas_call(
        functools.partial(_selected_kernel, n_slc=n_slc),
        grid_spec=grid_spec,
        out_shape=jax.ShapeDtypeStruct((D_MODEL, s), BF16),
        compiler_params=_params(("arbitrary",)),
        name="selected_attention",
    )(qi, ki, fl, p_arr, p_arr, vt, g3t, sel_t)


WINDOW_TQ = 1024
WINDOW_BLOCK = 256
WINDOW_LOOKAHEAD = 4


def _window_kernel(q_ref, *refs, n_blocks):
    k_refs, vt_refs = refs[:n_blocks], refs[n_blocks:2 * n_blocks]
    g3t_ref, o_ref = refs[2 * n_blocks:]
    tq = q_ref.shape[0]
    span = WINDOW + QSUB
    k_half = _split_heads(jnp.concatenate([r[...] for r in k_refs], axis=0))
    vt = jnp.concatenate([r[...] for r in vt_refs], axis=1)
    ones = jnp.ones((BF16_ROWS, vt.shape[1]), BF16)
    v_aug = [jnp.concatenate([vt[g * HEAD_DIM:(g + 1) * HEAD_DIM, :], ones], axis=0) for g in range(N_GROUPS)]
    gates = _sigmoid(g3t_ref[...].astype(F32))

    r = lax.broadcasted_iota(jnp.int32, (span, 1), 0)
    c = lax.broadcasted_iota(jnp.int32, (1, QSUB), 1)
    in_band = (r > c) & (r <= c + WINDOW)
    first_key = pl.program_id(0) * tq - WINDOW
    n_sub = tq // QSUB
    bias = [jnp.where(in_band & (first_key + qs * QSUB + r >= 0), 0.0, NEG) for qs in range(n_sub)]

    units = [(h, qs) for qs in range(n_sub) for h in range(N_HEADS)]

    def scores(unit):
        h, qs = unit
        p, e = divmod(h, 2)
        g = p // PAIRS_PER_GROUP
        q = q_ref[qs * QSUB:(qs + 1) * QSUB, p * LANES:(p + 1) * LANES]
        s = (_dot_nt(k_half[g][e][qs * QSUB:qs * QSUB + span, :], q) + bias[qs]).astype(BF16)
        return s, jnp.max(s, axis=0, keepdims=True)

    pending = [scores(u) for u in units[:WINDOW_LOOKAHEAD]]
    for n, (h, qs) in enumerate(units):
        g = h // HEADS_PER_GROUP
        cols = slice(qs * QSUB, (qs + 1) * QSUB)
        if n + WINDOW_LOOKAHEAD < len(units):
            pending.append(scores(units[n + WINDOW_LOOKAHEAD]))
        s, m = pending.pop(0)
        res = _dot(v_aug[g][:, qs * QSUB:qs * QSUB + span], jnp.exp2(s - m))
        scale = gates[3 * h + 2:3 * h + 3, cols] / res[HEAD_DIM:HEAD_DIM + 1, :]
        o_ref[h * HEAD_DIM:(h + 1) * HEAD_DIM, cols] = (res[:HEAD_DIM, :] * scale).astype(BF16)


def _window_attention(p_arr, vt, g3t):
    s = p_arr.shape[0]
    tq = WINDOW_TQ
    n_blocks = (WINDOW + tq) // WINDOW_BLOCK
    back = WINDOW // WINDOW_BLOCK
    per_tile = tq // WINDOW_BLOCK

    def block(j):
        return lambda i: jnp.maximum(i * per_tile - back + j, 0)

    k_specs = [pl.BlockSpec((WINDOW_BLOCK, LANES), lambda i, b=block(j): (b(i), P_KW)) for j in range(n_blocks)]
    v_specs = [pl.BlockSpec((LANES, WINDOW_BLOCK), lambda i, b=block(j): (0, b(i))) for j in range(n_blocks)]
    return pl.pallas_call(
        functools.partial(_window_kernel, n_blocks=n_blocks),
        grid=(s // tq,),
        in_specs=[pl.BlockSpec((tq, D_MODEL), lambda i: (i, P_Q // SEG_BLOCKS))] + k_specs + v_specs
        + [pl.BlockSpec((LANES, tq), lambda i: (0, i))],
        out_specs=pl.BlockSpec((D_MODEL, tq), lambda i: (0, i)),
        out_shape=jax.ShapeDtypeStruct((D_MODEL, s), BF16),
        compiler_params=_params(("parallel",)),
        name="window_attention",
    )(p_arr, *([p_arr] * n_blocks), *([vt] * n_blocks), g3t)


OUT_TM = 512
PREV_ROWS = BF16_ROWS


def _out_kernel(x_ref, oc_ref, os_ref, ow_ref, cb_ref, cc_ref, cx_ref, pc_ref, px_ref, ga_ref, gb_ref,
                cw_ref, wa_ref, wb_ref, wo_ref, o_ref, u_ref):
    tm = x_ref.shape[0]
    prev = pc_ref[...].astype(F32) * px_ref[...].astype(F32)
    u_ref[0:PREV_ROWS, :] = jnp.where(pl.program_id(0) == 0, 0.0, prev)
    u_ref[PREV_ROWS:, :] = cc_ref[...].astype(F32) * cx_ref[...].astype(F32)
    cw = cw_ref[...]
    conv = (cw[0:1, :] * u_ref[pl.ds(PREV_ROWS - 2, tm), :]
            + cw[1:2, :] * u_ref[pl.ds(PREV_ROWS - 1, tm), :]
            + cw[2:3, :] * u_ref[pl.ds(PREV_ROWS, tm), :])
    y_b = _dot((cb_ref[...].astype(F32) * conv).astype(BF16), wb_ref[...])

    o_nsa_t = oc_ref[...].astype(F32) + os_ref[...].astype(F32) + ow_ref[...].astype(F32)
    y_a = _dot(o_nsa_t.T.astype(BF16), wa_ref[...])

    merged = _sigmoid(ga_ref[...].astype(F32)) * y_a + _sigmoid(gb_ref[...].astype(F32)) * y_b
    o_ref[...] = x_ref[...] + _dot(merged.astype(BF16), wo_ref[...])


def _out(x1, o_cmp_t, o_slc_t, o_win_t, p_arr, conv_w, w_nsa_out, w_conv_out, w_out):
    s = x1.shape[0]
    tm = OUT_TM
    row = lambda i: (i, 0)
    col = lambda i: (0, i)
    prev_blocks = tm // PREV_ROWS

    def seg(c):
        return pl.BlockSpec((tm, D_MODEL), lambda i: (i, c // SEG_BLOCKS))

    def seg_prev(c):
        return pl.BlockSpec((PREV_ROWS, D_MODEL), lambda i: (jnp.maximum(i * prev_blocks - 1, 0), c // SEG_BLOCKS))

    full = lambda shape: pl.BlockSpec(shape, lambda i: (0, 0))
    return pl.pallas_call(
        _out_kernel,
        grid=(s // tm,),
        in_specs=[
            pl.BlockSpec((tm, D_MODEL), row), pl.BlockSpec((D_MODEL, tm), col),
            pl.BlockSpec((D_MODEL, tm), col), pl.BlockSpec((D_MODEL, tm), col),
            seg(P_CB), seg(P_CC), seg(P_CX), seg_prev(P_CC), seg_prev(P_CX), seg(P_GA), seg(P_GB),
            full((CONV_K, CONV_WIDTH)), full((D_MODEL, D_MODEL)), full((CONV_WIDTH, D_MODEL)),
            full((D_MODEL, D_MODEL)),
        ],
        out_specs=pl.BlockSpec((tm, D_MODEL), row),
        out_shape=jax.ShapeDtypeStruct((s, D_MODEL), F32),
        scratch_shapes=[pltpu.VMEM((tm + PREV_ROWS, CONV_WIDTH), F32)],
        compiler_params=_params(("parallel",)),
        name="merge_out",
    )(x1, o_cmp_t, o_slc_t, o_win_t, p_arr, p_arr, p_arr, p_arr, p_arr, p_arr, p_arr,
      conv_w, w_nsa_out, w_conv_out, w_out)


def _layer(x, ffn1_norm, ffn1_w_gate, ffn1_w_up, ffn1_w_down, mix_norm, w_in, cmp_pe_k, cmp_pe_v,
           cmp_k_w1, cmp_k_w2, cmp_v_w1, cmp_v_w2, conv_w, w_nsa_out, w_conv_out, w_out,
           ffn2_norm, ffn2_w_gate, ffn2_w_up, ffn2_w_down, final_g, final_norm):
    s = x.shape[0]
    n_cmp = s // CMP_STRIDE
    n_slc = s // SLC_BLOCK
    row = lambda v: v.reshape(1, -1).astype(F32)

    x1 = _ffn(x, row(ffn1_norm), ffn1_w_gate.astype(BF16), ffn1_w_up.astype(BF16),
              ffn1_w_down.astype(BF16), row(final_g), False)
    p_arr, vst, vwt, g3t = _proj(x1, row(mix_norm), _proj_weight(w_in))

    col = lambda c: p_arr[:, c * LANES:(c + 1) * LANES]
    chunks = lambda c: col(c).reshape(n_cmp, CMP_STRIDE * LANES)
    kc = _compress(chunks(P_KC), *_compress_weights(cmp_pe_k, cmp_k_w1, cmp_k_w2, False), False)
    vct = _compress(chunks(P_VC), *_compress_weights(cmp_pe_v, cmp_v_w1, cmp_v_w2, True), True)

    o_cmp_t, sel_t = _cmp_attention(p_arr, kc, vct, _overlap_matrix_t(n_cmp, n_slc), g3t)
    o_slc_t = _selected_attention(p_arr, vst, g3t, sel_t)
    o_win_t = _window_attention(p_arr, vwt, g3t)
    x2 = _out(x1, o_cmp_t, o_slc_t, o_win_t, p_arr, conv_w.astype(F32), w_nsa_out.astype(BF16),
              w_conv_out.astype(BF16), w_out.astype(BF16))
    return _ffn(x2, row(ffn2_norm), ffn2_w_gate.astype(BF16), ffn2_w_up.astype(BF16),
                ffn2_w_down.astype(BF16), row(final_g), final_norm)


def kernel(x, ffn1_norm, ffn1_w_gate, ffn1_w_up, ffn1_w_down, mix_norm, w_in, cmp_pe_k, cmp_pe_v,
           cmp_k_w1, cmp_k_w2, cmp_v_w1, cmp_v_w2, conv_w, w_nsa_out, w_conv_out, w_out, ffn2_norm,
           ffn2_w_gate, ffn2_w_up, ffn2_w_down, final_norm):
    batch, _, _ = x.shape
    depth = ffn1_norm.shape[0]
    outs = []
    for b in range(batch):
        xb = x[b]
        for l in range(depth):
            xb = _layer(xb, ffn1_norm[l], ffn1_w_gate[l], ffn1_w_up[l], ffn1_w_down[l], mix_norm[l],
                        w_in[l], cmp_pe_k[l], cmp_pe_v[l], cmp_k_w1[l], cmp_k_w2[l], cmp_v_w1[l],
                        cmp_v_w2[l], conv_w[l], w_nsa_out[l], w_conv_out[l], w_out[l], ffn2_norm[l],
                        ffn2_w_gate[l], ffn2_w_up[l], ffn2_w_down[l], final_norm, l == depth - 1)
        outs.append(xb)
    return outs[0][None] if batch == 1 else jnp.stack(outs)
```

```python
import functools

import jax
import jax.numpy as jnp
import numpy as np
from jax import lax
from jax.experimental import pallas as pl
from jax.experimental.pallas import tpu as pltpu

D_MODEL = 1024
N_HEADS = 16
HEAD_DIM = 64
N_GROUPS = 2
HEADS_PER_GROUP = N_HEADS // N_GROUPS
N_PAIRS = N_HEADS // 2
PAIRS_PER_GROUP = N_PAIRS // N_GROUPS
CMP_BLOCK = 32
CMP_STRIDE = 16
CMP_HIDDEN = 256
SLC_BLOCK = 64
N_SELECT = 16
WINDOW = 512
CONV_WIDTH = 1024
CONV_K = 3
D_FF = 2816
EPS = 1e-6
NEG = -(2.0 ** 100)
N_FORCED = 3
LOG2E = float(np.log2(np.e))
LANES = 128
BF16_ROWS = 16

P_Q, P_CB, P_CC, P_CX, P_GA, P_GB = 0, 8, 16, 24, 32, 40
P_KC, P_VC, P_KS, P_VS, P_KW, P_VW, P_G3 = 48, 49, 50, 51, 52, 53, 54
P_BLOCKS = 56
P_COLS = P_BLOCKS * LANES
SEG_BLOCKS = D_MODEL // LANES

F32 = jnp.float32
BF16 = jnp.bfloat16

VMEM_LIMIT = 52 * 1024 * 1024


def _dot(a, b):
    return jnp.dot(a, b, preferred_element_type=F32)


def _dot_nt(a, b):
    return lax.dot_general(a, b, (((1,), (1,)), ((), ())), preferred_element_type=F32)


def _sigmoid(x):
    return 1.0 / (1.0 + jnp.exp(-x))


def _rms(x, g):
    return x * lax.rsqrt(jnp.mean(x * x, axis=-1, keepdims=True) + EPS) * g


def _params(sem, limit=VMEM_LIMIT, flags=None):
    return pltpu.CompilerParams(dimension_semantics=sem, vmem_limit_bytes=limit, flags=flags)


FFN_TM = 512
FFN_CHUNK = 256


def _ffn_kernel(x_ref, g_ref, wg_ref, wu_ref, wd_ref, fg_ref, o_ref, *, final_norm):
    x = x_ref[...]
    h = _rms(x, g_ref[...]).astype(BF16)
    acc = None
    for c in range(D_FF // FFN_CHUNK):
        cols = slice(c * FFN_CHUNK, (c + 1) * FFN_CHUNK)
        gate = _dot(h, wg_ref[:, cols])
        up = _dot(h, wu_ref[:, cols])
        a = (gate * _sigmoid(gate) * up).astype(BF16)
        part = _dot(a, wd_ref[cols, :])
        acc = part if acc is None else acc + part
    y = x + 0.5 * acc
    if final_norm:
        y = _rms(y, fg_ref[...])
    o_ref[...] = y


def _ffn(x, norm_g, wg, wu, wd, final_g, final_norm):
    s = x.shape[0]
    resident = lambda shape: pl.BlockSpec(shape, lambda i: (0, 0), pipeline_mode=pl.Buffered(1))
    return pl.pallas_call(
        functools.partial(_ffn_kernel, final_norm=final_norm),
        grid=(s // FFN_TM,),
        in_specs=[
            pl.BlockSpec((FFN_TM, D_MODEL), lambda i: (i, 0)),
            resident((1, D_MODEL)),
            resident((D_MODEL, D_FF)),
            resident((D_MODEL, D_FF)),
            resident((D_FF, D_MODEL)),
            resident((1, D_MODEL)),
        ],
        out_specs=pl.BlockSpec((FFN_TM, D_MODEL), lambda i: (i, 0)),
        out_shape=jax.ShapeDtypeStruct((s, D_MODEL), F32),
        compiler_params=_params(("parallel",)),
        name="ffn_final" if final_norm else "ffn",
    )(x, norm_g, wg, wu, wd, final_g)


PROJ_TM = 512
PROJ_CHUNK = 512


P_TRANSPOSED = (P_VS, P_VW, P_G3)


def _proj_kernel(x_ref, g_ref, w_ref, o_ref, *t_refs):
    h = _rms(x_ref[...], g_ref[...]).astype(BF16)
    chunk_blocks = PROJ_CHUNK // LANES
    for c in range(P_COLS // PROJ_CHUNK):
        cols = slice(c * PROJ_CHUNK, (c + 1) * PROJ_CHUNK)
        y = _dot(h, w_ref[:, cols])
        o_ref[:, cols] = y.astype(BF16)
        for t_ref, blk in zip(t_refs, P_TRANSPOSED):
            if blk // chunk_blocks == c:
                lo = (blk % chunk_blocks) * LANES
                t_ref[...] = y[:, lo:lo + LANES].T.astype(BF16)


def _proj(x, norm_g, w):
    s = x.shape[0]
    resident = lambda shape: pl.BlockSpec(shape, lambda i: (0, 0), pipeline_mode=pl.Buffered(1))
    return pl.pallas_call(
        _proj_kernel,
        grid=(s // PROJ_TM,),
        in_specs=[
            pl.BlockSpec((PROJ_TM, D_MODEL), lambda i: (i, 0)),
            resident((1, D_MODEL)),
            resident((D_MODEL, P_COLS)),
        ],
        out_specs=[pl.BlockSpec((PROJ_TM, P_COLS), lambda i: (i, 0))]
        + [pl.BlockSpec((LANES, PROJ_TM), lambda i: (0, i)) for _ in P_TRANSPOSED],
        out_shape=[jax.ShapeDtypeStruct((s, P_COLS), BF16)]
        + [jax.ShapeDtypeStruct((LANES, s), BF16) for _ in P_TRANSPOSED],
        compiler_params=_params(("parallel",)),
        name="proj",
    )(x, norm_g, w)


def _proj_weight(w_in):
    q, kc, vc, ks, vs, kw, vw, g3, cb, cc, cx, ga, gb = jnp.split(
        w_in, np.cumsum([1024, 128, 128, 128, 128, 128, 128, 48, 1024, 1024, 1024, 1024]).tolist(), axis=1)
    g3 = jnp.pad(g3, ((0, 0), (0, (P_BLOCKS - P_G3) * LANES - g3.shape[1])))
    w = jnp.concatenate([q * (HEAD_DIM ** -0.5 * LOG2E), cb, cc, cx, ga, gb, kc, vc, ks, vs, kw, vw, g3], axis=1)
    return w.astype(BF16)


def _gelu_tanh(x):
    return 0.5 * x * (1.0 + jnp.tanh(np.sqrt(2.0 / np.pi).astype(np.float32) * (x + 0.044715 * (x * x * x))))


def _compress_kernel(cf_ref, pe_ref, w1_ref, w2_ref, o_ref, *, transposed):
    cf = cf_ref[...].astype(F32)
    top = (cf + pe_ref[0]).astype(BF16)
    bot = (cf + pe_ref[1]).astype(BF16)
    a = _dot(top, w1_ref[0, 0])
    b = _dot(bot, w1_ref[0, 1])
    n = a.shape[0]
    pre = a + pltpu.roll(b, n - 1, 0)
    h = _gelu_tanh(pre).astype(BF16)
    if transposed:
        o_ref[0] = _dot_nt(w2_ref[...], h).astype(BF16)
    else:
        o_ref[0, 0] = _dot(h, w2_ref[0]).astype(BF16)
        o_ref[0, 1] = _dot(h, w2_ref[1]).astype(BF16)


def _compress(cf, pe, w1, w2, transposed):
    n = cf.shape[0]
    width = CMP_STRIDE * LANES
    if transposed:
        w2_spec = pl.BlockSpec((HEAD_DIM, CMP_HIDDEN), lambda g: (0, 0))
        out_spec = pl.BlockSpec((1, HEAD_DIM, n), lambda g: (g, 0, 0))
        out_shape = jax.ShapeDtypeStruct((N_GROUPS, HEAD_DIM, n), BF16)
    else:
        w2_spec = pl.BlockSpec((2, CMP_HIDDEN, LANES), lambda g: (0, 0, 0))
        out_spec = pl.BlockSpec((1, 2, n, LANES), lambda g: (g, 0, 0, 0))
        out_shape = jax.ShapeDtypeStruct((N_GROUPS, 2, n, LANES), BF16)
    return pl.pallas_call(
        functools.partial(_compress_kernel, transposed=transposed),
        grid=(N_GROUPS,),
        in_specs=[
            pl.BlockSpec((n, width), lambda g: (0, 0)),
            pl.BlockSpec((2, 1, width), lambda g: (0, 0, 0)),
            pl.BlockSpec((1, 2, width, CMP_HIDDEN), lambda g: (g, 0, 0, 0)),
            w2_spec,
        ],
        out_specs=out_spec,
        out_shape=out_shape,
        compiler_params=_params(("parallel",)),
        name="compress_v" if transposed else "compress_k",
    )(cf, pe, w1, w2)


def _compress_weights(pe, w1, w2, transposed):
    w1h = w1.reshape(2, CMP_STRIDE, HEAD_DIM, CMP_HIDDEN)
    zeros = jnp.zeros_like(w1h)
    w1g = jnp.stack([jnp.concatenate([w1h, zeros], axis=2), jnp.concatenate([zeros, w1h], axis=2)])
    w1g = w1g.reshape(N_GROUPS, 2, CMP_STRIDE * LANES, CMP_HIDDEN).astype(BF16)
    peh = pe.reshape(2, CMP_STRIDE, HEAD_DIM)
    pe2 = jnp.concatenate([peh, peh], axis=2).reshape(2, 1, CMP_STRIDE * LANES)
    if transposed:
        w2p = w2.T.astype(BF16)
    else:
        z2 = jnp.zeros_like(w2)
        w2p = jnp.stack([jnp.concatenate([w2, z2], axis=1), jnp.concatenate([z2, w2], axis=1)]).astype(BF16)
    return pe2, w1g, w2p


def _split_heads(kt):
    lo = lax.broadcasted_iota(jnp.int32, kt.shape, 1) < HEAD_DIM
    kr = pltpu.roll(kt, HEAD_DIM, 1)
    zero = jnp.zeros_like(kt)
    return [[jnp.where(lo, kt, zero), jnp.where(lo, zero, kr)],
            [jnp.where(lo, kr, zero), jnp.where(lo, zero, kt)]]


QSUB = 256
CMP_TQ = 256
CMP_KEY_CHUNK = 256
CMP_LOOKAHEAD = 4


def _cmp_kernel(q_ref, kc_ref, vct_ref, ovt_ref, g3t_ref, o_ref, sel_ref, *, n_slc):
    tq = q_ref.shape[0]
    n_cmp = kc_ref.shape[2]
    q0 = pl.program_id(0) * tq
    t = q0 + lax.broadcasted_iota(jnp.int32, (1, tq), 1)
    any_valid = t >= CMP_BLOCK - 1
    cur = t // SLC_BLOCK

    def body(n_keys, n_blk):
        c_end = lax.broadcasted_iota(jnp.int32, (n_keys, 1), 0) * CMP_STRIDE + (CMP_BLOCK - 1)
        bias = jnp.where(c_end <= t, 0.0, NEG)
        gates = _sigmoid(g3t_ref[...].astype(F32))

        ovt = ovt_ref[:n_blk, :n_keys]
        ones = jnp.ones((BF16_ROWS, n_keys), BF16)
        lhs = [jnp.concatenate([vct_ref[g, :, :n_keys], ones, ovt], axis=0) for g in range(N_GROUPS)]
        imp_row = HEAD_DIM + BF16_ROWS
        blk = lax.broadcasted_iota(jnp.int32, (n_blk, QSUB), 0)
        blk_f = blk.astype(F32)

        def select(cols, importance):
            forced = (blk == 0) | (blk == cur[:, cols]) | (blk == cur[:, cols] - 1)
            causal = blk * SLC_BLOCK <= t[:, cols]
            for g in range(N_GROUPS):
                work = jnp.where(causal & ~forced, importance[g], -1.0)
                sel = jnp.where(forced, 1.0, 0.0)
                for _ in range(N_SELECT - N_FORCED):
                    m = jnp.max(work, axis=0, keepdims=True)
                    first = jnp.min(jnp.where(work == m, blk_f, float(n_blk)), axis=0, keepdims=True)
                    hit = blk_f == first
                    sel = jnp.where(hit, 1.0, sel)
                    work = jnp.where(hit, -jnp.inf, work)
                sel_ref[g * n_slc:g * n_slc + n_blk, cols] = jnp.where(sel > 0.0, 0.0, NEG)
                if n_blk < n_slc:
                    sel_ref[g * n_slc + n_blk:(g + 1) * n_slc, cols] = jnp.full((n_slc - n_blk, QSUB), NEG, F32)

        units = [(h, slice(qs * QSUB, (qs + 1) * QSUB)) for qs in range(tq // QSUB) for h in range(N_HEADS)]

        def scores(unit):
            h, cols = unit
            p, e = divmod(h, 2)
            g = p // PAIRS_PER_GROUP
            s = _dot_nt(kc_ref[g, e, :n_keys, :], q_ref[cols, p * LANES:(p + 1) * LANES]) + bias[:, cols]
            s = s.astype(BF16)
            return s, jnp.max(s, axis=0, keepdims=True)

        importance = [None] * N_GROUPS
        pending = [scores(u) for u in units[:CMP_LOOKAHEAD]]
        for n, (h, cols) in enumerate(units):
            g = h // HEADS_PER_GROUP
            if n + CMP_LOOKAHEAD < len(units):
                pending.append(scores(units[n + CMP_LOOKAHEAD]))
            s, m = pending.pop(0)
            pr = jnp.exp2(s - m)
            res = _dot(lhs[g], pr)
            inv_l = jnp.where(any_valid[:, cols], 1.0 / res[HEAD_DIM:HEAD_DIM + 1, :], 0.0)
            o_h = res[:HEAD_DIM, :] * (inv_l * gates[3 * h:3 * h + 1, cols])
            o_ref[h * HEAD_DIM:(h + 1) * HEAD_DIM, cols] = o_h.astype(BF16)
            imp_h = res[imp_row:, :] * inv_l
            importance[g] = imp_h if importance[g] is None else importance[g] + imp_h
            if h == N_HEADS - 1:
                select(cols, importance)
                importance = [None] * N_GROUPS

    chunk_tokens = CMP_KEY_CHUNK * CMP_STRIDE
    variant = (q0 + tq - 1) // chunk_tokens
    for v in range(n_cmp // CMP_KEY_CHUNK):
        pl.when(variant == v)(functools.partial(body, (v + 1) * CMP_KEY_CHUNK,
                                                (v + 1) * chunk_tokens // SLC_BLOCK))


def _cmp_attention(p_arr, kc, vct, overlap_t, g3t):
    s = p_arr.shape[0]
    n_cmp = kc.shape[2]
    n_slc = s // SLC_BLOCK
    return pl.pallas_call(
        functools.partial(_cmp_kernel, n_slc=n_slc),
        grid=(s // CMP_TQ,),
        in_specs=[
            pl.BlockSpec((CMP_TQ, D_MODEL), lambda i: (i, P_Q // SEG_BLOCKS)),
            pl.BlockSpec((N_GROUPS, 2, n_cmp, LANES), lambda i: (0, 0, 0, 0)),
            pl.BlockSpec((N_GROUPS, HEAD_DIM, n_cmp), lambda i: (0, 0, 0)),
            pl.BlockSpec((n_slc, n_cmp), lambda i: (0, 0)),
            pl.BlockSpec((LANES, CMP_TQ), lambda i: (0, i)),
        ],
        out_specs=[
            pl.BlockSpec((D_MODEL, CMP_TQ), lambda i: (0, i)),
            pl.BlockSpec((N_GROUPS * n_slc, CMP_TQ), lambda i: (0, i)),
        ],
        out_shape=[
            jax.ShapeDtypeStruct((D_MODEL, s), BF16),
            jax.ShapeDtypeStruct((N_GROUPS * n_slc, s), F32),
        ],
        compiler_params=_params(("parallel",)),
        name="cmp_attention",
    )(p_arr, kc, vct, overlap_t, g3t)


def _overlap_matrix_t(n_cmp, n_slc):
    c0 = np.arange(n_cmp)[None, :] * CMP_STRIDE
    s0 = np.arange(n_slc)[:, None] * SLC_BLOCK
    ov = (c0 < s0 + SLC_BLOCK) & (c0 + CMP_BLOCK > s0)
    return jnp.asarray(ov, dtype=BF16)


SLC_TQ = 1024
SLC_TK = 1024
KSUB = 512
SLC_LOOKAHEAD = 4
ACC_ROWS = HEAD_DIM + BF16_ROWS


def _selected_kernel(qi_ref, ki_ref, fl_ref, q_ref, k_ref, vt_ref, g3t_ref, sel_ref, o_ref, m_ref, acc_ref, *,
                     n_slc):
    step = pl.program_id(0)
    tq = q_ref.shape[0]
    tk = k_ref.shape[0]
    flags = fl_ref[step]

    @pl.when((flags & 1) != 0)
    def _():
        m_ref[...] = jnp.full_like(m_ref, NEG)
        acc_ref[...] = jnp.zeros_like(acc_ref)

    def attend(first_subs):
        vt = vt_ref[...]
        ones = jnp.ones((ACC_ROWS - HEAD_DIM, KSUB), BF16)
        sub_blocks = KSUB // SLC_BLOCK
        k_half, v_aug, bias, units = [], [], [], []
        for ks, first_sub in enumerate(first_subs):
            c0 = first_sub * QSUB
            rows_k = slice(ks * KSUB, (ks + 1) * KSUB)
            t = qi_ref[step] * tq + c0 + lax.broadcasted_iota(jnp.int32, (1, tq - c0), 1)
            key = ki_ref[step] * tk + ks * KSUB + lax.broadcasted_iota(jnp.int32, (KSUB, 1), 0)
            causal_bias = jnp.where(key <= t, 0.0, NEG)
            bias_ks = []
            for g in range(N_GROUPS):
                first_block = pl.multiple_of(
                    g * n_slc + ki_ref[step] * (tk // SLC_BLOCK) + ks * sub_blocks, sub_blocks)
                rows = sel_ref[pl.ds(first_block, sub_blocks), c0:]
                per_key = jnp.broadcast_to(rows[:, None, :], (sub_blocks, SLC_BLOCK, tq - c0)).reshape(KSUB, tq - c0)
                bias_ks.append(jnp.minimum(per_key, causal_bias))
            bias.append(bias_ks)
            k_half.append(_split_heads(k_ref[rows_k, :]))
            v_aug.append([jnp.concatenate([vt[g * HEAD_DIM:(g + 1) * HEAD_DIM, rows_k], ones], axis=0)
                          for g in range(N_GROUPS)])
            units += [(ks, h, qs) for qs in range(first_sub, tq // QSUB) for h in range(N_HEADS)]

        def scores(unit):
            ks, h, qs = unit
            p, e = divmod(h, 2)
            g = p // PAIRS_PER_GROUP
            c0 = first_subs[ks] * QSUB
            q = q_ref[qs * QSUB:(qs + 1) * QSUB, p * LANES:(p + 1) * LANES]
            s = (_dot_nt(k_half[ks][g][e], q) + bias[ks][g][:, qs * QSUB - c0:(qs + 1) * QSUB - c0]).astype(BF16)
            return s, jnp.max(s, axis=0, keepdims=True)

        pending = [scores(u) for u in units[:SLC_LOOKAHEAD]]
        for n, (ks, h, qs) in enumerate(units):
            g = h // HEADS_PER_GROUP
            cols = slice(qs * QSUB, (qs + 1) * QSUB)
            if n + SLC_LOOKAHEAD < len(units):
                pending.append(scores(units[n + SLC_LOOKAHEAD]))
            s, s_max = pending.pop(0)
            m_prev = m_ref[h:h + 1, cols]
            m_new = jnp.maximum(m_prev, s_max.astype(F32))
            alpha = jnp.exp2(m_prev - m_new)
            pr = jnp.exp2(s - m_new.astype(BF16))
            m_ref[h:h + 1, cols] = m_new
            acc_ref[h, :, cols] = acc_ref[h, :, cols] * alpha + _dot(v_aug[ks][g], pr)

    n_ksub = tk // KSUB
    is_last = (flags & 2) != 0

    @pl.when(jnp.logical_not(is_last))
    def _():
        attend([0] * n_ksub)

    @pl.when(is_last)
    def _():
        attend([(tq - tk + ks * KSUB) // QSUB for ks in range(n_ksub)])
        gates = _sigmoid(g3t_ref[...].astype(F32))
        for h in range(N_HEADS):
            acc = acc_ref[h]
            scale = gates[3 * h + 1:3 * h + 2, :] / acc[HEAD_DIM:HEAD_DIM + 1, :]
            o_ref[h * HEAD_DIM:(h + 1) * HEAD_DIM, :] = (acc[:HEAD_DIM, :] * scale).astype(BF16)


def _causal_tile_tables(s, tq, tk):
    qi, ki, fl = [], [], []
    for i in range(s // tq):
        last = (i * tq + tq - 1) // tk
        for k in range(last + 1):
            qi.append(i)
            ki.append(k)
            fl.append((1 if k == 0 else 0) | (2 if k == last else 0))
    return (jnp.asarray(qi, jnp.int32), jnp.asarray(ki, jnp.int32), jnp.asarray(fl, jnp.int32))


def _selected_attention(p_arr, vt, g3t, sel_t):
    s = p_arr.shape[0]
    n_slc = s // SLC_BLOCK
    tq, tk = SLC_TQ, SLC_TK
    assert tq % tk == 0 and tk % KSUB == 0 and KSUB % QSUB == 0, "the kernel's slice skipping needs nested tiles"
    qi, ki, fl = _causal_tile_tables(s, tq, tk)
    grid_spec = pltpu.PrefetchScalarGridSpec(
        num_scalar_prefetch=3,
        grid=(qi.shape[0],),
        in_specs=[
            pl.BlockSpec((tq, D_MODEL), lambda n, qi, ki, fl: (qi[n], P_Q // SEG_BLOCKS)),
            pl.BlockSpec((tk, LANES), lambda n, qi, ki, fl: (ki[n], P_KS)),
            pl.BlockSpec((LANES, tk), lambda n, qi, ki, fl: (0, ki[n])),
            pl.BlockSpec((LANES, tq), lambda n, qi, ki, fl: (0, qi[n])),
            pl.BlockSpec((N_GROUPS * n_slc, tq), lambda n, qi, ki, fl: (0, qi[n])),
        ],
        out_specs=pl.BlockSpec((D_MODEL, tq), lambda n, qi, ki, fl: (0, qi[n])),
        scratch_shapes=[
            pltpu.VMEM((N_HEADS, tq), F32),
            pltpu.VMEM((N_HEADS, ACC_ROWS, tq), F32),
        ],
    )
    return pl.pallas_call(
        functools.partial(_selected_kernel, n_slc=n_slc),
        grid_spec=grid_spec,
        out_shape=jax.ShapeDtypeStruct((D_MODEL, s), BF16),
        compiler_params=_params(("arbitrary",)),
        name="selected_attention",
    )(qi, ki, fl, p_arr, p_arr, vt, g3t, sel_t)


WINDOW_TQ = 1024
WINDOW_BLOCK = 256
WINDOW_LOOKAHEAD = 4


def _window_kernel(q_ref, *refs, n_blocks):
    k_refs, vt_refs = refs[:n_blocks], refs[n_blocks:2 * n_blocks]
    g3t_ref, o_ref = refs[2 * n_blocks:]
    tq = q_ref.shape[0]
    span = WINDOW + QSUB
    k_half = _split_heads(jnp.concatenate([r[...] for r in k_refs], axis=0))
    vt = jnp.concatenate([r[...] for r in vt_refs], axis=1)
    ones = jnp.ones((BF16_ROWS, vt.shape[1]), BF16)
    v_aug = [jnp.concatenate([vt[g * HEAD_DIM:(g + 1) * HEAD_DIM, :], ones], axis=0) for g in range(N_GROUPS)]
    gates = _sigmoid(g3t_ref[...].astype(F32))

    r = lax.broadcasted_iota(jnp.int32, (span, 1), 0)
    c = lax.broadcasted_iota(jnp.int32, (1, QSUB), 1)
    in_band = (r > c) & (r <= c + WINDOW)
    first_key = pl.program_id(0) * tq - WINDOW
    n_sub = tq // QSUB
    bias = [jnp.where(in_band & (first_key + qs * QSUB + r >= 0), 0.0, NEG) for qs in range(n_sub)]

    units = [(h, qs) for qs in range(n_sub) for h in range(N_HEADS)]

    def scores(unit):
        h, qs = unit
        p, e = divmod(h, 2)
        g = p // PAIRS_PER_GROUP
        q = q_ref[qs * QSUB:(qs + 1) * QSUB, p * LANES:(p + 1) * LANES]
        s = (_dot_nt(k_half[g][e][qs * QSUB:qs * QSUB + span, :], q) + bias[qs]).astype(BF16)
        return s, jnp.max(s, axis=0, keepdims=True)

    pending = [scores(u) for u in units[:WINDOW_LOOKAHEAD]]
    for n, (h, qs) in enumerate(units):
        g = h // HEADS_PER_GROUP
        cols = slice(qs * QSUB, (qs + 1) * QSUB)
        if n + WINDOW_LOOKAHEAD < len(units):
            pending.append(scores(units[n + WINDOW_LOOKAHEAD]))
        s, m = pending.pop(0)
        res = _dot(v_aug[g][:, qs * QSUB:qs * QSUB + span], jnp.exp2(s - m))
        scale = gates[3 * h + 2:3 * h + 3, cols] / res[HEAD_DIM:HEAD_DIM + 1, :]
        o_ref[h * HEAD_DIM:(h + 1) * HEAD_DIM, cols] = (res[:HEAD_DIM, :] * scale).astype(BF16)


def _window_attention(p_arr, vt, g3t):
    s = p_arr.shape[0]
    tq = WINDOW_TQ
    n_blocks = (WINDOW + tq) // WINDOW_BLOCK
    back = WINDOW // WINDOW_BLOCK
    per_tile = tq // WINDOW_BLOCK

    def block(j):
        return lambda i: jnp.maximum(i * per_tile - back + j, 0)

    k_specs = [pl.BlockSpec((WINDOW_BLOCK, LANES), lambda i, b=block(j): (b(i), P_KW)) for j in range(n_blocks)]
    v_specs = [pl.BlockSpec((LANES, WINDOW_BLOCK), lambda i, b=block(j): (0, b(i))) for j in range(n_blocks)]
    return pl.pallas_call(
        functools.partial(_window_kernel, n_blocks=n_blocks),
        grid=(s // tq,),
        in_specs=[pl.BlockSpec((tq, D_MODEL), lambda i: (i, P_Q // SEG_BLOCKS))] + k_specs + v_specs
        + [pl.BlockSpec((LANES, tq), lambda i: (0, i))],
        out_specs=pl.BlockSpec((D_MODEL, tq), lambda i: (0, i)),
        out_shape=jax.ShapeDtypeStruct((D_MODEL, s), BF16),
        compiler_params=_params(("parallel",)),
        name="window_attention",
    )(p_arr, *([p_arr] * n_blocks), *([vt] * n_blocks), g3t)


OUT_TM = 512
PREV_ROWS = BF16_ROWS


def _out_kernel(x_ref, oc_ref, os_ref, ow_ref, cb_ref, cc_ref, cx_ref, pc_ref, px_ref, ga_ref, gb_ref,
                cw_ref, wa_ref, wb_ref, wo_ref, o_ref, u_ref):
    tm = x_ref.shape[0]
    prev = pc_ref[...].astype(F32) * px_ref[...].astype(F32)
    u_ref[0:PREV_ROWS, :] = jnp.where(pl.program_id(0) == 0, 0.0, prev)
    u_ref[PREV_ROWS:, :] = cc_ref[...].astype(F32) * cx_ref[...].astype(F32)
    cw = cw_ref[...]
    conv = (cw[0:1, :] * u_ref[pl.ds(PREV_ROWS - 2, tm), :]
            + cw[1:2, :] * u_ref[pl.ds(PREV_ROWS - 1, tm), :]
            + cw[2:3, :] * u_ref[pl.ds(PREV_ROWS, tm), :])
    y_b = _dot((cb_ref[...].astype(F32) * conv).astype(BF16), wb_ref[...])

    o_nsa_t = oc_ref[...].astype(F32) + os_ref[...].astype(F32) + ow_ref[...].astype(F32)
    y_a = _dot(o_nsa_t.T.astype(BF16), wa_ref[...])

    merged = _sigmoid(ga_ref[...].astype(F32)) * y_a + _sigmoid(gb_ref[...].astype(F32)) * y_b
    o_ref[...] = x_ref[...] + _dot(merged.astype(BF16), wo_ref[...])


def _out(x1, o_cmp_t, o_slc_t, o_win_t, p_arr, conv_w, w_nsa_out, w_conv_out, w_out):
    s = x1.shape[0]
    tm = OUT_TM
    row = lambda i: (i, 0)
    col = lambda i: (0, i)
    prev_blocks = tm // PREV_ROWS

    def seg(c):
        return pl.BlockSpec((tm, D_MODEL), lambda i: (i, c // SEG_BLOCKS))

    def seg_prev(c):
        return pl.BlockSpec((PREV_ROWS, D_MODEL), lambda i: (jnp.maximum(i * prev_blocks - 1, 0), c // SEG_BLOCKS))

    full = lambda shape: pl.BlockSpec(shape, lambda i: (0, 0))
    return pl.pallas_call(
        _out_kernel,
        grid=(s // tm,),
        in_specs=[
            pl.BlockSpec((tm, D_MODEL), row), pl.BlockSpec((D_MODEL, tm), col),
            pl.BlockSpec((D_MODEL, tm), col), pl.BlockSpec((D_MODEL, tm), col),
            seg(P_CB), seg(P_CC), seg(P_CX), seg_prev(P_CC), seg_prev(P_CX), seg(P_GA), seg(P_GB),
            full((CONV_K, CONV_WIDTH)), full((D_MODEL, D_MODEL)), full((CONV_WIDTH, D_MODEL)),
            full((D_MODEL, D_MODEL)),
        ],
        out_specs=pl.BlockSpec((tm, D_MODEL), row),
        out_shape=jax.ShapeDtypeStruct((s, D_MODEL), F32),
        scratch_shapes=[pltpu.VMEM((tm + PREV_ROWS, CONV_WIDTH), F32)],
        compiler_params=_params(("parallel",)),
        name="merge_out",
    )(x1, o_cmp_t, o_slc_t, o_win_t, p_arr, p_arr, p_arr, p_arr, p_arr, p_arr, p_arr,
      conv_w, w_nsa_out, w_conv_out, w_out)


def _layer(x, ffn1_norm, ffn1_w_gate, ffn1_w_up, ffn1_w_down, mix_norm, w_in, cmp_pe_k, cmp_pe_v,
           cmp_k_w1, cmp_k_w2, cmp_v_w1, cmp_v_w2, conv_w, w_nsa_out, w_conv_out, w_out,
           ffn2_norm, ffn2_w_gate, ffn2_w_up, ffn2_w_down, final_g, final_norm):
    s = x.shape[0]
    n_cmp = s // CMP_STRIDE
    n_slc = s // SLC_BLOCK
    row = lambda v: v.reshape(1, -1).astype(F32)

    x1 = _ffn(x, row(ffn1_norm), ffn1_w_gate.astype(BF16), ffn1_w_up.astype(BF16),
              ffn1_w_down.astype(BF16), row(final_g), False)
    p_arr, vst, vwt, g3t = _proj(x1, row(mix_norm), _proj_weight(w_in))

    col = lambda c: p_arr[:, c * LANES:(c + 1) * LANES]
    chunks = lambda c: col(c).reshape(n_cmp, CMP_STRIDE * LANES)
    kc = _compress(chunks(P_KC), *_compress_weights(cmp_pe_k, cmp_k_w1, cmp_k_w2, False), False)
    vct = _compress(chunks(P_VC), *_compress_weights(cmp_pe_v, cmp_v_w1, cmp_v_w2, True), True)

    o_cmp_t, sel_t = _cmp_attention(p_arr, kc, vct, _overlap_matrix_t(n_cmp, n_slc), g3t)
    o_slc_t = _selected_attention(p_arr, vst, g3t, sel_t)
    o_win_t = _window_attention(p_arr, vwt, g3t)
    x2 = _out(x1, o_cmp_t, o_slc_t, o_win_t, p_arr, conv_w.astype(F32), w_nsa_out.astype(BF16),
              w_conv_out.astype(BF16), w_out.astype(BF16))
    return _ffn(x2, row(ffn2_norm), ffn2_w_gate.astype(BF16), ffn2_w_up.astype(BF16),
                ffn2_w_down.astype(BF16), row(final_g), final_norm)


def kernel(x, ffn1_norm, ffn1_w_gate, ffn1_w_up, ffn1_w_down, mix_norm, w_in, cmp_pe_k, cmp_pe_v,
           cmp_k_w1, cmp_k_w2, cmp_v_w1, cmp_v_w2, conv_w, w_nsa_out, w_conv_out, w_out, ffn2_norm,
           ffn2_w_gate, ffn2_w_up, ffn2_w_down, final_norm):
    batch, _, _ = x.shape
    depth = ffn1_norm.shape[0]
    outs = []
    for b in range(batch):
        xb = x[b]
        for l in range(depth):
            xb = _layer(xb, ffn1_norm[l], ffn1_w_gate[l], ffn1_w_up[l], ffn1_w_down[l], mix_norm[l],
                        w_in[l], cmp_pe_k[l], cmp_pe_v[l], cmp_k_w1[l], cmp_k_w2[l], cmp_v_w1[l],
                        cmp_v_w2[l], conv_w[l], w_nsa_out[l], w_conv_out[l], w_out[l], ffn2_norm[l],
                        ffn2_w_gate[l], ffn2_w_up[l], ffn2_w_down[l], final_norm, l == depth - 1)
        outs.append(xb)
    return outs[0][None] if batch == 1 else jnp.stack(outs)
```

```python
import functools

import jax
import jax.numpy as jnp
import numpy as np
from jax import lax
from jax.experimental import pallas as pl
from jax.experimental.pallas import tpu as pltpu

D_MODEL = 1024
N_HEADS = 16
HEAD_DIM = 64
N_GROUPS = 2
HEADS_PER_GROUP = N_HEADS // N_GROUPS
N_PAIRS = N_HEADS // 2
PAIRS_PER_GROUP = N_PAIRS // N_GROUPS
CMP_BLOCK = 32
CMP_STRIDE = 16
CMP_HIDDEN = 256
SLC_BLOCK = 64
N_SELECT = 16
WINDOW = 512
CONV_WIDTH = 1024
CONV_K = 3
D_FF = 2816
EPS = 1e-6
NEG = -(2.0 ** 100)
N_FORCED = 3
LOG2E = float(np.log2(np.e))
LANES = 128
BF16_ROWS = 16

P_Q, P_CB, P_CC, P_CX, P_GA, P_GB = 0, 8, 16, 24, 32, 40
P_KC, P_VC, P_KS, P_VS, P_KW, P_VW, P_G3 = 48, 49, 50, 51, 52, 53, 54
P_BLOCKS = 56
P_COLS = P_BLOCKS * LANES
SEG_BLOCKS = D_MODEL // LANES

F32 = jnp.float32
BF16 = jnp.bfloat16

VMEM_LIMIT = 52 * 1024 * 1024


def _dot(a, b):
    return jnp.dot(a, b, preferred_element_type=F32)


def _dot_nt(a, b):
    return lax.dot_general(a, b, (((1,), (1,)), ((), ())), preferred_element_type=F32)


def _sigmoid(x):
    return 1.0 / (1.0 + jnp.exp(-x))


def _rms(x, g):
    return x * lax.rsqrt(jnp.mean(x * x, axis=-1, keepdims=True) + EPS) * g


def _params(sem):
    return pltpu.CompilerParams(dimension_semantics=sem, vmem_limit_bytes=VMEM_LIMIT)


FFN_TM = 512
FFN_CHUNK = 256


def _ffn_kernel(x_ref, g_ref, wg_ref, wu_ref, wd_ref, fg_ref, o_ref, *, final_norm):
    x = x_ref[...]
    h = _rms(x, g_ref[...]).astype(BF16)
    acc = None
    for c in range(D_FF // FFN_CHUNK):
        cols = slice(c * FFN_CHUNK, (c + 1) * FFN_CHUNK)
        gate = _dot(h, wg_ref[:, cols])
        up = _dot(h, wu_ref[:, cols])
        a = (gate * _sigmoid(gate) * up).astype(BF16)
        part = _dot(a, wd_ref[cols, :])
        acc = part if acc is None else acc + part
    y = x + 0.5 * acc
    if final_norm:
        y = _rms(y, fg_ref[...])
    o_ref[...] = y


def _ffn(x, norm_g, wg, wu, wd, final_g, final_norm):
    s = x.shape[0]
    resident = lambda shape: pl.BlockSpec(shape, lambda i: (0, 0), pipeline_mode=pl.Buffered(1))
    return pl.pallas_call(
        functools.partial(_ffn_kernel, final_norm=final_norm),
        grid=(s // FFN_TM,),
        in_specs=[
            pl.BlockSpec((FFN_TM, D_MODEL), lambda i: (i, 0)),
            resident((1, D_MODEL)),
            resident((D_MODEL, D_FF)),
            resident((D_MODEL, D_FF)),
            resident((D_FF, D_MODEL)),
            resident((1, D_MODEL)),
        ],
        out_specs=pl.BlockSpec((FFN_TM, D_MODEL), lambda i: (i, 0)),
        out_shape=jax.ShapeDtypeStruct((s, D_MODEL), F32),
        compiler_params=_params(("parallel",)),
        name="ffn_final" if final_norm else "ffn",
    )(x, norm_g, wg, wu, wd, final_g)


PROJ_TM = 512
PROJ_CHUNK = 512


P_TRANSPOSED = (P_VS, P_VW, P_G3)


def _proj_kernel(x_ref, g_ref, w_ref, o_ref, *t_refs):
    h = _rms(x_ref[...], g_ref[...]).astype(BF16)
    chunk_blocks = PROJ_CHUNK // LANES
    for c in range(P_COLS // PROJ_CHUNK):
        cols = slice(c * PROJ_CHUNK, (c + 1) * PROJ_CHUNK)
        y = _dot(h, w_ref[:, cols])
        o_ref[:, cols] = y.astype(BF16)
        for t_ref, blk in zip(t_refs, P_TRANSPOSED):
            if blk // chunk_blocks == c:
                lo = (blk % chunk_blocks) * LANES
                t_ref[...] = y[:, lo:lo + LANES].T.astype(BF16)


def _proj(x, norm_g, w):
    s = x.shape[0]
    resident = lambda shape: pl.BlockSpec(shape, lambda i: (0, 0), pipeline_mode=pl.Buffered(1))
    return pl.pallas_call(
        _proj_kernel,
        grid=(s // PROJ_TM,),
        in_specs=[
            pl.BlockSpec((PROJ_TM, D_MODEL), lambda i: (i, 0)),
            resident((1, D_MODEL)),
            resident((D_MODEL, P_COLS)),
        ],
        out_specs=[pl.BlockSpec((PROJ_TM, P_COLS), lambda i: (i, 0))]
        + [pl.BlockSpec((LANES, PROJ_TM), lambda i: (0, i)) for _ in P_TRANSPOSED],
        out_shape=[jax.ShapeDtypeStruct((s, P_COLS), BF16)]
        + [jax.ShapeDtypeStruct((LANES, s), BF16) for _ in P_TRANSPOSED],
        compiler_params=_params(("parallel",)),
        name="proj",
    )(x, norm_g, w)


def _proj_weight(w_in):
    q, kc, vc, ks, vs, kw, vw, g3, cb, cc, cx, ga, gb = jnp.split(
        w_in, np.cumsum([1024, 128, 128, 128, 128, 128, 128, 48, 1024, 1024, 1024, 1024]).tolist(), axis=1)
    g3 = jnp.pad(g3, ((0, 0), (0, (P_BLOCKS - P_G3) * LANES - g3.shape[1])))
    w = jnp.concatenate([q * (HEAD_DIM ** -0.5 * LOG2E), cb, cc, cx, ga, gb, kc, vc, ks, vs, kw, vw, g3], axis=1)
    return w.astype(BF16)


def _gelu_tanh(x):
    return 0.5 * x * (1.0 + jnp.tanh(np.sqrt(2.0 / np.pi).astype(np.float32) * (x + 0.044715 * (x * x * x))))


def _compress_kernel(cf_ref, pe_ref, w1_ref, w2_ref, o_ref, *, transposed):
    cf = cf_ref[...].astype(F32)
    top = (cf + pe_ref[0]).astype(BF16)
    bot = (cf + pe_ref[1]).astype(BF16)
    a = _dot(top, w1_ref[0, 0])
    b = _dot(bot, w1_ref[0, 1])
    n = a.shape[0]
    pre = a + pltpu.roll(b, n - 1, 0)
    h = _gelu_tanh(pre).astype(BF16)
    if transposed:
        o_ref[0] = _dot_nt(w2_ref[...], h).astype(BF16)
    else:
        o_ref[0, 0] = _dot(h, w2_ref[0]).astype(BF16)
        o_ref[0, 1] = _dot(h, w2_ref[1]).astype(BF16)


def _compress(cf, pe, w1, w2, transposed):
    n = cf.shape[0]
    width = CMP_STRIDE * LANES
    if transposed:
        w2_spec = pl.BlockSpec((HEAD_DIM, CMP_HIDDEN), lambda g: (0, 0))
        out_spec = pl.BlockSpec((1, HEAD_DIM, n), lambda g: (g, 0, 0))
        out_shape = jax.ShapeDtypeStruct((N_GROUPS, HEAD_DIM, n), BF16)
    else:
        w2_spec = pl.BlockSpec((2, CMP_HIDDEN, LANES), lambda g: (0, 0, 0))
        out_spec = pl.BlockSpec((1, 2, n, LANES), lambda g: (g, 0, 0, 0))
        out_shape = jax.ShapeDtypeStruct((N_GROUPS, 2, n, LANES), BF16)
    return pl.pallas_call(
        functools.partial(_compress_kernel, transposed=transposed),
        grid=(N_GROUPS,),
        in_specs=[
            pl.BlockSpec((n, width), lambda g: (0, 0)),
            pl.BlockSpec((2, 1, width), lambda g: (0, 0, 0)),
            pl.BlockSpec((1, 2, width, CMP_HIDDEN), lambda g: (g, 0, 0, 0)),
            w2_spec,
        ],
        out_specs=out_spec,
        out_shape=out_shape,
        compiler_params=_params(("parallel",)),
        name="compress_v" if transposed else "compress_k",
    )(cf, pe, w1, w2)


def _compress_weights(pe, w1, w2, transposed):
    w1h = w1.reshape(2, CMP_STRIDE, HEAD_DIM, CMP_HIDDEN)
    zeros = jnp.zeros_like(w1h)
    w1g = jnp.stack([jnp.concatenate([w1h, zeros], axis=2), jnp.concatenate([zeros, w1h], axis=2)])
    w1g = w1g.reshape(N_GROUPS, 2, CMP_STRIDE * LANES, CMP_HIDDEN).astype(BF16)
    peh = pe.reshape(2, CMP_STRIDE, HEAD_DIM)
    pe2 = jnp.concatenate([peh, peh], axis=2).reshape(2, 1, CMP_STRIDE * LANES)
    if transposed:
        w2p = w2.T.astype(BF16)
    else:
        z2 = jnp.zeros_like(w2)
        w2p = jnp.stack([jnp.concatenate([w2, z2], axis=1), jnp.concatenate([z2, w2], axis=1)]).astype(BF16)
    return pe2, w1g, w2p


def _split_heads(kt):
    lo = lax.broadcasted_iota(jnp.int32, kt.shape, 1) < HEAD_DIM
    kr = pltpu.roll(kt, HEAD_DIM, 1)
    zero = jnp.zeros_like(kt)
    return [[jnp.where(lo, kt, zero), jnp.where(lo, zero, kr)],
            [jnp.where(lo, kr, zero), jnp.where(lo, zero, kt)]]


QSUB = 256
CMP_TQ = 256
CMP_KEY_CHUNK = 256
CMP_LOOKAHEAD = 4


def _cmp_kernel(q_ref, kc_ref, vct_ref, ovt_ref, g3t_ref, o_ref, sel_ref, *, n_slc):
    tq = q_ref.shape[0]
    n_cmp = kc_ref.shape[2]
    q0 = pl.program_id(0) * tq
    t = q0 + lax.broadcasted_iota(jnp.int32, (1, tq), 1)
    any_valid = t >= CMP_BLOCK - 1
    cur = t // SLC_BLOCK

    def body(n_keys, n_blk):
        c_end = lax.broadcasted_iota(jnp.int32, (n_keys, 1), 0) * CMP_STRIDE + (CMP_BLOCK - 1)
        bias = jnp.where(c_end <= t, 0.0, NEG)
        gates = _sigmoid(g3t_ref[...].astype(F32))

        ovt = ovt_ref[:n_blk, :n_keys]
        ones = jnp.ones((BF16_ROWS, n_keys), BF16)
        lhs = [jnp.concatenate([vct_ref[g, :, :n_keys], ones, ovt], axis=0) for g in range(N_GROUPS)]
        imp_row = HEAD_DIM + BF16_ROWS
        blk = lax.broadcasted_iota(jnp.int32, (n_blk, QSUB), 0)
        blk_f = blk.astype(F32)

        def select(cols, importance):
            forced = (blk == 0) | (blk == cur[:, cols]) | (blk == cur[:, cols] - 1)
            causal = blk * SLC_BLOCK <= t[:, cols]
            for g in range(N_GROUPS):
                work = jnp.where(causal & ~forced, importance[g], -1.0)
                sel = jnp.where(forced, 1.0, 0.0)
                for _ in range(N_SELECT - N_FORCED):
                    m = jnp.max(work, axis=0, keepdims=True)
                    first = jnp.min(jnp.where(work == m, blk_f, float(n_blk)), axis=0, keepdims=True)
                    hit = blk_f == first
                    sel = jnp.where(hit, 1.0, sel)
                    work = jnp.where(hit, -jnp.inf, work)
                sel_ref[g * n_slc:g * n_slc + n_blk, cols] = jnp.where(sel > 0.0, 0.0, NEG)
                if n_blk < n_slc:
                    sel_ref[g * n_slc + n_blk:(g + 1) * n_slc, cols] = jnp.full((n_slc - n_blk, QSUB), NEG, F32)

        units = [(h, slice(qs * QSUB, (qs + 1) * QSUB)) for qs in range(tq // QSUB) for h in range(N_HEADS)]

        def scores(unit):
            h, cols = unit
            p, e = divmod(h, 2)
            g = p // PAIRS_PER_GROUP
            s = _dot_nt(kc_ref[g, e, :n_keys, :], q_ref[cols, p * LANES:(p + 1) * LANES]) + bias[:, cols]
            s = s.astype(BF16)
            return s, jnp.max(s, axis=0, keepdims=True)

        importance = [None] * N_GROUPS
        pending = [scores(u) for u in units[:CMP_LOOKAHEAD]]
        for n, (h, cols) in enumerate(units):
            g = h // HEADS_PER_GROUP
            if n + CMP_LOOKAHEAD < len(units):
                pending.append(scores(units[n + CMP_LOOKAHEAD]))
            s, m = pending.pop(0)
            pr = jnp.exp2(s - m)
            res = _dot(lhs[g], pr)
            inv_l = jnp.where(any_valid[:, cols], 1.0 / res[HEAD_DIM:HEAD_DIM + 1, :], 0.0)
            o_h = res[:HEAD_DIM, :] * (inv_l * gates[3 * h:3 * h + 1, cols])
            o_ref[h * HEAD_DIM:(h + 1) * HEAD_DIM, cols] = o_h.astype(BF16)
            imp_h = res[imp_row:, :] * inv_l
            importance[g] = imp_h if importance[g] is None else importance[g] + imp_h
            if h == N_HEADS - 1:
                select(cols, importance)
                importance = [None] * N_GROUPS

    chunk_tokens = CMP_KEY_CHUNK * CMP_STRIDE
    variant = (q0 + tq - 1) // chunk_tokens
    for v in range(n_cmp // CMP_KEY_CHUNK):
        pl.when(variant == v)(functools.partial(body, (v + 1) * CMP_KEY_CHUNK,
                                                (v + 1) * chunk_tokens // SLC_BLOCK))


def _cmp_attention(p_arr, kc, vct, overlap_t, g3t):
    s = p_arr.shape[0]
    n_cmp = kc.shape[2]
    n_slc = s // SLC_BLOCK
    return pl.pallas_call(
        functools.partial(_cmp_kernel, n_slc=n_slc),
        grid=(s // CMP_TQ,),
        in_specs=[
            pl.BlockSpec((CMP_TQ, D_MODEL), lambda i: (i, P_Q // SEG_BLOCKS)),
            pl.BlockSpec((N_GROUPS, 2, n_cmp, LANES), lambda i: (0, 0, 0, 0)),
            pl.BlockSpec((N_GROUPS, HEAD_DIM, n_cmp), lambda i: (0, 0, 0)),
            pl.BlockSpec((n_slc, n_cmp), lambda i: (0, 0)),
            pl.BlockSpec((LANES, CMP_TQ), lambda i: (0, i)),
        ],
        out_specs=[
            pl.BlockSpec((D_MODEL, CMP_TQ), lambda i: (0, i)),
            pl.BlockSpec((N_GROUPS * n_slc, CMP_TQ), lambda i: (0, i)),
        ],
        out_shape=[
            jax.ShapeDtypeStruct((D_MODEL, s), BF16),
            jax.ShapeDtypeStruct((N_GROUPS * n_slc, s), F32),
        ],
        compiler_params=_params(("parallel",)),
        name="cmp_attention",
    )(p_arr, kc, vct, overlap_t, g3t)


def _overlap_matrix_t(n_cmp, n_slc):
    c0 = np.arange(n_cmp)[None, :] * CMP_STRIDE
    s0 = np.arange(n_slc)[:, None] * SLC_BLOCK
    ov = (c0 < s0 + SLC_BLOCK) & (c0 + CMP_BLOCK > s0)
    return jnp.asarray(ov, dtype=BF16)


SLC_TQ = 1024
SLC_TK = 1024
KSUB = 512
SLC_LOOKAHEAD = 4
ACC_ROWS = HEAD_DIM + BF16_ROWS


def _selected_kernel(qi_ref, ki_ref, fl_ref, q_ref, k_ref, vt_ref, g3t_ref, sel_ref, o_ref, m_ref, acc_ref, *,
                     n_slc):
    step = pl.program_id(0)
    tq = q_ref.shape[0]
    tk = k_ref.shape[0]
    flags = fl_ref[step]

    @pl.when((flags & 1) != 0)
    def _():
        m_ref[...] = jnp.full_like(m_ref, NEG)
        acc_ref[...] = jnp.zeros_like(acc_ref)

    def attend(first_subs):
        vt = vt_ref[...]
        ones = jnp.ones((ACC_ROWS - HEAD_DIM, KSUB), BF16)
        sub_blocks = KSUB // SLC_BLOCK
        k_half, v_aug, bias, units = [], [], [], []
        for ks, first_sub in enumerate(first_subs):
            c0 = first_sub * QSUB
            rows_k = slice(ks * KSUB, (ks + 1) * KSUB)
            t = qi_ref[step] * tq + c0 + lax.broadcasted_iota(jnp.int32, (1, tq - c0), 1)
            key = ki_ref[step] * tk + ks * KSUB + lax.broadcasted_iota(jnp.int32, (KSUB, 1), 0)
            causal_bias = jnp.where(key <= t, 0.0, NEG)
            bias_ks = []
            for g in range(N_GROUPS):
                first_block = pl.multiple_of(
                    g * n_slc + ki_ref[step] * (tk // SLC_BLOCK) + ks * sub_blocks, sub_blocks)
                rows = sel_ref[pl.ds(first_block, sub_blocks), c0:]
                per_key = jnp.broadcast_to(rows[:, None, :], (sub_blocks, SLC_BLOCK, tq - c0)).reshape(KSUB, tq - c0)
                bias_ks.append(jnp.minimum(per_key, causal_bias))
            bias.append(bias_ks)
            k_half.append(_split_heads(k_ref[rows_k, :]))
            v_aug.append([jnp.concatenate([vt[g * HEAD_DIM:(g + 1) * HEAD_DIM, rows_k], ones], axis=0)
                          for g in range(N_GROUPS)])
            units += [(ks, h, qs) for qs in range(first_sub, tq // QSUB) for h in range(N_HEADS)]

        def scores(unit):
            ks, h, qs = unit
            p, e = divmod(h, 2)
            g = p // PAIRS_PER_GROUP
            c0 = first_subs[ks] * QSUB
            q = q_ref[qs * QSUB:(qs + 1) * QSUB, p * LANES:(p + 1) * LANES]
            s = (_dot_nt(k_half[ks][g][e], q) + bias[ks][g][:, qs * QSUB - c0:(qs + 1) * QSUB - c0]).astype(BF16)
            return s, jnp.max(s, axis=0, keepdims=True)

        pending = [scores(u) for u in units[:SLC_LOOKAHEAD]]
        for n, (ks, h, qs) in enumerate(units):
            g = h // HEADS_PER_GROUP
            cols = slice(qs * QSUB, (qs + 1) * QSUB)
            if n + SLC_LOOKAHEAD < len(units):
                pending.append(scores(units[n + SLC_LOOKAHEAD]))
            s, s_max = pending.pop(0)
            m_prev = m_ref[h:h + 1, cols]
            m_new = jnp.maximum(m_prev, s_max.astype(F32))
            alpha = jnp.exp2(m_prev - m_new)
            pr = jnp.exp2(s - m_new.astype(BF16))
            m_ref[h:h + 1, cols] = m_new
            acc_ref[h, :, cols] = acc_ref[h, :, cols] * alpha + _dot(v_aug[ks][g], pr)

    n_ksub = tk // KSUB
    is_last = (flags & 2) != 0

    @pl.when(jnp.logical_not(is_last))
    def _():
        attend([0] * n_ksub)

    @pl.when(is_last)
    def _():
        attend([(tq - tk + ks * KSUB) // QSUB for ks in range(n_ksub)])
        gates = _sigmoid(g3t_ref[...].astype(F32))
        for h in range(N_HEADS):
            acc = acc_ref[h]
            scale = gates[3 * h + 1:3 * h + 2, :] / acc[HEAD_DIM:HEAD_DIM + 1, :]
            o_ref[h * HEAD_DIM:(h + 1) * HEAD_DIM, :] = (acc[:HEAD_DIM, :] * scale).astype(BF16)


def _causal_tile_tables(s, tq, tk):
    qi, ki, fl = [], [], []
    for i in range(s // tq):
        last = (i * tq + tq - 1) // tk
        for k in range(last + 1):
            qi.append(i)
            ki.append(k)
            fl.append((1 if k == 0 else 0) | (2 if k == last else 0))
    return (jnp.asarray(qi, jnp.int32), jnp.asarray(ki, jnp.int32), jnp.asarray(fl, jnp.int32))


def _selected_attention(p_arr, vt, g3t, sel_t):
    s = p_arr.shape[0]
    n_slc = s // SLC_BLOCK
    tq, tk = SLC_TQ, SLC_TK
    assert tq % tk == 0 and tk % KSUB == 0 and KSUB % QSUB == 0, "the kernel's slice skipping needs nested tiles"
    qi, ki, fl = _causal_tile_tables(s, tq, tk)
    grid_spec = pltpu.PrefetchScalarGridSpec(
        num_scalar_prefetch=3,
        grid=(qi.shape[0],),
        in_specs=[
            pl.BlockSpec((tq, D_MODEL), lambda n, qi, ki, fl: (qi[n], P_Q // SEG_BLOCKS)),
            pl.BlockSpec((tk, LANES), lambda n, qi, ki, fl: (ki[n], P_KS)),
            pl.BlockSpec((LANES, tk), lambda n, qi, ki, fl: (0, ki[n])),
            pl.BlockSpec((LANES, tq), lambda n, qi, ki, fl: (0, qi[n])),
            pl.BlockSpec((N_GROUPS * n_slc, tq), lambda n, qi, ki, fl: (0, qi[n])),
        ],
        out_specs=pl.BlockSpec((D_MODEL, tq), lambda n, qi, ki, fl: (0, qi[n])),
        scratch_shapes=[
            pltpu.VMEM((N_HEADS, tq), F32),
            pltpu.VMEM((N_HEADS, ACC_ROWS, tq), F32),
        ],
    )
    return pl.pallas_call(
        functools.partial(_selected_kernel, n_slc=n_slc),
        grid_spec=grid_spec,
        out_shape=jax.ShapeDtypeStruct((D_MODEL, s), BF16),
        compiler_params=_params(("arbitrary",)),
        name="selected_attention",
    )(qi, ki, fl, p_arr, p_arr, vt, g3t, sel_t)


WINDOW_TQ = 1024
WINDOW_BLOCK = 256
WINDOW_LOOKAHEAD = 4


def _window_kernel(q_ref, *refs, n_blocks):
    k_refs, vt_refs = refs[:n_blocks], refs[n_blocks:2 * n_blocks]
    g3t_ref, o_ref = refs[2 * n_blocks:]
    tq = q_ref.shape[0]
    span = WINDOW + QSUB
    k_half = _split_heads(jnp.concatenate([r[...] for r in k_refs], axis=0))
    vt = jnp.concatenate([r[...] for r in vt_refs], axis=1)
    ones = jnp.ones((BF16_ROWS, vt.shape[1]), BF16)
    v_aug = [jnp.concatenate([vt[g * HEAD_DIM:(g + 1) * HEAD_DIM, :], ones], axis=0) for g in range(N_GROUPS)]
    gates = _sigmoid(g3t_ref[...].astype(F32))

    r = lax.broadcasted_iota(jnp.int32, (span, 1), 0)
    c = lax.broadcasted_iota(jnp.int32, (1, QSUB), 1)
    in_band = (r > c) & (r <= c + WINDOW)
    first_key = pl.program_id(0) * tq - WINDOW
    n_sub = tq // QSUB
    bias = [jnp.where(in_band & (first_key + qs * QSUB + r >= 0), 0.0, NEG) for qs in range(n_sub)]

    units = [(h, qs) for qs in range(n_sub) for h in range(N_HEADS)]

    def scores(unit):
        h, qs = unit
        p, e = divmod(h, 2)
        g = p // PAIRS_PER_GROUP
        q = q_ref[qs * QSUB:(qs + 1) * QSUB, p * LANES:(p + 1) * LANES]
        s = (_dot_nt(k_half[g][e][qs * QSUB:qs * QSUB + span, :], q) + bias[qs]).astype(BF16)
        return s, jnp.max(s, axis=0, keepdims=True)

    pending = [scores(u) for u in units[:WINDOW_LOOKAHEAD]]
    for n, (h, qs) in enumerate(units):
        g = h // HEADS_PER_GROUP
        cols = slice(qs * QSUB, (qs + 1) * QSUB)
        if n + WINDOW_LOOKAHEAD < len(units):
            pending.append(scores(units[n + WINDOW_LOOKAHEAD]))
        s, m = pending.pop(0)
        res = _dot(v_aug[g][:, qs * QSUB:qs * QSUB + span], jnp.exp2(s - m))
        scale = gates[3 * h + 2:3 * h + 3, cols] / res[HEAD_DIM:HEAD_DIM + 1, :]
        o_ref[h * HEAD_DIM:(h + 1) * HEAD_DIM, cols] = (res[:HEAD_DIM, :] * scale).astype(BF16)


def _window_attention(p_arr, vt, g3t):
    s = p_arr.shape[0]
    tq = WINDOW_TQ
    n_blocks = (WINDOW + tq) // WINDOW_BLOCK
    back = WINDOW // WINDOW_BLOCK
    per_tile = tq // WINDOW_BLOCK

    def block(j):
        return lambda i: jnp.maximum(i * per_tile - back + j, 0)

    k_specs = [pl.BlockSpec((WINDOW_BLOCK, LANES), lambda i, b=block(j): (b(i), P_KW)) for j in range(n_blocks)]
    v_specs = [pl.BlockSpec((LANES, WINDOW_BLOCK), lambda i, b=block(j): (0, b(i))) for j in range(n_blocks)]
    return pl.pallas_call(
        functools.partial(_window_kernel, n_blocks=n_blocks),
        grid=(s // tq,),
        in_specs=[pl.BlockSpec((tq, D_MODEL), lambda i: (i, P_Q // SEG_BLOCKS))] + k_specs + v_specs
        + [pl.BlockSpec((LANES, tq), lambda i: (0, i))],
        out_specs=pl.BlockSpec((D_MODEL, tq), lambda i: (0, i)),
        out_shape=jax.ShapeDtypeStruct((D_MODEL, s), BF16),
        compiler_params=_params(("parallel",)),
        name="window_attention",
    )(p_arr, *([p_arr] * n_blocks), *([vt] * n_blocks), g3t)


OUT_TM = 512
PREV_ROWS = BF16_ROWS


def _out_kernel(x_ref, oc_ref, os_ref, ow_ref, cb_ref, cc_ref, cx_ref, pc_ref, px_ref, ga_ref, gb_ref,
                cw_ref, wa_ref, wb_ref, wo_ref, o_ref, u_ref):
    tm = x_ref.shape[0]
    prev = pc_ref[...].astype(F32) * px_ref[...].astype(F32)
    u_ref[0:PREV_ROWS, :] = jnp.where(pl.program_id(0) == 0, 0.0, prev)
    u_ref[PREV_ROWS:, :] = cc_ref[...].astype(F32) * cx_ref[...].astype(F32)
    cw = cw_ref[...]
    conv = (cw[0:1, :] * u_ref[pl.ds(PREV_ROWS - 2, tm), :]
            + cw[1:2, :] * u_ref[pl.ds(PREV_ROWS - 1, tm), :]
            + cw[2:3, :] * u_ref[pl.ds(PREV_ROWS, tm), :])
    y_b = _dot((cb_ref[...].astype(F32) * conv).astype(BF16), wb_ref[...])

    o_nsa_t = oc_ref[...].astype(F32) + os_ref[...].astype(F32) + ow_ref[...].astype(F32)
    y_a = _dot(o_nsa_t.T.astype(BF16), wa_ref[...])

    merged = _sigmoid(ga_ref[...].astype(F32)) * y_a + _sigmoid(gb_ref[...].astype(F32)) * y_b
    o_ref[...] = x_ref[...] + _dot(merged.astype(BF16), wo_ref[...])


def _out(x1, o_cmp_t, o_slc_t, o_win_t, p_arr, conv_w, w_nsa_out, w_conv_out, w_out):
    s = x1.shape[0]
    tm = OUT_TM
    row = lambda i: (i, 0)
    col = lambda i: (0, i)
    prev_blocks = tm // PREV_ROWS

    def seg(c):
        return pl.BlockSpec((tm, D_MODEL), lambda i: (i, c // SEG_BLOCKS))

    def seg_prev(c):
        return pl.BlockSpec((PREV_ROWS, D_MODEL), lambda i: (jnp.maximum(i * prev_blocks - 1, 0), c // SEG_BLOCKS))

    full = lambda shape: pl.BlockSpec(shape, lambda i: (0, 0))
    return pl.pallas_call(
        _out_kernel,
        grid=(s // tm,),
        in_specs=[
            pl.BlockSpec((tm, D_MODEL), row), pl.BlockSpec((D_MODEL, tm), col),
            pl.BlockSpec((D_MODEL, tm), col), pl.BlockSpec((D_MODEL, tm), col),
            seg(P_CB), seg(P_CC), seg(P_CX), seg_prev(P_CC), seg_prev(P_CX), seg(P_GA), seg(P_GB),
            full((CONV_K, CONV_WIDTH)), full((D_MODEL, D_MODEL)), full((CONV_WIDTH, D_MODEL)),
            full((D_MODEL, D_MODEL)),
        ],
        out_specs=pl.BlockSpec((tm, D_MODEL), row),
        out_shape=jax.ShapeDtypeStruct((s, D_MODEL), F32),
        scratch_shapes=[pltpu.VMEM((tm + PREV_ROWS, CONV_WIDTH), F32)],
        compiler_params=_params(("parallel",)),
        name="merge_out",
    )(x1, o_cmp_t, o_slc_t, o_win_t, p_arr, p_arr, p_arr, p_arr, p_arr, p_arr, p_arr,
      conv_w, w_nsa_out, w_conv_out, w_out)


def _layer(x, ffn1_norm, ffn1_w_gate, ffn1_w_up, ffn1_w_down, mix_norm, w_in, cmp_pe_k, cmp_pe_v,
           cmp_k_w1, cmp_k_w2, cmp_v_w1, cmp_v_w2, conv_w, w_nsa_out, w_conv_out, w_out,
           ffn2_norm, ffn2_w_gate, ffn2_w_up, ffn2_w_down, final_g, final_norm):
    s = x.shape[0]
    n_cmp = s // CMP_STRIDE
    n_slc = s // SLC_BLOCK
    row = lambda v: v.reshape(1, -1).astype(F32)

    x1 = _ffn(x, row(ffn1_norm), ffn1_w_gate.astype(BF16), ffn1_w_up.astype(BF16),
              ffn1_w_down.astype(BF16), row(final_g), False)
    p_arr, vst, vwt, g3t = _proj(x1, row(mix_norm), _proj_weight(w_in))

    col = lambda c: p_arr[:, c * LANES:(c + 1) * LANES]
    chunks = lambda c: col(c).reshape(n_cmp, CMP_STRIDE * LANES)
    kc = _compress(chunks(P_KC), *_compress_weights(cmp_pe_k, cmp_k_w1, cmp_k_w2, False), False)
    vct = _compress(chunks(P_VC), *_compress_weights(cmp_pe_v, cmp_v_w1, cmp_v_w2, True), True)

    o_cmp_t, sel_t = _cmp_attention(p_arr, kc, vct, _overlap_matrix_t(n_cmp, n_slc), g3t)
    o_slc_t = _selected_attention(p_arr, vst, g3t, sel_t)
    o_win_t = _window_attention(p_arr, vwt, g3t)
    x2 = _out(x1, o_cmp_t, o_slc_t, o_win_t, p_arr, conv_w.astype(F32), w_nsa_out.astype(BF16),
              w_conv_out.astype(BF16), w_out.astype(BF16))
    return _ffn(x2, row(ffn2_norm), ffn2_w_gate.astype(BF16), ffn2_w_up.astype(BF16),
                ffn2_w_down.astype(BF16), row(final_g), final_norm)


def kernel(x, ffn1_norm, ffn1_w_gate, ffn1_w_up, ffn1_w_down, mix_norm, w_in, cmp_pe_k, cmp_pe_v,
           cmp_k_w1, cmp_k_w2, cmp_v_w1, cmp_v_w2, conv_w, w_nsa_out, w_conv_out, w_out, ffn2_norm,
           ffn2_w_gate, ffn2_w_up, ffn2_w_down, final_norm):
    batch, _, _ = x.shape
    depth = ffn1_norm.shape[0]
    outs = []
    for b in range(batch):
        xb = x[b]
        for l in range(depth):
            xb = _layer(xb, ffn1_norm[l], ffn1_w_gate[l], ffn1_w_up[l], ffn1_w_down[l], mix_norm[l],
                        w_in[l], cmp_pe_k[l], cmp_pe_v[l], cmp_k_w1[l], cmp_k_w2[l], cmp_v_w1[l],
                        cmp_v_w2[l], conv_w[l], w_nsa_out[l], w_conv_out[l], w_out[l], ffn2_norm[l],
                        ffn2_w_gate[l], ffn2_w_up[l], ffn2_w_down[l], final_norm, l == depth - 1)
        outs.append(xb)
    return outs[0][None] if batch == 1 else jnp.stack(outs)
```

```python
import functools

import jax
import jax.numpy as jnp
import numpy as np
from jax import lax
from jax.experimental import pallas as pl
from jax.experimental.pallas import tpu as pltpu

D_MODEL = 1024
N_HEADS = 16
HEAD_DIM = 64
N_GROUPS = 2
HEADS_PER_GROUP = N_HEADS // N_GROUPS
N_PAIRS = N_HEADS // 2
PAIRS_PER_GROUP = N_PAIRS // N_GROUPS
CMP_BLOCK = 32
CMP_STRIDE = 16
CMP_HIDDEN = 256
SLC_BLOCK = 64
N_SELECT = 16
WINDOW = 512
CONV_WIDTH = 1024
CONV_K = 3
D_FF = 2816
EPS = 1e-6
NEG = -(2.0 ** 100)
N_FORCED = 3
LOG2E = float(np.log2(np.e))
LANES = 128
BF16_ROWS = 16

P_Q, P_CB, P_CC, P_CX, P_GA, P_GB = 0, 8, 16, 24, 32, 40
P_KC, P_VC, P_KS, P_VS, P_KW, P_VW, P_G3 = 48, 49, 50, 51, 52, 53, 54
P_BLOCKS = 56
P_COLS = P_BLOCKS * LANES
SEG_BLOCKS = D_MODEL // LANES

F32 = jnp.float32
BF16 = jnp.bfloat16

VMEM_LIMIT = 52 * 1024 * 1024


def _dot(a, b):
    return jnp.dot(a, b, preferred_element_type=F32)


def _dot_nt(a, b):
    return lax.dot_general(a, b, (((1,), (1,)), ((), ())), preferred_element_type=F32)


def _sigmoid(x):
    return 1.0 / (1.0 + jnp.exp(-x))


def _rms(x, g):
    return x * lax.rsqrt(jnp.mean(x * x, axis=-1, keepdims=True) + EPS) * g


def _params(sem):
    return pltpu.CompilerParams(dimension_semantics=sem, vmem_limit_bytes=VMEM_LIMIT)


FFN_TM = 512
FFN_CHUNK = 256


def _ffn_kernel(x_ref, g_ref, wg_ref, wu_ref, wd_ref, fg_ref, o_ref, *, final_norm):
    x = x_ref[...]
    h = _rms(x, g_ref[...]).astype(BF16)
    acc = None
    for c in range(D_FF // FFN_CHUNK):
        cols = slice(c * FFN_CHUNK, (c + 1) * FFN_CHUNK)
        gate = _dot(h, wg_ref[:, cols].astype(BF16))
        up = _dot(h, wu_ref[:, cols].astype(BF16))
        a = (gate * _sigmoid(gate) * up).astype(BF16)
        part = _dot(a, wd_ref[cols, :].astype(BF16))
        acc = part if acc is None else acc + part
    y = x + 0.5 * acc
    if final_norm:
        y = _rms(y, fg_ref[...])
    o_ref[...] = y


def _ffn(x, norm_g, wg, wu, wd, final_g, final_norm):
    s = x.shape[0]
    resident = lambda shape: pl.BlockSpec(shape, lambda i: (0, 0), pipeline_mode=pl.Buffered(1))
    return pl.pallas_call(
        functools.partial(_ffn_kernel, final_norm=final_norm),
        grid=(s // FFN_TM,),
        in_specs=[
            pl.BlockSpec((FFN_TM, D_MODEL), lambda i: (i, 0)),
            resident((1, D_MODEL)),
            resident((D_MODEL, D_FF)),
            resident((D_MODEL, D_FF)),
            resident((D_FF, D_MODEL)),
            resident((1, D_MODEL)),
        ],
        out_specs=pl.BlockSpec((FFN_TM, D_MODEL), lambda i: (i, 0)),
        out_shape=jax.ShapeDtypeStruct((s, D_MODEL), F32),
        compiler_params=_params(("parallel",)),
        name="ffn_final" if final_norm else "ffn",
    )(x, norm_g, wg, wu, wd, final_g)


PROJ_TM = 512
PROJ_CHUNK = 512


P_TRANSPOSED = (P_VS, P_VW, P_G3)


def _proj_kernel(x_ref, g_ref, w_ref, o_ref, *t_refs):
    h = _rms(x_ref[...], g_ref[...]).astype(BF16)
    chunk_blocks = PROJ_CHUNK // LANES
    for c in range(P_COLS // PROJ_CHUNK):
        cols = slice(c * PROJ_CHUNK, (c + 1) * PROJ_CHUNK)
        y = _dot(h, w_ref[:, cols])
        o_ref[:, cols] = y.astype(BF16)
        for t_ref, blk in zip(t_refs, P_TRANSPOSED):
            if blk // chunk_blocks == c:
                lo = (blk % chunk_blocks) * LANES
                t_ref[...] = y[:, lo:lo + LANES].T.astype(BF16)


def _proj(x, norm_g, w):
    s = x.shape[0]
    resident = lambda shape: pl.BlockSpec(shape, lambda i: (0, 0), pipeline_mode=pl.Buffered(1))
    return pl.pallas_call(
        _proj_kernel,
        grid=(s // PROJ_TM,),
        in_specs=[
            pl.BlockSpec((PROJ_TM, D_MODEL), lambda i: (i, 0)),
            resident((1, D_MODEL)),
            resident((D_MODEL, P_COLS)),
        ],
        out_specs=[pl.BlockSpec((PROJ_TM, P_COLS), lambda i: (i, 0))]
        + [pl.BlockSpec((LANES, PROJ_TM), lambda i: (0, i)) for _ in P_TRANSPOSED],
        out_shape=[jax.ShapeDtypeStruct((s, P_COLS), BF16)]
        + [jax.ShapeDtypeStruct((LANES, s), BF16) for _ in P_TRANSPOSED],
        compiler_params=_params(("parallel",)),
        name="proj",
    )(x, norm_g, w)


def _proj_weight(w_in):
    q, kc, vc, ks, vs, kw, vw, g3, cb, cc, cx, ga, gb = jnp.split(
        w_in, np.cumsum([1024, 128, 128, 128, 128, 128, 128, 48, 1024, 1024, 1024, 1024]).tolist(), axis=1)
    g3 = jnp.pad(g3, ((0, 0), (0, (P_BLOCKS - P_G3) * LANES - g3.shape[1])))
    w = jnp.concatenate([q * (HEAD_DIM ** -0.5 * LOG2E), cb, cc, cx, ga, gb, kc, vc, ks, vs, kw, vw, g3], axis=1)
    return w.astype(BF16)


def _gelu_tanh(x):
    return 0.5 * x * (1.0 + jnp.tanh(np.sqrt(2.0 / np.pi).astype(np.float32) * (x + 0.044715 * (x * x * x))))


def _compress_kernel(cf_ref, pe_ref, w1_ref, w2_ref, o_ref, *, transposed):
    cf = cf_ref[...].astype(F32)
    top = (cf + pe_ref[0]).astype(BF16)
    bot = (cf + pe_ref[1]).astype(BF16)
    a = _dot(top, w1_ref[0, 0])
    b = _dot(bot, w1_ref[0, 1])
    n = a.shape[0]
    pre = a + pltpu.roll(b, n - 1, 0)
    h = _gelu_tanh(pre).astype(BF16)
    if transposed:
        o_ref[0] = _dot_nt(w2_ref[...], h).astype(BF16)
    else:
        o_ref[0, 0] = _dot(h, w2_ref[0]).astype(BF16)
        o_ref[0, 1] = _dot(h, w2_ref[1]).astype(BF16)


def _compress(cf, pe, w1, w2, transposed):
    n = cf.shape[0]
    width = CMP_STRIDE * LANES
    if transposed:
        w2_spec = pl.BlockSpec((HEAD_DIM, CMP_HIDDEN), lambda g: (0, 0))
        out_spec = pl.BlockSpec((1, HEAD_DIM, n), lambda g: (g, 0, 0))
        out_shape = jax.ShapeDtypeStruct((N_GROUPS, HEAD_DIM, n), BF16)
    else:
        w2_spec = pl.BlockSpec((2, CMP_HIDDEN, LANES), lambda g: (0, 0, 0))
        out_spec = pl.BlockSpec((1, 2, n, LANES), lambda g: (g, 0, 0, 0))
        out_shape = jax.ShapeDtypeStruct((N_GROUPS, 2, n, LANES), BF16)
    return pl.pallas_call(
        functools.partial(_compress_kernel, transposed=transposed),
        grid=(N_GROUPS,),
        in_specs=[
            pl.BlockSpec((n, width), lambda g: (0, 0)),
            pl.BlockSpec((2, 1, width), lambda g: (0, 0, 0)),
            pl.BlockSpec((1, 2, width, CMP_HIDDEN), lambda g: (g, 0, 0, 0)),
            w2_spec,
        ],
        out_specs=out_spec,
        out_shape=out_shape,
        compiler_params=_params(("parallel",)),
        name="compress_v" if transposed else "compress_k",
    )(cf, pe, w1, w2)


def _compress_weights(pe, w1, w2, transposed):
    w1h = w1.reshape(2, CMP_STRIDE, HEAD_DIM, CMP_HIDDEN)
    zeros = jnp.zeros_like(w1h)
    w1g = jnp.stack([jnp.concatenate([w1h, zeros], axis=2), jnp.concatenate([zeros, w1h], axis=2)])
    w1g = w1g.reshape(N_GROUPS, 2, CMP_STRIDE * LANES, CMP_HIDDEN).astype(BF16)
    peh = pe.reshape(2, CMP_STRIDE, HEAD_DIM)
    pe2 = jnp.concatenate([peh, peh], axis=2).reshape(2, 1, CMP_STRIDE * LANES)
    if transposed:
        w2p = w2.T.astype(BF16)
    else:
        z2 = jnp.zeros_like(w2)
        w2p = jnp.stack([jnp.concatenate([w2, z2], axis=1), jnp.concatenate([z2, w2], axis=1)]).astype(BF16)
    return pe2, w1g, w2p


def _split_heads(kt):
    lo = lax.broadcasted_iota(jnp.int32, kt.shape, 1) < HEAD_DIM
    kr = pltpu.roll(kt, HEAD_DIM, 1)
    zero = jnp.zeros_like(kt)
    return [[jnp.where(lo, kt, zero), jnp.where(lo, zero, kr)],
            [jnp.where(lo, kr, zero), jnp.where(lo, zero, kt)]]


QSUB = 256
CMP_TQ = 256
CMP_KEY_CHUNK = 256
CMP_LOOKAHEAD = 4


def _cmp_kernel(q_ref, kc_ref, vct_ref, ovt_ref, g3t_ref, o_ref, sel_ref, *, n_slc):
    tq = q_ref.shape[0]
    n_cmp = kc_ref.shape[2]
    q0 = pl.program_id(0) * tq
    t = q0 + lax.broadcasted_iota(jnp.int32, (1, tq), 1)
    any_valid = t >= CMP_BLOCK - 1
    cur = t // SLC_BLOCK

    def body(n_keys, n_blk):
        c_end = lax.broadcasted_iota(jnp.int32, (n_keys, 1), 0) * CMP_STRIDE + (CMP_BLOCK - 1)
        bias = jnp.where(c_end <= t, 0.0, NEG)
        gates = _sigmoid(g3t_ref[...].astype(F32))

        ovt = ovt_ref[:n_blk, :n_keys]
        ones = jnp.ones((BF16_ROWS, n_keys), BF16)
        lhs = [jnp.concatenate([vct_ref[g, :, :n_keys], ones, ovt], axis=0) for g in range(N_GROUPS)]
        imp_row = HEAD_DIM + BF16_ROWS
        blk = lax.broadcasted_iota(jnp.int32, (n_blk, QSUB), 0)
        blk_f = blk.astype(F32)

        def select(cols, importance):
            forced = (blk == 0) | (blk == cur[:, cols]) | (blk == cur[:, cols] - 1)
            causal = blk * SLC_BLOCK <= t[:, cols]
            for g in range(N_GROUPS):
                work = jnp.where(causal & ~forced, importance[g], -1.0)
                sel = jnp.where(forced, 1.0, 0.0)
                for _ in range(N_SELECT - N_FORCED):
                    m = jnp.max(work, axis=0, keepdims=True)
                    first = jnp.min(jnp.where(work == m, blk_f, float(n_blk)), axis=0, keepdims=True)
                    hit = blk_f == first
                    sel = jnp.where(hit, 1.0, sel)
                    work = jnp.where(hit, -jnp.inf, work)
                sel_ref[g * n_slc:g * n_slc + n_blk, cols] = jnp.where(sel > 0.0, 0.0, NEG)
                if n_blk < n_slc:
                    sel_ref[g * n_slc + n_blk:(g + 1) * n_slc, cols] = jnp.full((n_slc - n_blk, QSUB), NEG, F32)

        units = [(h, slice(qs * QSUB, (qs + 1) * QSUB)) for qs in range(tq // QSUB) for h in range(N_HEADS)]

        def scores(unit):
            h, cols = unit
            p, e = divmod(h, 2)
            g = p // PAIRS_PER_GROUP
            s = _dot_nt(kc_ref[g, e, :n_keys, :], q_ref[cols, p * LANES:(p + 1) * LANES]) + bias[:, cols]
            s = s.astype(BF16)
            return s, jnp.max(s, axis=0, keepdims=True)

        importance = [None] * N_GROUPS
        pending = [scores(u) for u in units[:CMP_LOOKAHEAD]]
        for n, (h, cols) in enumerate(units):
            g = h // HEADS_PER_GROUP
            if n + CMP_LOOKAHEAD < len(units):
                pending.append(scores(units[n + CMP_LOOKAHEAD]))
            s, m = pending.pop(0)
            pr = jnp.exp2(s - m)
            res = _dot(lhs[g], pr)
            inv_l = jnp.where(any_valid[:, cols], 1.0 / res[HEAD_DIM:HEAD_DIM + 1, :], 0.0)
            o_h = res[:HEAD_DIM, :] * (inv_l * gates[3 * h:3 * h + 1, cols])
            o_ref[h * HEAD_DIM:(h + 1) * HEAD_DIM, cols] = o_h.astype(BF16)
            imp_h = res[imp_row:, :] * inv_l
            importance[g] = imp_h if importance[g] is None else importance[g] + imp_h
            if h == N_HEADS - 1:
                select(cols, importance)
                importance = [None] * N_GROUPS

    chunk_tokens = CMP_KEY_CHUNK * CMP_STRIDE
    variant = (q0 + tq - 1) // chunk_tokens
    for v in range(n_cmp // CMP_KEY_CHUNK):
        pl.when(variant == v)(functools.partial(body, (v + 1) * CMP_KEY_CHUNK,
                                                (v + 1) * chunk_tokens // SLC_BLOCK))


def _cmp_attention(p_arr, kc, vct, overlap_t, g3t):
    s = p_arr.shape[0]
    n_cmp = kc.shape[2]
    n_slc = s // SLC_BLOCK
    return pl.pallas_call(
        functools.partial(_cmp_kernel, n_slc=n_slc),
        grid=(s // CMP_TQ,),
        in_specs=[
            pl.BlockSpec((CMP_TQ, D_MODEL), lambda i: (i, P_Q // SEG_BLOCKS)),
            pl.BlockSpec((N_GROUPS, 2, n_cmp, LANES), lambda i: (0, 0, 0, 0)),
            pl.BlockSpec((N_GROUPS, HEAD_DIM, n_cmp), lambda i: (0, 0, 0)),
            pl.BlockSpec((n_slc, n_cmp), lambda i: (0, 0)),
            pl.BlockSpec((LANES, CMP_TQ), lambda i: (0, i)),
        ],
        out_specs=[
            pl.BlockSpec((D_MODEL, CMP_TQ), lambda i: (0, i)),
            pl.BlockSpec((N_GROUPS * n_slc, CMP_TQ), lambda i: (0, i)),
        ],
        out_shape=[
            jax.ShapeDtypeStruct((D_MODEL, s), BF16),
            jax.ShapeDtypeStruct((N_GROUPS * n_slc, s), F32),
        ],
        compiler_params=_params(("parallel",)),
        name="cmp_attention",
    )(p_arr, kc, vct, overlap_t, g3t)


def _overlap_matrix_t(n_cmp, n_slc):
    c0 = np.arange(n_cmp)[None, :] * CMP_STRIDE
    s0 = np.arange(n_slc)[:, None] * SLC_BLOCK
    ov = (c0 < s0 + SLC_BLOCK) & (c0 + CMP_BLOCK > s0)
    return jnp.asarray(ov, dtype=BF16)


SLC_TQ = 1024
SLC_TK = 1024
KSUB = 512
SLC_LOOKAHEAD = 4
ACC_ROWS = HEAD_DIM + BF16_ROWS


def _selected_kernel(qi_ref, ki_ref, fl_ref, q_ref, k_ref, vt_ref, g3t_ref, sel_ref, o_ref, m_ref, acc_ref, *,
                     n_slc):
    step = pl.program_id(0)
    tq = q_ref.shape[0]
    tk = k_ref.shape[0]
    flags = fl_ref[step]

    @pl.when((flags & 1) != 0)
    def _():
        m_ref[...] = jnp.full_like(m_ref, NEG)
        acc_ref[...] = jnp.zeros_like(acc_ref)

    def attend(first_subs):
        vt = vt_ref[...]
        ones = jnp.ones((ACC_ROWS - HEAD_DIM, KSUB), BF16)
        sub_blocks = KSUB // SLC_BLOCK
        k_half, v_aug, bias, units = [], [], [], []
        for ks, first_sub in enumerate(first_subs):
            c0 = first_sub * QSUB
            rows_k = slice(ks * KSUB, (ks + 1) * KSUB)
            t = qi_ref[step] * tq + c0 + lax.broadcasted_iota(jnp.int32, (1, tq - c0), 1)
            key = ki_ref[step] * tk + ks * KSUB + lax.broadcasted_iota(jnp.int32, (KSUB, 1), 0)
            causal_bias = jnp.where(key <= t, 0.0, NEG)
            bias_ks = []
            for g in range(N_GROUPS):
                first_block = pl.multiple_of(
                    g * n_slc + ki_ref[step] * (tk // SLC_BLOCK) + ks * sub_blocks, sub_blocks)
                rows = sel_ref[pl.ds(first_block, sub_blocks), c0:]
                per_key = jnp.broadcast_to(rows[:, None, :], (sub_blocks, SLC_BLOCK, tq - c0)).reshape(KSUB, tq - c0)
                bias_ks.append(jnp.minimum(per_key, causal_bias))
            bias.append(bias_ks)
            k_half.append(_split_heads(k_ref[rows_k, :]))
            v_aug.append([jnp.concatenate([vt[g * HEAD_DIM:(g + 1) * HEAD_DIM, rows_k], ones], axis=0)
                          for g in range(N_GROUPS)])
            units += [(ks, h, qs) for qs in range(first_sub, tq // QSUB) for h in range(N_HEADS)]

        def scores(unit):
            ks, h, qs = unit
            p, e = divmod(h, 2)
            g = p // PAIRS_PER_GROUP
            c0 = first_subs[ks] * QSUB
            q = q_ref[qs * QSUB:(qs + 1) * QSUB, p * LANES:(p + 1) * LANES]
            s = (_dot_nt(k_half[ks][g][e], q) + bias[ks][g][:, qs * QSUB - c0:(qs + 1) * QSUB - c0]).astype(BF16)
            return s, jnp.max(s, axis=0, keepdims=True)

        pending = [scores(u) for u in units[:SLC_LOOKAHEAD]]
        for n, (ks, h, qs) in enumerate(units):
            g = h // HEADS_PER_GROUP
            cols = slice(qs * QSUB, (qs + 1) * QSUB)
            if n + SLC_LOOKAHEAD < len(units):
                pending.append(scores(units[n + SLC_LOOKAHEAD]))
            s, s_max = pending.pop(0)
            m_prev = m_ref[h:h + 1, cols]
            m_new = jnp.maximum(m_prev, s_max.astype(F32))
            alpha = jnp.exp2(m_prev - m_new)
            pr = jnp.exp2(s - m_new.astype(BF16))
            m_ref[h:h + 1, cols] = m_new
            acc_ref[h, :, cols] = acc_ref[h, :, cols] * alpha + _dot(v_aug[ks][g], pr)

    n_ksub = tk // KSUB
    is_last = (flags & 2) != 0

    @pl.when(jnp.logical_not(is_last))
    def _():
        attend([0] * n_ksub)

    @pl.when(is_last)
    def _():
        attend([(tq - tk + ks * KSUB) // QSUB for ks in range(n_ksub)])
        gates = _sigmoid(g3t_ref[...].astype(F32))
        for h in range(N_HEADS):
            acc = acc_ref[h]
            scale = gates[3 * h + 1:3 * h + 2, :] / acc[HEAD_DIM:HEAD_DIM + 1, :]
            o_ref[h * HEAD_DIM:(h + 1) * HEAD_DIM, :] = (acc[:HEAD_DIM, :] * scale).astype(BF16)


def _causal_tile_tables(s, tq, tk):
    qi, ki, fl = [], [], []
    for i in range(s // tq):
        last = (i * tq + tq - 1) // tk
        for k in range(last + 1):
            qi.append(i)
            ki.append(k)
            fl.append((1 if k == 0 else 0) | (2 if k == last else 0))
    return (jnp.asarray(qi, jnp.int32), jnp.asarray(ki, jnp.int32), jnp.asarray(fl, jnp.int32))


def _selected_attention(p_arr, vt, g3t, sel_t):
    s = p_arr.shape[0]
    n_slc = s // SLC_BLOCK
    tq, tk = SLC_TQ, SLC_TK
    assert tq % tk == 0 and tk % KSUB == 0 and KSUB % QSUB == 0, "the kernel's slice skipping needs nested tiles"
    qi, ki, fl = _causal_tile_tables(s, tq, tk)
    grid_spec = pltpu.PrefetchScalarGridSpec(
        num_scalar_prefetch=3,
        grid=(qi.shape[0],),
        in_specs=[
            pl.BlockSpec((tq, D_MODEL), lambda n, qi, ki, fl: (qi[n], P_Q // SEG_BLOCKS)),
            pl.BlockSpec((tk, LANES), lambda n, qi, ki, fl: (ki[n], P_KS)),
            pl.BlockSpec((LANES, tk), lambda n, qi, ki, fl: (0, ki[n])),
            pl.BlockSpec((LANES, tq), lambda n, qi, ki, fl: (0, qi[n])),
            pl.BlockSpec((N_GROUPS * n_slc, tq), lambda n, qi, ki, fl: (0, qi[n])),
        ],
        out_specs=pl.BlockSpec((D_MODEL, tq), lambda n, qi, ki, fl: (0, qi[n])),
        scratch_shapes=[
            pltpu.VMEM((N_HEADS, tq), F32),
            pltpu.VMEM((N_HEADS, ACC_ROWS, tq), F32),
        ],
    )
    return pl.pallas_call(
        functools.partial(_selected_kernel, n_slc=n_slc),
        grid_spec=grid_spec,
        out_shape=jax.ShapeDtypeStruct((D_MODEL, s), BF16),
        compiler_params=_params(("arbitrary",)),
        name="selected_attention",
    )(qi, ki, fl, p_arr, p_arr, vt, g3t, sel_t)


WINDOW_TQ = 1024
WINDOW_BLOCK = 256
WINDOW_LOOKAHEAD = 4


def _window_kernel(q_ref, *refs, n_blocks):
    k_refs, vt_refs = refs[:n_blocks], refs[n_blocks:2 * n_blocks]
    g3t_ref, o_ref = refs[2 * n_blocks:]
    tq = q_ref.shape[0]
    span = WINDOW + QSUB
    k_half = _split_heads(jnp.concatenate([r[...] for r in k_refs], axis=0))
    vt = jnp.concatenate([r[...] for r in vt_refs], axis=1)
    ones = jnp.ones((BF16_ROWS, vt.shape[1]), BF16)
    v_aug = [jnp.concatenate([vt[g * HEAD_DIM:(g + 1) * HEAD_DIM, :], ones], axis=0) for g in range(N_GROUPS)]
    gates = _sigmoid(g3t_ref[...].astype(F32))

    r = lax.broadcasted_iota(jnp.int32, (span, 1), 0)
    c = lax.broadcasted_iota(jnp.int32, (1, QSUB), 1)
    in_band = (r > c) & (r <= c + WINDOW)
    first_key = pl.program_id(0) * tq - WINDOW
    n_sub = tq // QSUB
    bias = [jnp.where(in_band & (first_key + qs * QSUB + r >= 0), 0.0, NEG) for qs in range(n_sub)]

    units = [(h, qs) for qs in range(n_sub) for h in range(N_HEADS)]

    def scores(unit):
        h, qs = unit
        p, e = divmod(h, 2)
        g = p // PAIRS_PER_GROUP
        q = q_ref[qs * QSUB:(qs + 1) * QSUB, p * LANES:(p + 1) * LANES]
        s = (_dot_nt(k_half[g][e][qs * QSUB:qs * QSUB + span, :], q) + bias[qs]).astype(BF16)
        return s, jnp.max(s, axis=0, keepdims=True)

    pending = [scores(u) for u in units[:WINDOW_LOOKAHEAD]]
    for n, (h, qs) in enumerate(units):
        g = h // HEADS_PER_GROUP
        cols = slice(qs * QSUB, (qs + 1) * QSUB)
        if n + WINDOW_LOOKAHEAD < len(units):
            pending.append(scores(units[n + WINDOW_LOOKAHEAD]))
        s, m = pending.pop(0)
        res = _dot(v_aug[g][:, qs * QSUB:qs * QSUB + span], jnp.exp2(s - m))
        scale = gates[3 * h + 2:3 * h + 3, cols] / res[HEAD_DIM:HEAD_DIM + 1, :]
        o_ref[h * HEAD_DIM:(h + 1) * HEAD_DIM, cols] = (res[:HEAD_DIM, :] * scale).astype(BF16)


def _window_attention(p_arr, vt, g3t):
    s = p_arr.shape[0]
    tq = WINDOW_TQ
    n_blocks = (WINDOW + tq) // WINDOW_BLOCK
    back = WINDOW // WINDOW_BLOCK
    per_tile = tq // WINDOW_BLOCK

    def block(j):
        return lambda i: jnp.maximum(i * per_tile - back + j, 0)

    k_specs = [pl.BlockSpec((WINDOW_BLOCK, LANES), lambda i, b=block(j): (b(i), P_KW)) for j in range(n_blocks)]
    v_specs = [pl.BlockSpec((LANES, WINDOW_BLOCK), lambda i, b=block(j): (0, b(i))) for j in range(n_blocks)]
    return pl.pallas_call(
        functools.partial(_window_kernel, n_blocks=n_blocks),
        grid=(s // tq,),
        in_specs=[pl.BlockSpec((tq, D_MODEL), lambda i: (i, P_Q // SEG_BLOCKS))] + k_specs + v_specs
        + [pl.BlockSpec((LANES, tq), lambda i: (0, i))],
        out_specs=pl.BlockSpec((D_MODEL, tq), lambda i: (0, i)),
        out_shape=jax.ShapeDtypeStruct((D_MODEL, s), BF16),
        compiler_params=_params(("parallel",)),
        name="window_attention",
    )(p_arr, *([p_arr] * n_blocks), *([vt] * n_blocks), g3t)


OUT_TM = 512
PREV_ROWS = BF16_ROWS


def _out_kernel(x_ref, oc_ref, os_ref, ow_ref, cb_ref, cc_ref, cx_ref, pc_ref, px_ref, ga_ref, gb_ref,
                cw_ref, wa_ref, wb_ref, wo_ref, o_ref, u_ref):
    tm = x_ref.shape[0]
    prev = pc_ref[...].astype(F32) * px_ref[...].astype(F32)
    u_ref[0:PREV_ROWS, :] = jnp.where(pl.program_id(0) == 0, 0.0, prev)
    u_ref[PREV_ROWS:, :] = cc_ref[...].astype(F32) * cx_ref[...].astype(F32)
    cw = cw_ref[...]
    conv = (cw[0:1, :] * u_ref[pl.ds(PREV_ROWS - 2, tm), :]
            + cw[1:2, :] * u_ref[pl.ds(PREV_ROWS - 1, tm), :]
            + cw[2:3, :] * u_ref[pl.ds(PREV_ROWS, tm), :])
    y_b = _dot((cb_ref[...].astype(F32) * conv).astype(BF16), wb_ref[...])

    o_nsa_t = oc_ref[...].astype(F32) + os_ref[...].astype(F32) + ow_ref[...].astype(F32)
    y_a = _dot(o_nsa_t.T.astype(BF16), wa_ref[...])

    merged = _sigmoid(ga_ref[...].astype(F32)) * y_a + _sigmoid(gb_ref[...].astype(F32)) * y_b
    o_ref[...] = x_ref[...] + _dot(merged.astype(BF16), wo_ref[...])


def _out(x1, o_cmp_t, o_slc_t, o_win_t, p_arr, conv_w, w_nsa_out, w_conv_out, w_out):
    s = x1.shape[0]
    tm = OUT_TM
    row = lambda i: (i, 0)
    col = lambda i: (0, i)
    prev_blocks = tm // PREV_ROWS

    def seg(c):
        return pl.BlockSpec((tm, D_MODEL), lambda i: (i, c // SEG_BLOCKS))

    def seg_prev(c):
        return pl.BlockSpec((PREV_ROWS, D_MODEL), lambda i: (jnp.maximum(i * prev_blocks - 1, 0), c // SEG_BLOCKS))

    full = lambda shape: pl.BlockSpec(shape, lambda i: (0, 0))
    return pl.pallas_call(
        _out_kernel,
        grid=(s // tm,),
        in_specs=[
            pl.BlockSpec((tm, D_MODEL), row), pl.BlockSpec((D_MODEL, tm), col),
            pl.BlockSpec((D_MODEL, tm), col), pl.BlockSpec((D_MODEL, tm), col),
            seg(P_CB), seg(P_CC), seg(P_CX), seg_prev(P_CC), seg_prev(P_CX), seg(P_GA), seg(P_GB),
            full((CONV_K, CONV_WIDTH)), full((D_MODEL, D_MODEL)), full((CONV_WIDTH, D_MODEL)),
            full((D_MODEL, D_MODEL)),
        ],
        out_specs=pl.BlockSpec((tm, D_MODEL), row),
        out_shape=jax.ShapeDtypeStruct((s, D_MODEL), F32),
        scratch_shapes=[pltpu.VMEM((tm + PREV_ROWS, CONV_WIDTH), F32)],
        compiler_params=_params(("parallel",)),
        name="merge_out",
    )(x1, o_cmp_t, o_slc_t, o_win_t, p_arr, p_arr, p_arr, p_arr, p_arr, p_arr, p_arr,
      conv_w, w_nsa_out, w_conv_out, w_out)


def _layer(x, ffn1_norm, ffn1_w_gate, ffn1_w_up, ffn1_w_down, mix_norm, w_in, cmp_pe_k, cmp_pe_v,
           cmp_k_w1, cmp_k_w2, cmp_v_w1, cmp_v_w2, conv_w, w_nsa_out, w_conv_out, w_out,
           ffn2_norm, ffn2_w_gate, ffn2_w_up, ffn2_w_down, final_g, final_norm):
    s = x.shape[0]
    n_cmp = s // CMP_STRIDE
    n_slc = s // SLC_BLOCK
    row = lambda v: v.reshape(1, -1).astype(F32)

    x1 = _ffn(x, row(ffn1_norm), ffn1_w_gate, ffn1_w_up, ffn1_w_down, row(final_g), False)
    p_arr, vst, vwt, g3t = _proj(x1, row(mix_norm), _proj_weight(w_in))

    col = lambda c: p_arr[:, c * LANES:(c + 1) * LANES]
    chunks = lambda c: col(c).reshape(n_cmp, CMP_STRIDE * LANES)
    kc = _compress(chunks(P_KC), *_compress_weights(cmp_pe_k, cmp_k_w1, cmp_k_w2, False), False)
    vct = _compress(chunks(P_VC), *_compress_weights(cmp_pe_v, cmp_v_w1, cmp_v_w2, True), True)

    o_cmp_t, sel_t = _cmp_attention(p_arr, kc, vct, _overlap_matrix_t(n_cmp, n_slc), g3t)
    o_slc_t = _selected_attention(p_arr, vst, g3t, sel_t)
    o_win_t = _window_attention(p_arr, vwt, g3t)
    x2 = _out(x1, o_cmp_t, o_slc_t, o_win_t, p_arr, conv_w.astype(F32), w_nsa_out.astype(BF16),
              w_conv_out.astype(BF16), w_out.astype(BF16))
    return _ffn(x2, row(ffn2_norm), ffn2_w_gate, ffn2_w_up, ffn2_w_down, row(final_g), final_norm)


def kernel(x, ffn1_norm, ffn1_w_gate, ffn1_w_up, ffn1_w_down, mix_norm, w_in, cmp_pe_k, cmp_pe_v,
           cmp_k_w1, cmp_k_w2, cmp_v_w1, cmp_v_w2, conv_w, w_nsa_out, w_conv_out, w_out, ffn2_norm,
           ffn2_w_gate, ffn2_w_up, ffn2_w_down, final_norm):
    batch, _, _ = x.shape
    depth = ffn1_norm.shape[0]
    outs = []
    for b in range(batch):
        xb = x[b]
        for l in range(depth):
            xb = _layer(xb, ffn1_norm[l], ffn1_w_gate[l], ffn1_w_up[l], ffn1_w_down[l], mix_norm[l],
                        w_in[l], cmp_pe_k[l], cmp_pe_v[l], cmp_k_w1[l], cmp_k_w2[l], cmp_v_w1[l],
                        cmp_v_w2[l], conv_w[l], w_nsa_out[l], w_conv_out[l], w_out[l], ffn2_norm[l],
                        ffn2_w_gate[l], ffn2_w_up[l], ffn2_w_down[l], final_norm, l == depth - 1)
        outs.append(xb)
    return outs[0][None] if batch == 1 else jnp.stack(outs)
```

```python
import functools

import jax
import jax.numpy as jnp
import numpy as np
from jax import lax
from jax.experimental import pallas as pl
from jax.experimental.pallas import tpu as pltpu

D_MODEL = 1024
N_HEADS = 16
HEAD_DIM = 64
N_GROUPS = 2
HEADS_PER_GROUP = N_HEADS // N_GROUPS
N_PAIRS = N_HEADS // 2
PAIRS_PER_GROUP = N_PAIRS // N_GROUPS
CMP_BLOCK = 32
CMP_STRIDE = 16
CMP_HIDDEN = 256
SLC_BLOCK = 64
N_SELECT = 16
WINDOW = 512
CONV_WIDTH = 1024
CONV_K = 3
D_FF = 2816
EPS = 1e-6
NEG = -(2.0 ** 100)
N_FORCED = 3
LOG2E = float(np.log2(np.e))
LANES = 128
BF16_ROWS = 16

P_Q, P_CB, P_CC, P_CX, P_GA, P_GB = 0, 8, 16, 24, 32, 40
P_KC, P_VC, P_KS, P_VS, P_KW, P_VW, P_G3 = 48, 49, 50, 51, 52, 53, 54
P_BLOCKS = 56
P_COLS = P_BLOCKS * LANES
SEG_BLOCKS = D_MODEL // LANES

F32 = jnp.float32
BF16 = jnp.bfloat16

VMEM_LIMIT = 52 * 1024 * 1024


def _dot(a, b):
    return jnp.dot(a, b, preferred_element_type=F32)


def _dot_nt(a, b):
    return lax.dot_general(a, b, (((1,), (1,)), ((), ())), preferred_element_type=F32)


def _sigmoid(x):
    return 1.0 / (1.0 + jnp.exp(-x))


def _rms(x, g):
    return x * lax.rsqrt(jnp.mean(x * x, axis=-1, keepdims=True) + EPS) * g


def _params(sem):
    return pltpu.CompilerParams(dimension_semantics=sem, vmem_limit_bytes=VMEM_LIMIT)


FFN_TM = 512
FFN_CHUNK = 256


def _ffn_kernel(x_ref, g_ref, wg_ref, wu_ref, wd_ref, fg_ref, o_ref, *, final_norm):
    x = x_ref[...]
    h = _rms(x, g_ref[...]).astype(BF16)
    acc = None
    for c in range(D_FF // FFN_CHUNK):
        cols = slice(c * FFN_CHUNK, (c + 1) * FFN_CHUNK)
        gate = _dot(h, wg_ref[:, cols].astype(BF16))
        up = _dot(h, wu_ref[:, cols].astype(BF16))
        a = (gate * _sigmoid(gate) * up).astype(BF16)
        part = _dot(a, wd_ref[cols, :].astype(BF16))
        acc = part if acc is None else acc + part
    y = x + 0.5 * acc
    if final_norm:
        y = _rms(y, fg_ref[...])
    o_ref[...] = y


def _ffn(x, norm_g, wg, wu, wd, final_g, final_norm):
    s = x.shape[0]
    resident = lambda shape: pl.BlockSpec(shape, lambda i: (0, 0), pipeline_mode=pl.Buffered(1))
    return pl.pallas_call(
        functools.partial(_ffn_kernel, final_norm=final_norm),
        grid=(s // FFN_TM,),
        in_specs=[
            pl.BlockSpec((FFN_TM, D_MODEL), lambda i: (i, 0)),
            resident((1, D_MODEL)),
            resident((D_MODEL, D_FF)),
            resident((D_MODEL, D_FF)),
            resident((D_FF, D_MODEL)),
            resident((1, D_MODEL)),
        ],
        out_specs=pl.BlockSpec((FFN_TM, D_MODEL), lambda i: (i, 0)),
        out_shape=jax.ShapeDtypeStruct((s, D_MODEL), F32),
        compiler_params=_params(("parallel",)),
        name="ffn_final" if final_norm else "ffn",
    )(x, norm_g, wg, wu, wd, final_g)


PROJ_TM = 512
PROJ_CHUNK = 512


P_TRANSPOSED = (P_VS, P_VW, P_G3)


def _proj_kernel(x_ref, g_ref, w_ref, o_ref, *t_refs):
    h = _rms(x_ref[...], g_ref[...]).astype(BF16)
    chunk_blocks = PROJ_CHUNK // LANES
    for c in range(P_COLS // PROJ_CHUNK):
        cols = slice(c * PROJ_CHUNK, (c + 1) * PROJ_CHUNK)
        y = _dot(h, w_ref[:, cols])
        o_ref[:, cols] = y.astype(BF16)
        for t_ref, blk in zip(t_refs, P_TRANSPOSED):
            if blk // chunk_blocks == c:
                lo = (blk % chunk_blocks) * LANES
                t_ref[...] = y[:, lo:lo + LANES].T.astype(BF16)


def _proj(x, norm_g, w):
    s = x.shape[0]
    resident = lambda shape: pl.BlockSpec(shape, lambda i: (0, 0), pipeline_mode=pl.Buffered(1))
    return pl.pallas_call(
        _proj_kernel,
        grid=(s // PROJ_TM,),
        in_specs=[
            pl.BlockSpec((PROJ_TM, D_MODEL), lambda i: (i, 0)),
            resident((1, D_MODEL)),
            resident((D_MODEL, P_COLS)),
        ],
        out_specs=[pl.BlockSpec((PROJ_TM, P_COLS), lambda i: (i, 0))]
        + [pl.BlockSpec((LANES, PROJ_TM), lambda i: (0, i)) for _ in P_TRANSPOSED],
        out_shape=[jax.ShapeDtypeStruct((s, P_COLS), BF16)]
        + [jax.ShapeDtypeStruct((LANES, s), BF16) for _ in P_TRANSPOSED],
        compiler_params=_params(("parallel",)),
        name="proj",
    )(x, norm_g, w)


W_IN_ROWS = 128


def _w_in_kernel(w_ref, o_ref):
    w = w_ref[...]
    col = lax.broadcasted_iota(jnp.int32, w.shape, 1)
    o_ref[...] = jnp.where(col < N_HEADS * HEAD_DIM, w * (HEAD_DIM ** -0.5 * LOG2E), w).astype(BF16)


def _proj_weight(w_in):
    rows, cols = w_in.shape
    w = pl.pallas_call(
        _w_in_kernel,
        grid=(rows // W_IN_ROWS,),
        in_specs=[pl.BlockSpec((W_IN_ROWS, cols), lambda i: (i, 0))],
        out_specs=pl.BlockSpec((W_IN_ROWS, cols), lambda i: (i, 0)),
        out_shape=jax.ShapeDtypeStruct((rows, cols), BF16),
        compiler_params=_params(("parallel",)),
        name="w_in_cast",
    )(w_in)
    q, kc, vc, ks, vs, kw, vw, g3, cb, cc, cx, ga, gb = jnp.split(
        w, np.cumsum([1024, 128, 128, 128, 128, 128, 128, 48, 1024, 1024, 1024, 1024]).tolist(), axis=1)
    g3 = jnp.pad(g3, ((0, 0), (0, (P_BLOCKS - P_G3) * LANES - g3.shape[1])))
    return jnp.concatenate([q, cb, cc, cx, ga, gb, kc, vc, ks, vs, kw, vw, g3], axis=1)


def _gelu_tanh(x):
    return 0.5 * x * (1.0 + jnp.tanh(np.sqrt(2.0 / np.pi).astype(np.float32) * (x + 0.044715 * (x * x * x))))


def _compress_kernel(cf_ref, pe_ref, w1_ref, w2_ref, o_ref, *, transposed):
    cf = cf_ref[...].astype(F32)
    top = (cf + pe_ref[0]).astype(BF16)
    bot = (cf + pe_ref[1]).astype(BF16)
    a = _dot(top, w1_ref[0, 0])
    b = _dot(bot, w1_ref[0, 1])
    n = a.shape[0]
    pre = a + pltpu.roll(b, n - 1, 0)
    h = _gelu_tanh(pre).astype(BF16)
    if transposed:
        o_ref[0] = _dot_nt(w2_ref[...], h).astype(BF16)
    else:
        o_ref[0, 0] = _dot(h, w2_ref[0]).astype(BF16)
        o_ref[0, 1] = _dot(h, w2_ref[1]).astype(BF16)


def _compress(cf, pe, w1, w2, transposed):
    n = cf.shape[0]
    width = CMP_STRIDE * LANES
    if transposed:
        w2_spec = pl.BlockSpec((HEAD_DIM, CMP_HIDDEN), lambda g: (0, 0))
        out_spec = pl.BlockSpec((1, HEAD_DIM, n), lambda g: (g, 0, 0))
        out_shape = jax.ShapeDtypeStruct((N_GROUPS, HEAD_DIM, n), BF16)
    else:
        w2_spec = pl.BlockSpec((2, CMP_HIDDEN, LANES), lambda g: (0, 0, 0))
        out_spec = pl.BlockSpec((1, 2, n, LANES), lambda g: (g, 0, 0, 0))
        out_shape = jax.ShapeDtypeStruct((N_GROUPS, 2, n, LANES), BF16)
    return pl.pallas_call(
        functools.partial(_compress_kernel, transposed=transposed),
        grid=(N_GROUPS,),
        in_specs=[
            pl.BlockSpec((n, width), lambda g: (0, 0)),
            pl.BlockSpec((2, 1, width), lambda g: (0, 0, 0)),
            pl.BlockSpec((1, 2, width, CMP_HIDDEN), lambda g: (g, 0, 0, 0)),
            w2_spec,
        ],
        out_specs=out_spec,
        out_shape=out_shape,
        compiler_params=_params(("parallel",)),
        name="compress_v" if transposed else "compress_k",
    )(cf, pe, w1, w2)


def _compress_weights(pe, w1, w2, transposed):
    w1h = w1.reshape(2, CMP_STRIDE, HEAD_DIM, CMP_HIDDEN)
    zeros = jnp.zeros_like(w1h)
    w1g = jnp.stack([jnp.concatenate([w1h, zeros], axis=2), jnp.concatenate([zeros, w1h], axis=2)])
    w1g = w1g.reshape(N_GROUPS, 2, CMP_STRIDE * LANES, CMP_HIDDEN).astype(BF16)
    peh = pe.reshape(2, CMP_STRIDE, HEAD_DIM)
    pe2 = jnp.concatenate([peh, peh], axis=2).reshape(2, 1, CMP_STRIDE * LANES)
    if transposed:
        w2p = w2.T.astype(BF16)
    else:
        z2 = jnp.zeros_like(w2)
        w2p = jnp.stack([jnp.concatenate([w2, z2], axis=1), jnp.concatenate([z2, w2], axis=1)]).astype(BF16)
    return pe2, w1g, w2p


def _split_heads(kt):
    lo = lax.broadcasted_iota(jnp.int32, kt.shape, 1) < HEAD_DIM
    kr = pltpu.roll(kt, HEAD_DIM, 1)
    zero = jnp.zeros_like(kt)
    return [[jnp.where(lo, kt, zero), jnp.where(lo, zero, kr)],
            [jnp.where(lo, kr, zero), jnp.where(lo, zero, kt)]]


QSUB = 256
CMP_TQ = 256
CMP_KEY_CHUNK = 256
CMP_LOOKAHEAD = 4


def _cmp_kernel(q_ref, kc_ref, vct_ref, ovt_ref, g3t_ref, o_ref, sel_ref, *, n_slc):
    tq = q_ref.shape[0]
    n_cmp = kc_ref.shape[2]
    q0 = pl.program_id(0) * tq
    t = q0 + lax.broadcasted_iota(jnp.int32, (1, tq), 1)
    any_valid = t >= CMP_BLOCK - 1
    cur = t // SLC_BLOCK

    def body(n_keys, n_blk):
        c_end = lax.broadcasted_iota(jnp.int32, (n_keys, 1), 0) * CMP_STRIDE + (CMP_BLOCK - 1)
        bias = jnp.where(c_end <= t, 0.0, NEG)
        gates = _sigmoid(g3t_ref[...].astype(F32))

        ovt = ovt_ref[:n_blk, :n_keys]
        ones = jnp.ones((BF16_ROWS, n_keys), BF16)
        lhs = [jnp.concatenate([vct_ref[g, :, :n_keys], ones, ovt], axis=0) for g in range(N_GROUPS)]
        imp_row = HEAD_DIM + BF16_ROWS
        blk = lax.broadcasted_iota(jnp.int32, (n_blk, QSUB), 0)
        blk_f = blk.astype(F32)

        def select(cols, importance):
            forced = (blk == 0) | (blk == cur[:, cols]) | (blk == cur[:, cols] - 1)
            causal = blk * SLC_BLOCK <= t[:, cols]
            for g in range(N_GROUPS):
                work = jnp.where(causal & ~forced, importance[g], -1.0)
                sel = jnp.where(forced, 1.0, 0.0)
                for _ in range(N_SELECT - N_FORCED):
                    m = jnp.max(work, axis=0, keepdims=True)
                    first = jnp.min(jnp.where(work == m, blk_f, float(n_blk)), axis=0, keepdims=True)
                    hit = blk_f == first
                    sel = jnp.where(hit, 1.0, sel)
                    work = jnp.where(hit, -jnp.inf, work)
                sel_ref[g * n_slc:g * n_slc + n_blk, cols] = jnp.where(sel > 0.0, 0.0, NEG)
                if n_blk < n_slc:
                    sel_ref[g * n_slc + n_blk:(g + 1) * n_slc, cols] = jnp.full((n_slc - n_blk, QSUB), NEG, F32)

        units = [(h, slice(qs * QSUB, (qs + 1) * QSUB)) for qs in range(tq // QSUB) for h in range(N_HEADS)]

        def scores(unit):
            h, cols = unit
            p, e = divmod(h, 2)
            g = p // PAIRS_PER_GROUP
            s = _dot_nt(kc_ref[g, e, :n_keys, :], q_ref[cols, p * LANES:(p + 1) * LANES]) + bias[:, cols]
            s = s.astype(BF16)
            return s, jnp.max(s, axis=0, keepdims=True)

        importance = [None] * N_GROUPS
        pending = [scores(u) for u in units[:CMP_LOOKAHEAD]]
        for n, (h, cols) in enumerate(units):
            g = h // HEADS_PER_GROUP
            if n + CMP_LOOKAHEAD < len(units):
                pending.append(scores(units[n + CMP_LOOKAHEAD]))
            s, m = pending.pop(0)
            pr = jnp.exp2(s - m)
            res = _dot(lhs[g], pr)
            inv_l = jnp.where(any_valid[:, cols], 1.0 / res[HEAD_DIM:HEAD_DIM + 1, :], 0.0)
            o_h = res[:HEAD_DIM, :] * (inv_l * gates[3 * h:3 * h + 1, cols])
            o_ref[h * HEAD_DIM:(h + 1) * HEAD_DIM, cols] = o_h.astype(BF16)
            imp_h = res[imp_row:, :] * inv_l
            importance[g] = imp_h if importance[g] is None else importance[g] + imp_h
            if h == N_HEADS - 1:
                select(cols, importance)
                importance = [None] * N_GROUPS

    chunk_tokens = CMP_KEY_CHUNK * CMP_STRIDE
    variant = (q0 + tq - 1) // chunk_tokens
    for v in range(n_cmp // CMP_KEY_CHUNK):
        pl.when(variant == v)(functools.partial(body, (v + 1) * CMP_KEY_CHUNK,
                                                (v + 1) * chunk_tokens // SLC_BLOCK))


def _cmp_attention(p_arr, kc, vct, overlap_t, g3t):
    s = p_arr.shape[0]
    n_cmp = kc.shape[2]
    n_slc = s // SLC_BLOCK
    return pl.pallas_call(
        functools.partial(_cmp_kernel, n_slc=n_slc),
        grid=(s // CMP_TQ,),
        in_specs=[
            pl.BlockSpec((CMP_TQ, D_MODEL), lambda i: (i, P_Q // SEG_BLOCKS)),
            pl.BlockSpec((N_GROUPS, 2, n_cmp, LANES), lambda i: (0, 0, 0, 0)),
            pl.BlockSpec((N_GROUPS, HEAD_DIM, n_cmp), lambda i: (0, 0, 0)),
            pl.BlockSpec((n_slc, n_cmp), lambda i: (0, 0)),
            pl.BlockSpec((LANES, CMP_TQ), lambda i: (0, i)),
        ],
        out_specs=[
            pl.BlockSpec((D_MODEL, CMP_TQ), lambda i: (0, i)),
            pl.BlockSpec((N_GROUPS * n_slc, CMP_TQ), lambda i: (0, i)),
        ],
        out_shape=[
            jax.ShapeDtypeStruct((D_MODEL, s), BF16),
            jax.ShapeDtypeStruct((N_GROUPS * n_slc, s), F32),
        ],
        compiler_params=_params(("parallel",)),
        name="cmp_attention",
    )(p_arr, kc, vct, overlap_t, g3t)


def _overlap_matrix_t(n_cmp, n_slc):
    c0 = np.arange(n_cmp)[None, :] * CMP_STRIDE
    s0 = np.arange(n_slc)[:, None] * SLC_BLOCK
    ov = (c0 < s0 + SLC_BLOCK) & (c0 + CMP_BLOCK > s0)
    return jnp.asarray(ov, dtype=BF16)


SLC_TQ = 1024
SLC_TK = 1024
KSUB = 512
SLC_LOOKAHEAD = 4
ACC_ROWS = HEAD_DIM + BF16_ROWS


def _selected_kernel(qi_ref, ki_ref, fl_ref, q_ref, k_ref, vt_ref, g3t_ref, sel_ref, o_ref, m_ref, acc_ref, *,
                     n_slc):
    step = pl.program_id(0)
    tq = q_ref.shape[0]
    tk = k_ref.shape[0]
    flags = fl_ref[step]

    @pl.when((flags & 1) != 0)
    def _():
        m_ref[...] = jnp.full_like(m_ref, NEG)
        acc_ref[...] = jnp.zeros_like(acc_ref)

    def attend(first_subs):
        vt = vt_ref[...]
        ones = jnp.ones((ACC_ROWS - HEAD_DIM, KSUB), BF16)
        sub_blocks = KSUB // SLC_BLOCK
        k_half, v_aug, bias, units = [], [], [], []
        for ks, first_sub in enumerate(first_subs):
            c0 = first_sub * QSUB
            rows_k = slice(ks * KSUB, (ks + 1) * KSUB)
            t = qi_ref[step] * tq + c0 + lax.broadcasted_iota(jnp.int32, (1, tq - c0), 1)
            key = ki_ref[step] * tk + ks * KSUB + lax.broadcasted_iota(jnp.int32, (KSUB, 1), 0)
            causal_bias = jnp.where(key <= t, 0.0, NEG)
            bias_ks = []
            for g in range(N_GROUPS):
                first_block = pl.multiple_of(
                    g * n_slc + ki_ref[step] * (tk // SLC_BLOCK) + ks * sub_blocks, sub_blocks)
                rows = sel_ref[pl.ds(first_block, sub_blocks), c0:]
                per_key = jnp.broadcast_to(rows[:, None, :], (sub_blocks, SLC_BLOCK, tq - c0)).reshape(KSUB, tq - c0)
                bias_ks.append(jnp.minimum(per_key, causal_bias))
            bias.append(bias_ks)
            k_half.append(_split_heads(k_ref[rows_k, :]))
            v_aug.append([jnp.concatenate([vt[g * HEAD_DIM:(g + 1) * HEAD_DIM, rows_k], ones], axis=0)
                          for g in range(N_GROUPS)])
            units += [(ks, h, qs) for qs in range(first_sub, tq // QSUB) for h in range(N_HEADS)]

        def scores(unit):
            ks, h, qs = unit
            p, e = divmod(h, 2)
            g = p // PAIRS_PER_GROUP
            c0 = first_subs[ks] * QSUB
            q = q_ref[qs * QSUB:(qs + 1) * QSUB, p * LANES:(p + 1) * LANES]
            s = (_dot_nt(k_half[ks][g][e], q) + bias[ks][g][:, qs * QSUB - c0:(qs + 1) * QSUB - c0]).astype(BF16)
            return s, jnp.max(s, axis=0, keepdims=True)

        pending = [scores(u) for u in units[:SLC_LOOKAHEAD]]
        for n, (ks, h, qs) in enumerate(units):
            g = h // HEADS_PER_GROUP
            cols = slice(qs * QSUB, (qs + 1) * QSUB)
            if n + SLC_LOOKAHEAD < len(units):
                pending.append(scores(units[n + SLC_LOOKAHEAD]))
            s, s_max = pending.pop(0)
            m_prev = m_ref[h:h + 1, cols]
            m_new = jnp.maximum(m_prev, s_max.astype(F32))
            alpha = jnp.exp2(m_prev - m_new)
            pr = jnp.exp2(s - m_new.astype(BF16))
            m_ref[h:h + 1, cols] = m_new
            acc_ref[h, :, cols] = acc_ref[h, :, cols] * alpha + _dot(v_aug[ks][g], pr)

    n_ksub = tk // KSUB
    is_last = (flags & 2) != 0

    @pl.when(jnp.logical_not(is_last))
    def _():
        attend([0] * n_ksub)

    @pl.when(is_last)
    def _():
        attend([(tq - tk + ks * KSUB) // QSUB for ks in range(n_ksub)])
        gates = _sigmoid(g3t_ref[...].astype(F32))
        for h in range(N_HEADS):
            acc = acc_ref[h]
            scale = gates[3 * h + 1:3 * h + 2, :] / acc[HEAD_DIM:HEAD_DIM + 1, :]
            o_ref[h * HEAD_DIM:(h + 1) * HEAD_DIM, :] = (acc[:HEAD_DIM, :] * scale).astype(BF16)


def _causal_tile_tables(s, tq, tk):
    qi, ki, fl = [], [], []
    for i in range(s // tq):
        last = (i * tq + tq - 1) // tk
        for k in range(last + 1):
            qi.append(i)
            ki.append(k)
            fl.append((1 if k == 0 else 0) | (2 if k == last else 0))
    return (jnp.asarray(qi, jnp.int32), jnp.asarray(ki, jnp.int32), jnp.asarray(fl, jnp.int32))


def _selected_attention(p_arr, vt, g3t, sel_t):
    s = p_arr.shape[0]
    n_slc = s // SLC_BLOCK
    tq, tk = SLC_TQ, SLC_TK
    assert tq % tk == 0 and tk % KSUB == 0 and KSUB % QSUB == 0, "the kernel's slice skipping needs nested tiles"
    qi, ki, fl = _causal_tile_tables(s, tq, tk)
    grid_spec = pltpu.PrefetchScalarGridSpec(
        num_scalar_prefetch=3,
        grid=(qi.shape[0],),
        in_specs=[
            pl.BlockSpec((tq, D_MODEL), lambda n, qi, ki, fl: (qi[n], P_Q // SEG_BLOCKS)),
            pl.BlockSpec((tk, LANES), lambda n, qi, ki, fl: (ki[n], P_KS)),
            pl.BlockSpec((LANES, tk), lambda n, qi, ki, fl: (0, ki[n])),
            pl.BlockSpec((LANES, tq), lambda n, qi, ki, fl: (0, qi[n])),
            pl.BlockSpec((N_GROUPS * n_slc, tq), lambda n, qi, ki, fl: (0, qi[n])),
        ],
        out_specs=pl.BlockSpec((D_MODEL, tq), lambda n, qi, ki, fl: (0, qi[n])),
        scratch_shapes=[
            pltpu.VMEM((N_HEADS, tq), F32),
            pltpu.VMEM((N_HEADS, ACC_ROWS, tq), F32),
        ],
    )
    return pl.pallas_call(
        functools.partial(_selected_kernel, n_slc=n_slc),
        grid_spec=grid_spec,
        out_shape=jax.ShapeDtypeStruct((D_MODEL, s), BF16),
        compiler_params=_params(("arbitrary",)),
        name="selected_attention",
    )(qi, ki, fl, p_arr, p_arr, vt, g3t, sel_t)


WINDOW_TQ = 1024
WINDOW_BLOCK = 256
WINDOW_LOOKAHEAD = 4


def _window_kernel(q_ref, *refs, n_blocks):
    k_refs, vt_refs = refs[:n_blocks], refs[n_blocks:2 * n_blocks]
    g3t_ref, o_ref = refs[2 * n_blocks:]
    tq = q_ref.shape[0]
    span = WINDOW + QSUB
    k_half = _split_heads(jnp.concatenate([r[...] for r in k_refs], axis=0))
    vt = jnp.concatenate([r[...] for r in vt_refs], axis=1)
    ones = jnp.ones((BF16_ROWS, vt.shape[1]), BF16)
    v_aug = [jnp.concatenate([vt[g * HEAD_DIM:(g + 1) * HEAD_DIM, :], ones], axis=0) for g in range(N_GROUPS)]
    gates = _sigmoid(g3t_ref[...].astype(F32))

    r = lax.broadcasted_iota(jnp.int32, (span, 1), 0)
    c = lax.broadcasted_iota(jnp.int32, (1, QSUB), 1)
    in_band = (r > c) & (r <= c + WINDOW)
    first_key = pl.program_id(0) * tq - WINDOW
    n_sub = tq // QSUB
    bias = [jnp.where(in_band & (first_key + qs * QSUB + r >= 0), 0.0, NEG) for qs in range(n_sub)]

    units = [(h, qs) for qs in range(n_sub) for h in range(N_HEADS)]

    def scores(unit):
        h, qs = unit
        p, e = divmod(h, 2)
        g = p // PAIRS_PER_GROUP
        q = q_ref[qs * QSUB:(qs + 1) * QSUB, p * LANES:(p + 1) * LANES]
        s = (_dot_nt(k_half[g][e][qs * QSUB:qs * QSUB + span, :], q) + bias[qs]).astype(BF16)
        return s, jnp.max(s, axis=0, keepdims=True)

    pending = [scores(u) for u in units[:WINDOW_LOOKAHEAD]]
    for n, (h, qs) in enumerate(units):
        g = h // HEADS_PER_GROUP
        cols = slice(qs * QSUB, (qs + 1) * QSUB)
        if n + WINDOW_LOOKAHEAD < len(units):
            pending.append(scores(units[n + WINDOW_LOOKAHEAD]))
        s, m = pending.pop(0)
        res = _dot(v_aug[g][:, qs * QSUB:qs * QSUB + span], jnp.exp2(s - m))
        scale = gates[3 * h + 2:3 * h + 3, cols] / res[HEAD_DIM:HEAD_DIM + 1, :]
        o_ref[h * HEAD_DIM:(h + 1) * HEAD_DIM, cols] = (res[:HEAD_DIM, :] * scale).astype(BF16)


def _window_attention(p_arr, vt, g3t):
    s = p_arr.shape[0]
    tq = WINDOW_TQ
    n_blocks = (WINDOW + tq) // WINDOW_BLOCK
    back = WINDOW // WINDOW_BLOCK
    per_tile = tq // WINDOW_BLOCK

    def block(j):
        return lambda i: jnp.maximum(i * per_tile - back + j, 0)

    k_specs = [pl.BlockSpec((WINDOW_BLOCK, LANES), lambda i, b=block(j): (b(i), P_KW)) for j in range(n_blocks)]
    v_specs = [pl.BlockSpec((LANES, WINDOW_BLOCK), lambda i, b=block(j): (0, b(i))) for j in range(n_blocks)]
    return pl.pallas_call(
        functools.partial(_window_kernel, n_blocks=n_blocks),
        grid=(s // tq,),
        in_specs=[pl.BlockSpec((tq, D_MODEL), lambda i: (i, P_Q // SEG_BLOCKS))] + k_specs + v_specs
        + [pl.BlockSpec((LANES, tq), lambda i: (0, i))],
        out_specs=pl.BlockSpec((D_MODEL, tq), lambda i: (0, i)),
        out_shape=jax.ShapeDtypeStruct((D_MODEL, s), BF16),
        compiler_params=_params(("parallel",)),
        name="window_attention",
    )(p_arr, *([p_arr] * n_blocks), *([vt] * n_blocks), g3t)


OUT_TM = 512
PREV_ROWS = BF16_ROWS


def _out_kernel(x_ref, oc_ref, os_ref, ow_ref, cb_ref, cc_ref, cx_ref, pc_ref, px_ref, ga_ref, gb_ref,
                cw_ref, wa_ref, wb_ref, wo_ref, o_ref, u_ref):
    tm = x_ref.shape[0]
    prev = pc_ref[...].astype(F32) * px_ref[...].astype(F32)
    u_ref[0:PREV_ROWS, :] = jnp.where(pl.program_id(0) == 0, 0.0, prev)
    u_ref[PREV_ROWS:, :] = cc_ref[...].astype(F32) * cx_ref[...].astype(F32)
    cw = cw_ref[...]
    conv = (cw[0:1, :] * u_ref[pl.ds(PREV_ROWS - 2, tm), :]
            + cw[1:2, :] * u_ref[pl.ds(PREV_ROWS - 1, tm), :]
            + cw[2:3, :] * u_ref[pl.ds(PREV_ROWS, tm), :])
    y_b = _dot((cb_ref[...].astype(F32) * conv).astype(BF16), wb_ref[...])

    o_nsa_t = oc_ref[...].astype(F32) + os_ref[...].astype(F32) + ow_ref[...].astype(F32)
    y_a = _dot(o_nsa_t.T.astype(BF16), wa_ref[...])

    merged = _sigmoid(ga_ref[...].astype(F32)) * y_a + _sigmoid(gb_ref[...].astype(F32)) * y_b
    o_ref[...] = x_ref[...] + _dot(merged.astype(BF16), wo_ref[...])


def _out(x1, o_cmp_t, o_slc_t, o_win_t, p_arr, conv_w, w_nsa_out, w_conv_out, w_out):
    s = x1.shape[0]
    tm = OUT_TM
    row = lambda i: (i, 0)
    col = lambda i: (0, i)
    prev_blocks = tm // PREV_ROWS

    def seg(c):
        return pl.BlockSpec((tm, D_MODEL), lambda i: (i, c // SEG_BLOCKS))

    def seg_prev(c):
        return pl.BlockSpec((PREV_ROWS, D_MODEL), lambda i: (jnp.maximum(i * prev_blocks - 1, 0), c // SEG_BLOCKS))

    full = lambda shape: pl.BlockSpec(shape, lambda i: (0, 0))
    return pl.pallas_call(
        _out_kernel,
        grid=(s // tm,),
        in_specs=[
            pl.BlockSpec((tm, D_MODEL), row), pl.BlockSpec((D_MODEL, tm), col),
            pl.BlockSpec((D_MODEL, tm), col), pl.BlockSpec((D_MODEL, tm), col),
            seg(P_CB), seg(P_CC), seg(P_CX), seg_prev(P_CC), seg_prev(P_CX), seg(P_GA), seg(P_GB),
            full((CONV_K, CONV_WIDTH)), full((D_MODEL, D_MODEL)), full((CONV_WIDTH, D_MODEL)),
            full((D_MODEL, D_MODEL)),
        ],
        out_specs=pl.BlockSpec((tm, D_MODEL), row),
        out_shape=jax.ShapeDtypeStruct((s, D_MODEL), F32),
        scratch_shapes=[pltpu.VMEM((tm + PREV_ROWS, CONV_WIDTH), F32)],
        compiler_params=_params(("parallel",)),
        name="merge_out",
    )(x1, o_cmp_t, o_slc_t, o_win_t, p_arr, p_arr, p_arr, p_arr, p_arr, p_arr, p_arr,
      conv_w, w_nsa_out, w_conv_out, w_out)


def _layer(x, ffn1_norm, ffn1_w_gate, ffn1_w_up, ffn1_w_down, mix_norm, w_in, cmp_pe_k, cmp_pe_v,
           cmp_k_w1, cmp_k_w2, cmp_v_w1, cmp_v_w2, conv_w, w_nsa_out, w_conv_out, w_out,
           ffn2_norm, ffn2_w_gate, ffn2_w_up, ffn2_w_down, final_g, final_norm):
    s = x.shape[0]
    n_cmp = s // CMP_STRIDE
    n_slc = s // SLC_BLOCK
    row = lambda v: v.reshape(1, -1).astype(F32)

    x1 = _ffn(x, row(ffn1_norm), ffn1_w_gate, ffn1_w_up, ffn1_w_down, row(final_g), False)
    p_arr, vst, vwt, g3t = _proj(x1, row(mix_norm), _proj_weight(w_in))

    col = lambda c: p_arr[:, c * LANES:(c + 1) * LANES]
    chunks = lambda c: col(c).reshape(n_cmp, CMP_STRIDE * LANES)
    kc = _compress(chunks(P_KC), *_compress_weights(cmp_pe_k, cmp_k_w1, cmp_k_w2, False), False)
    vct = _compress(chunks(P_VC), *_compress_weights(cmp_pe_v, cmp_v_w1, cmp_v_w2, True), True)

    o_cmp_t, sel_t = _cmp_attention(p_arr, kc, vct, _overlap_matrix_t(n_cmp, n_slc), g3t)
    o_slc_t = _selected_attention(p_arr, vst, g3t, sel_t)
    o_win_t = _window_attention(p_arr, vwt, g3t)
    x2 = _out(x1, o_cmp_t, o_slc_t, o_win_t, p_arr, conv_w.astype(F32), w_nsa_out.astype(BF16),
              w_conv_out.astype(BF16), w_out.astype(BF16))
    return _ffn(x2, row(ffn2_norm), ffn2_w_gate, ffn2_w_up, ffn2_w_down, row(final_g), final_norm)


def kernel(x, ffn1_norm, ffn1_w_gate, ffn1_w_up, ffn1_w_down, mix_norm, w_in, cmp_pe_k, cmp_pe_v,
           cmp_k_w1, cmp_k_w2, cmp_v_w1, cmp_v_w2, conv_w, w_nsa_out, w_conv_out, w_out, ffn2_norm,
           ffn2_w_gate, ffn2_w_up, ffn2_w_down, final_norm):
    batch, _, _ = x.shape
    depth = ffn1_norm.shape[0]
    outs = []
    for b in range(batch):
        xb = x[b]
        for l in range(depth):
            xb = _layer(xb, ffn1_norm[l], ffn1_w_gate[l], ffn1_w_up[l], ffn1_w_down[l], mix_norm[l],
                        w_in[l], cmp_pe_k[l], cmp_pe_v[l], cmp_k_w1[l], cmp_k_w2[l], cmp_v_w1[l],
                        cmp_v_w2[l], conv_w[l], w_nsa_out[l], w_conv_out[l], w_out[l], ffn2_norm[l],
                        ffn2_w_gate[l], ffn2_w_up[l], ffn2_w_down[l], final_norm, l == depth - 1)
        outs.append(xb)
    return outs[0][None] if batch == 1 else jnp.stack(outs)
```

```python
import functools

import jax
import jax.numpy as jnp
import numpy as np
from jax import lax
from jax.experimental import pallas as pl
from jax.experimental.pallas import tpu as pltpu

D_MODEL = 1024
N_HEADS = 16
HEAD_DIM = 64
N_GROUPS = 2
HEADS_PER_GROUP = N_HEADS // N_GROUPS
N_PAIRS = N_HEADS // 2
PAIRS_PER_GROUP = N_PAIRS // N_GROUPS
CMP_BLOCK = 32
CMP_STRIDE = 16
CMP_HIDDEN = 256
SLC_BLOCK = 64
N_SELECT = 16
WINDOW = 512
CONV_WIDTH = 1024
CONV_K = 3
D_FF = 2816
EPS = 1e-6
NEG = -(2.0 ** 100)
N_FORCED = 3
LOG2E = float(np.log2(np.e))
LANES = 128
BF16_ROWS = 16

P_Q, P_CB, P_CC, P_CX, P_GA, P_GB = 0, 8, 16, 24, 32, 40
P_KC, P_VC, P_KS, P_VS, P_KW, P_VW, P_G3 = 48, 49, 50, 51, 52, 53, 54
P_BLOCKS = 56
P_COLS = P_BLOCKS * LANES
SEG_BLOCKS = D_MODEL // LANES

F32 = jnp.float32
BF16 = jnp.bfloat16

VMEM_LIMIT = 52 * 1024 * 1024


def _dot(a, b):
    return jnp.dot(a, b, preferred_element_type=F32)


def _dot_nt(a, b):
    return lax.dot_general(a, b, (((1,), (1,)), ((), ())), preferred_element_type=F32)


def _sigmoid(x):
    return 1.0 / (1.0 + jnp.exp(-x))


def _rms(x, g):
    return x * lax.rsqrt(jnp.mean(x * x, axis=-1, keepdims=True) + EPS) * g


def _params(sem):
    return pltpu.CompilerParams(dimension_semantics=sem, vmem_limit_bytes=VMEM_LIMIT)


FFN_TM = 512
FFN_CHUNK = 256


def _ffn_kernel(x_ref, g_ref, wg_ref, wu_ref, wd_ref, fg_ref, o_ref, *, final_norm):
    x = x_ref[...]
    h = _rms(x, g_ref[...]).astype(BF16)
    acc = None
    for c in range(D_FF // FFN_CHUNK):
        cols = slice(c * FFN_CHUNK, (c + 1) * FFN_CHUNK)
        gate = _dot(h, wg_ref[:, cols].astype(BF16))
        up = _dot(h, wu_ref[:, cols].astype(BF16))
        a = (gate * _sigmoid(gate) * up).astype(BF16)
        part = _dot(a, wd_ref[cols, :].astype(BF16))
        acc = part if acc is None else acc + part
    y = x + 0.5 * acc
    if final_norm:
        y = _rms(y, fg_ref[...])
    o_ref[...] = y


def _ffn(x, norm_g, wg, wu, wd, final_g, final_norm):
    s = x.shape[0]
    resident = lambda shape: pl.BlockSpec(shape, lambda i: (0, 0), pipeline_mode=pl.Buffered(1))
    return pl.pallas_call(
        functools.partial(_ffn_kernel, final_norm=final_norm),
        grid=(s // FFN_TM,),
        in_specs=[
            pl.BlockSpec((FFN_TM, D_MODEL), lambda i: (i, 0)),
            resident((1, D_MODEL)),
            resident((D_MODEL, D_FF)),
            resident((D_MODEL, D_FF)),
            resident((D_FF, D_MODEL)),
            resident((1, D_MODEL)),
        ],
        out_specs=pl.BlockSpec((FFN_TM, D_MODEL), lambda i: (i, 0)),
        out_shape=jax.ShapeDtypeStruct((s, D_MODEL), F32),
        compiler_params=_params(("parallel",)),
        name="ffn_final" if final_norm else "ffn",
    )(x, norm_g, wg, wu, wd, final_g)


PROJ_TM = 512
PROJ_CHUNK = 512


P_TRANSPOSED = (P_VS, P_VW, P_G3)


def _proj_kernel(x_ref, g_ref, w_ref, o_ref, *t_refs):
    h = _rms(x_ref[...], g_ref[...]).astype(BF16)
    chunk_blocks = PROJ_CHUNK // LANES
    for c in range(P_COLS // PROJ_CHUNK):
        cols = slice(c * PROJ_CHUNK, (c + 1) * PROJ_CHUNK)
        y = _dot(h, w_ref[:, cols])
        o_ref[:, cols] = y.astype(BF16)
        for t_ref, blk in zip(t_refs, P_TRANSPOSED):
            if blk // chunk_blocks == c:
                lo = (blk % chunk_blocks) * LANES
                t_ref[...] = y[:, lo:lo + LANES].T.astype(BF16)


def _proj(x, norm_g, w):
    s = x.shape[0]
    resident = lambda shape: pl.BlockSpec(shape, lambda i: (0, 0), pipeline_mode=pl.Buffered(1))
    return pl.pallas_call(
        _proj_kernel,
        grid=(s // PROJ_TM,),
        in_specs=[
            pl.BlockSpec((PROJ_TM, D_MODEL), lambda i: (i, 0)),
            resident((1, D_MODEL)),
            resident((D_MODEL, P_COLS)),
        ],
        out_specs=[pl.BlockSpec((PROJ_TM, P_COLS), lambda i: (i, 0))]
        + [pl.BlockSpec((LANES, PROJ_TM), lambda i: (0, i)) for _ in P_TRANSPOSED],
        out_shape=[jax.ShapeDtypeStruct((s, P_COLS), BF16)]
        + [jax.ShapeDtypeStruct((LANES, s), BF16) for _ in P_TRANSPOSED],
        compiler_params=_params(("parallel",)),
        name="proj",
    )(x, norm_g, w)


def _proj_weight(w_in):
    q, kc, vc, ks, vs, kw, vw, g3, cb, cc, cx, ga, gb = jnp.split(
        w_in, np.cumsum([1024, 128, 128, 128, 128, 128, 128, 48, 1024, 1024, 1024, 1024]).tolist(), axis=1)
    g3 = jnp.pad(g3, ((0, 0), (0, (P_BLOCKS - P_G3) * LANES - g3.shape[1])))
    w = jnp.concatenate([q * (HEAD_DIM ** -0.5 * LOG2E), cb, cc, cx, ga, gb, kc, vc, ks, vs, kw, vw, g3], axis=1)
    return w.astype(BF16)


def _gelu_tanh(x):
    return 0.5 * x * (1.0 + jnp.tanh(np.sqrt(2.0 / np.pi).astype(np.float32) * (x + 0.044715 * (x * x * x))))


def _compress_kernel(cf_ref, pe_ref, w1_ref, w2_ref, o_ref, *, transposed):
    cf = cf_ref[...].astype(F32)
    top = (cf + pe_ref[0]).astype(BF16)
    bot = (cf + pe_ref[1]).astype(BF16)
    a = _dot(top, w1_ref[0, 0])
    b = _dot(bot, w1_ref[0, 1])
    n = a.shape[0]
    pre = a + pltpu.roll(b, n - 1, 0)
    h = _gelu_tanh(pre).astype(BF16)
    if transposed:
        o_ref[0] = _dot_nt(w2_ref[...], h).astype(BF16)
    else:
        o_ref[0, 0] = _dot(h, w2_ref[0]).astype(BF16)
        o_ref[0, 1] = _dot(h, w2_ref[1]).astype(BF16)


def _compress(cf, pe, w1, w2, transposed):
    n = cf.shape[0]
    width = CMP_STRIDE * LANES
    if transposed:
        w2_spec = pl.BlockSpec((HEAD_DIM, CMP_HIDDEN), lambda g: (0, 0))
        out_spec = pl.BlockSpec((1, HEAD_DIM, n), lambda g: (g, 0, 0))
        out_shape = jax.ShapeDtypeStruct((N_GROUPS, HEAD_DIM, n), BF16)
    else:
        w2_spec = pl.BlockSpec((2, CMP_HIDDEN, LANES), lambda g: (0, 0, 0))
        out_spec = pl.BlockSpec((1, 2, n, LANES), lambda g: (g, 0, 0, 0))
        out_shape = jax.ShapeDtypeStruct((N_GROUPS, 2, n, LANES), BF16)
    return pl.pallas_call(
        functools.partial(_compress_kernel, transposed=transposed),
        grid=(N_GROUPS,),
        in_specs=[
            pl.BlockSpec((n, width), lambda g: (0, 0)),
            pl.BlockSpec((2, 1, width), lambda g: (0, 0, 0)),
            pl.BlockSpec((1, 2, width, CMP_HIDDEN), lambda g: (g, 0, 0, 0)),
            w2_spec,
        ],
        out_specs=out_spec,
        out_shape=out_shape,
        compiler_params=_params(("parallel",)),
        name="compress_v" if transposed else "compress_k",
    )(cf, pe, w1, w2)


def _compress_weights(pe, w1, w2, transposed):
    w1h = w1.reshape(2, CMP_STRIDE, HEAD_DIM, CMP_HIDDEN)
    zeros = jnp.zeros_like(w1h)
    w1g = jnp.stack([jnp.concatenate([w1h, zeros], axis=2), jnp.concatenate([zeros, w1h], axis=2)])
    w1g = w1g.reshape(N_GROUPS, 2, CMP_STRIDE * LANES, CMP_HIDDEN).astype(BF16)
    peh = pe.reshape(2, CMP_STRIDE, HEAD_DIM)
    pe2 = jnp.concatenate([peh, peh], axis=2).reshape(2, 1, CMP_STRIDE * LANES)
    if transposed:
        w2p = w2.T.astype(BF16)
    else:
        z2 = jnp.zeros_like(w2)
        w2p = jnp.stack([jnp.concatenate([w2, z2], axis=1), jnp.concatenate([z2, w2], axis=1)]).astype(BF16)
    return pe2, w1g, w2p


def _split_heads(kt):
    lo = lax.broadcasted_iota(jnp.int32, kt.shape, 1) < HEAD_DIM
    kr = pltpu.roll(kt, HEAD_DIM, 1)
    zero = jnp.zeros_like(kt)
    return [[jnp.where(lo, kt, zero), jnp.where(lo, zero, kr)],
            [jnp.where(lo, kr, zero), jnp.where(lo, zero, kt)]]


QSUB = 256
CMP_TQ = 256
CMP_KEY_CHUNK = 256
CMP_LOOKAHEAD = 4


def _cmp_kernel(q_ref, kc_ref, vct_ref, ovt_ref, g3t_ref, o_ref, sel_ref, *, n_slc):
    tq = q_ref.shape[0]
    n_cmp = kc_ref.shape[2]
    q0 = pl.program_id(0) * tq
    t = q0 + lax.broadcasted_iota(jnp.int32, (1, tq), 1)
    any_valid = t >= CMP_BLOCK - 1
    cur = t // SLC_BLOCK

    def body(n_keys, n_blk):
        c_end = lax.broadcasted_iota(jnp.int32, (n_keys, 1), 0) * CMP_STRIDE + (CMP_BLOCK - 1)
        bias = jnp.where(c_end <= t, 0.0, NEG)
        gates = _sigmoid(g3t_ref[...].astype(F32))

        ovt = ovt_ref[:n_blk, :n_keys]
        ones = jnp.ones((BF16_ROWS, n_keys), BF16)
        lhs = [jnp.concatenate([vct_ref[g, :, :n_keys], ones, ovt], axis=0) for g in range(N_GROUPS)]
        imp_row = HEAD_DIM + BF16_ROWS
        blk = lax.broadcasted_iota(jnp.int32, (n_blk, QSUB), 0)
        blk_f = blk.astype(F32)

        def select(cols, importance):
            forced = (blk == 0) | (blk == cur[:, cols]) | (blk == cur[:, cols] - 1)
            causal = blk * SLC_BLOCK <= t[:, cols]
            for g in range(N_GROUPS):
                work = jnp.where(causal & ~forced, importance[g], -1.0)
                for _ in range(N_SELECT - N_FORCED):
                    m = jnp.max(work, axis=0, keepdims=True)
                    first = jnp.min(jnp.where(work == m, blk_f, float(n_blk)), axis=0, keepdims=True)
                    work = jnp.where(blk_f == first, -jnp.inf, work)
                sel_ref[g * n_slc:g * n_slc + n_blk, cols] = jnp.where(forced | (work == -jnp.inf), 0.0, NEG)
                if n_blk < n_slc:
                    sel_ref[g * n_slc + n_blk:(g + 1) * n_slc, cols] = jnp.full((n_slc - n_blk, QSUB), NEG, F32)

        units = [(h, slice(qs * QSUB, (qs + 1) * QSUB)) for qs in range(tq // QSUB) for h in range(N_HEADS)]

        def scores(unit):
            h, cols = unit
            p, e = divmod(h, 2)
            g = p // PAIRS_PER_GROUP
            s = _dot_nt(kc_ref[g, e, :n_keys, :], q_ref[cols, p * LANES:(p + 1) * LANES]) + bias[:, cols]
            s = s.astype(BF16)
            return s, jnp.max(s, axis=0, keepdims=True)

        importance = [None] * N_GROUPS
        pending = [scores(u) for u in units[:CMP_LOOKAHEAD]]
        for n, (h, cols) in enumerate(units):
            g = h // HEADS_PER_GROUP
            if n + CMP_LOOKAHEAD < len(units):
                pending.append(scores(units[n + CMP_LOOKAHEAD]))
            s, m = pending.pop(0)
            pr = jnp.exp2(s - m)
            res = _dot(lhs[g], pr)
            inv_l = jnp.where(any_valid[:, cols], 1.0 / res[HEAD_DIM:HEAD_DIM + 1, :], 0.0)
            o_h = res[:HEAD_DIM, :] * (inv_l * gates[3 * h:3 * h + 1, cols])
            o_ref[h * HEAD_DIM:(h + 1) * HEAD_DIM, cols] = o_h.astype(BF16)
            imp_h = res[imp_row:, :] * inv_l
            importance[g] = imp_h if importance[g] is None else importance[g] + imp_h
            if h == N_HEADS - 1:
                select(cols, importance)
                importance = [None] * N_GROUPS

    chunk_tokens = CMP_KEY_CHUNK * CMP_STRIDE
    variant = (q0 + tq - 1) // chunk_tokens
    for v in range(n_cmp // CMP_KEY_CHUNK):
        pl.when(variant == v)(functools.partial(body, (v + 1) * CMP_KEY_CHUNK,
                                                (v + 1) * chunk_tokens // SLC_BLOCK))


def _cmp_attention(p_arr, kc, vct, overlap_t, g3t):
    s = p_arr.shape[0]
    n_cmp = kc.shape[2]
    n_slc = s // SLC_BLOCK
    return pl.pallas_call(
        functools.partial(_cmp_kernel, n_slc=n_slc),
        grid=(s // CMP_TQ,),
        in_specs=[
            pl.BlockSpec((CMP_TQ, D_MODEL), lambda i: (i, P_Q // SEG_BLOCKS)),
            pl.BlockSpec((N_GROUPS, 2, n_cmp, LANES), lambda i: (0, 0, 0, 0)),
            pl.BlockSpec((N_GROUPS, HEAD_DIM, n_cmp), lambda i: (0, 0, 0)),
            pl.BlockSpec((n_slc, n_cmp), lambda i: (0, 0)),
            pl.BlockSpec((LANES, CMP_TQ), lambda i: (0, i)),
        ],
        out_specs=[
            pl.BlockSpec((D_MODEL, CMP_TQ), lambda i: (0, i)),
            pl.BlockSpec((N_GROUPS * n_slc, CMP_TQ), lambda i: (0, i)),
        ],
        out_shape=[
            jax.ShapeDtypeStruct((D_MODEL, s), BF16),
            jax.ShapeDtypeStruct((N_GROUPS * n_slc, s), F32),
        ],
        compiler_params=_params(("parallel",)),
        name="cmp_attention",
    )(p_arr, kc, vct, overlap_t, g3t)


def _overlap_matrix_t(n_cmp, n_slc):
    c0 = np.arange(n_cmp)[None, :] * CMP_STRIDE
    s0 = np.arange(n_slc)[:, None] * SLC_BLOCK
    ov = (c0 < s0 + SLC_BLOCK) & (c0 + CMP_BLOCK > s0)
    return jnp.asarray(ov, dtype=BF16)


SLC_TQ = 1024
SLC_TK = 1024
KSUB = 512
SLC_LOOKAHEAD = 4
ACC_ROWS = HEAD_DIM + BF16_ROWS


def _selected_kernel(qi_ref, ki_ref, fl_ref, q_ref, k_ref, vt_ref, g3t_ref, sel_ref, o_ref, m_ref, acc_ref, *,
                     n_slc):
    step = pl.program_id(0)
    tq = q_ref.shape[0]
    tk = k_ref.shape[0]
    flags = fl_ref[step]

    @pl.when((flags & 1) != 0)
    def _():
        m_ref[...] = jnp.full_like(m_ref, NEG)
        acc_ref[...] = jnp.zeros_like(acc_ref)

    def attend(first_subs, diagonal):
        vt = vt_ref[...]
        ones = jnp.ones((ACC_ROWS - HEAD_DIM, KSUB), BF16)
        sub_blocks = KSUB // SLC_BLOCK
        k_half, v_aug, bias, units = [], [], [], []
        for ks, first_sub in enumerate(first_subs):
            c0 = first_sub * QSUB
            rows_k = slice(ks * KSUB, (ks + 1) * KSUB)
            t = qi_ref[step] * tq + c0 + lax.broadcasted_iota(jnp.int32, (1, tq - c0), 1)
            key = ki_ref[step] * tk + ks * KSUB + lax.broadcasted_iota(jnp.int32, (KSUB, 1), 0)
            causal_bias = jnp.where(key <= t, 0.0, NEG)
            bias_ks = []
            for g in range(N_GROUPS):
                first_block = pl.multiple_of(
                    g * n_slc + ki_ref[step] * (tk // SLC_BLOCK) + ks * sub_blocks, sub_blocks)
                rows = sel_ref[pl.ds(first_block, sub_blocks), c0:]
                per_key = jnp.broadcast_to(rows[:, None, :], (sub_blocks, SLC_BLOCK, tq - c0)).reshape(KSUB, tq - c0)
                bias_ks.append(jnp.minimum(per_key, causal_bias))
            bias.append(bias_ks)
            k_half.append(_split_heads(k_ref[rows_k, :]))
            v_aug.append([jnp.concatenate([vt[g * HEAD_DIM:(g + 1) * HEAD_DIM, rows_k], ones], axis=0)
                          for g in range(N_GROUPS)])
            units += [(ks, h, qs) for qs in range(first_sub, tq // QSUB) for h in range(N_HEADS)]

        def visible_keys(ks, qs):
            if not diagonal:
                return KSUB
            return min(KSUB, (qs + 1) * QSUB - (tq - tk) - ks * KSUB)

        def scores(unit):
            ks, h, qs = unit
            p, e = divmod(h, 2)
            g = p // PAIRS_PER_GROUP
            c0 = first_subs[ks] * QSUB
            n_keys = visible_keys(ks, qs)
            q = q_ref[qs * QSUB:(qs + 1) * QSUB, p * LANES:(p + 1) * LANES]
            s = (_dot_nt(k_half[ks][g][e][:n_keys], q)
                 + bias[ks][g][:n_keys, qs * QSUB - c0:(qs + 1) * QSUB - c0]).astype(BF16)
            return s, jnp.max(s, axis=0, keepdims=True)

        pending = [scores(u) for u in units[:SLC_LOOKAHEAD]]
        for n, (ks, h, qs) in enumerate(units):
            g = h // HEADS_PER_GROUP
            cols = slice(qs * QSUB, (qs + 1) * QSUB)
            if n + SLC_LOOKAHEAD < len(units):
                pending.append(scores(units[n + SLC_LOOKAHEAD]))
            s, s_max = pending.pop(0)
            m_prev = m_ref[h:h + 1, cols]
            m_new = jnp.maximum(m_prev, s_max.astype(F32))
            alpha = jnp.exp2(m_prev - m_new)
            pr = jnp.exp2(s - m_new.astype(BF16))
            m_ref[h:h + 1, cols] = m_new
            acc_ref[h, :, cols] = (acc_ref[h, :, cols] * alpha
                                   + _dot(v_aug[ks][g][:, :visible_keys(ks, qs)], pr))

    n_ksub = tk // KSUB
    is_last = (flags & 2) != 0

    @pl.when(jnp.logical_not(is_last))
    def _():
        attend([0] * n_ksub, diagonal=False)

    @pl.when(is_last)
    def _():
        attend([(tq - tk + ks * KSUB) // QSUB for ks in range(n_ksub)], diagonal=True)
        gates = _sigmoid(g3t_ref[...].astype(F32))
        for h in range(N_HEADS):
            acc = acc_ref[h]
            scale = gates[3 * h + 1:3 * h + 2, :] / acc[HEAD_DIM:HEAD_DIM + 1, :]
            o_ref[h * HEAD_DIM:(h + 1) * HEAD_DIM, :] = (acc[:HEAD_DIM, :] * scale).astype(BF16)


def _causal_tile_tables(s, tq, tk):
    qi, ki, fl = [], [], []
    for i in range(s // tq):
        last = (i * tq + tq - 1) // tk
        for k in range(last + 1):
            qi.append(i)
            ki.append(k)
            fl.append((1 if k == 0 else 0) | (2 if k == last else 0))
    return (jnp.asarray(qi, jnp.int32), jnp.asarray(ki, jnp.int32), jnp.asarray(fl, jnp.int32))


def _selected_attention(p_arr, vt, g3t, sel_t):
    s = p_arr.shape[0]
    n_slc = s // SLC_BLOCK
    tq, tk = SLC_TQ, SLC_TK
    assert tq % tk == 0 and tk % KSUB == 0 and KSUB % QSUB == 0, "the kernel's slice skipping needs nested tiles"
    qi, ki, fl = _causal_tile_tables(s, tq, tk)
    grid_spec = pltpu.PrefetchScalarGridSpec(
        num_scalar_prefetch=3,
        grid=(qi.shape[0],),
        in_specs=[
            pl.BlockSpec((tq, D_MODEL), lambda n, qi, ki, fl: (qi[n], P_Q // SEG_BLOCKS)),
            pl.BlockSpec((tk, LANES), lambda n, qi, ki, fl: (ki[n], P_KS)),
            pl.BlockSpec((LANES, tk), lambda n, qi, ki, fl: (0, ki[n])),
            pl.BlockSpec((LANES, tq), lambda n, qi, ki, fl: (0, qi[n])),
            pl.BlockSpec((N_GROUPS * n_slc, tq), lambda n, qi, ki, fl: (0, qi[n])),
        ],
        out_specs=pl.BlockSpec((D_MODEL, tq), lambda n, qi, ki, fl: (0, qi[n])),
        scratch_shapes=[
            pltpu.VMEM((N_HEADS, tq), F32),
            pltpu.VMEM((N_HEADS, ACC_ROWS, tq), F32),
        ],
    )
    return pl.pallas_call(
        functools.partial(_selected_kernel, n_slc=n_slc),
        grid_spec=grid_spec,
        out_shape=jax.ShapeDtypeStruct((D_MODEL, s), BF16),
        compiler_params=_params(("arbitrary",)),
        name="selected_attention",
    )(qi, ki, fl, p_arr, p_arr, vt, g3t, sel_t)


WINDOW_TQ = 1024
WINDOW_BLOCK = 256
WINDOW_LOOKAHEAD = 4


def _window_kernel(q_ref, *refs, n_blocks):
    k_refs, vt_refs = refs[:n_blocks], refs[n_blocks:2 * n_blocks]
    g3t_ref, o_ref = refs[2 * n_blocks:]
    tq = q_ref.shape[0]
    span = WINDOW + QSUB
    k_half = _split_heads(jnp.concatenate([r[...] for r in k_refs], axis=0))
    vt = jnp.concatenate([r[...] for r in vt_refs], axis=1)
    ones = jnp.ones((BF16_ROWS, vt.shape[1]), BF16)
    v_aug = [jnp.concatenate([vt[g * HEAD_DIM:(g + 1) * HEAD_DIM, :], ones], axis=0) for g in range(N_GROUPS)]
    gates = _sigmoid(g3t_ref[...].astype(F32))

    r = lax.broadcasted_iota(jnp.int32, (span, 1), 0)
    c = lax.broadcasted_iota(jnp.int32, (1, QSUB), 1)
    in_band = (r > c) & (r <= c + WINDOW)
    first_key = pl.program_id(0) * tq - WINDOW
    n_sub = tq // QSUB
    bias = [jnp.where(in_band & (first_key + qs * QSUB + r >= 0), 0.0, NEG) for qs in range(n_sub)]

    units = [(h, qs) for qs in range(n_sub) for h in range(N_HEADS)]

    def scores(unit):
        h, qs = unit
        p, e = divmod(h, 2)
        g = p // PAIRS_PER_GROUP
        q = q_ref[qs * QSUB:(qs + 1) * QSUB, p * LANES:(p + 1) * LANES]
        s = (_dot_nt(k_half[g][e][qs * QSUB:qs * QSUB + span, :], q) + bias[qs]).astype(BF16)
        return s, jnp.max(s, axis=0, keepdims=True)

    pending = [scores(u) for u in units[:WINDOW_LOOKAHEAD]]
    for n, (h, qs) in enumerate(units):
        g = h // HEADS_PER_GROUP
        cols = slice(qs * QSUB, (qs + 1) * QSUB)
        if n + WINDOW_LOOKAHEAD < len(units):
            pending.append(scores(units[n + WINDOW_LOOKAHEAD]))
        s, m = pending.pop(0)
        res = _dot(v_aug[g][:, qs * QSUB:qs * QSUB + span], jnp.exp2(s - m))
        scale = gates[3 * h + 2:3 * h + 3, cols] / res[HEAD_DIM:HEAD_DIM + 1, :]
        o_ref[h * HEAD_DIM:(h + 1) * HEAD_DIM, cols] = (res[:HEAD_DIM, :] * scale).astype(BF16)


def _window_attention(p_arr, vt, g3t):
    s = p_arr.shape[0]
    tq = WINDOW_TQ
    n_blocks = (WINDOW + tq) // WINDOW_BLOCK
    back = WINDOW // WINDOW_BLOCK
    per_tile = tq // WINDOW_BLOCK

    def block(j):
        return lambda i: jnp.maximum(i * per_tile - back + j, 0)

    k_specs = [pl.BlockSpec((WINDOW_BLOCK, LANES), lambda i, b=block(j): (b(i), P_KW)) for j in range(n_blocks)]
    v_specs = [pl.BlockSpec((LANES, WINDOW_BLOCK), lambda i, b=block(j): (0, b(i))) for j in range(n_blocks)]
    return pl.pallas_call(
        functools.partial(_window_kernel, n_blocks=n_blocks),
        grid=(s // tq,),
        in_specs=[pl.BlockSpec((tq, D_MODEL), lambda i: (i, P_Q // SEG_BLOCKS))] + k_specs + v_specs
        + [pl.BlockSpec((LANES, tq), lambda i: (0, i))],
        out_specs=pl.BlockSpec((D_MODEL, tq), lambda i: (0, i)),
        out_shape=jax.ShapeDtypeStruct((D_MODEL, s), BF16),
        compiler_params=_params(("parallel",)),
        name="window_attention",
    )(p_arr, *([p_arr] * n_blocks), *([vt] * n_blocks), g3t)


OUT_TM = 512
PREV_ROWS = BF16_ROWS


def _out_kernel(x_ref, oc_ref, os_ref, ow_ref, cb_ref, cc_ref, cx_ref, pc_ref, px_ref, ga_ref, gb_ref,
                cw_ref, wa_ref, wb_ref, wo_ref, o_ref, u_ref):
    tm = x_ref.shape[0]
    prev = pc_ref[...].astype(F32) * px_ref[...].astype(F32)
    u_ref[0:PREV_ROWS, :] = jnp.where(pl.program_id(0) == 0, 0.0, prev)
    u_ref[PREV_ROWS:, :] = cc_ref[...].astype(F32) * cx_ref[...].astype(F32)
    cw = cw_ref[...]
    conv = (cw[0:1, :] * u_ref[pl.ds(PREV_ROWS - 2, tm), :]
            + cw[1:2, :] * u_ref[pl.ds(PREV_ROWS - 1, tm), :]
            + cw[2:3, :] * u_ref[pl.ds(PREV_ROWS, tm), :])
    y_b = _dot((cb_ref[...].astype(F32) * conv).astype(BF16), wb_ref[...])

    o_nsa_t = oc_ref[...].astype(F32) + os_ref[...].astype(F32) + ow_ref[...].astype(F32)
    y_a = _dot(o_nsa_t.T.astype(BF16), wa_ref[...])

    merged = _sigmoid(ga_ref[...].astype(F32)) * y_a + _sigmoid(gb_ref[...].astype(F32)) * y_b
    o_ref[...] = x_ref[...] + _dot(merged.astype(BF16), wo_ref[...])


def _out(x1, o_cmp_t, o_slc_t, o_win_t, p_arr, conv_w, w_nsa_out, w_conv_out, w_out):
    s = x1.shape[0]
    tm = OUT_TM
    row = lambda i: (i, 0)
    col = lambda i: (0, i)
    prev_blocks = tm // PREV_ROWS

    def seg(c):
        return pl.BlockSpec((tm, D_MODEL), lambda i: (i, c // SEG_BLOCKS))

    def seg_prev(c):
        return pl.BlockSpec((PREV_ROWS, D_MODEL), lambda i: (jnp.maximum(i * prev_blocks - 1, 0), c // SEG_BLOCKS))

    full = lambda shape: pl.BlockSpec(shape, lambda i: (0, 0))
    return pl.pallas_call(
        _out_kernel,
        grid=(s // tm,),
        in_specs=[
            pl.BlockSpec((tm, D_MODEL), row), pl.BlockSpec((D_MODEL, tm), col),
            pl.BlockSpec((D_MODEL, tm), col), pl.BlockSpec((D_MODEL, tm), col),
            seg(P_CB), seg(P_CC), seg(P_CX), seg_prev(P_CC), seg_prev(P_CX), seg(P_GA), seg(P_GB),
            full((CONV_K, CONV_WIDTH)), full((D_MODEL, D_MODEL)), full((CONV_WIDTH, D_MODEL)),
            full((D_MODEL, D_MODEL)),
        ],
        out_specs=pl.BlockSpec((tm, D_MODEL), row),
        out_shape=jax.ShapeDtypeStruct((s, D_MODEL), F32),
        scratch_shapes=[pltpu.VMEM((tm + PREV_ROWS, CONV_WIDTH), F32)],
        compiler_params=_params(("parallel",)),
        name="merge_out",
    )(x1, o_cmp_t, o_slc_t, o_win_t, p_arr, p_arr, p_arr, p_arr, p_arr, p_arr, p_arr,
      conv_w, w_nsa_out, w_conv_out, w_out)


def _layer(x, ffn1_norm, ffn1_w_gate, ffn1_w_up, ffn1_w_down, mix_norm, w_in, cmp_pe_k, cmp_pe_v,
           cmp_k_w1, cmp_k_w2, cmp_v_w1, cmp_v_w2, conv_w, w_nsa_out, w_conv_out, w_out,
           ffn2_norm, ffn2_w_gate, ffn2_w_up, ffn2_w_down, final_g, final_norm):
    s = x.shape[0]
    n_cmp = s // CMP_STRIDE
    n_slc = s // SLC_BLOCK
    row = lambda v: v.reshape(1, -1).astype(F32)

    x1 = _ffn(x, row(ffn1_norm), ffn1_w_gate, ffn1_w_up, ffn1_w_down, row(final_g), False)
    p_arr, vst, vwt, g3t = _proj(x1, row(mix_norm), _proj_weight(w_in))

    col = lambda c: p_arr[:, c * LANES:(c + 1) * LANES]
    chunks = lambda c: col(c).reshape(n_cmp, CMP_STRIDE * LANES)
    kc = _compress(chunks(P_KC), *_compress_weights(cmp_pe_k, cmp_k_w1, cmp_k_w2, False), False)
    vct = _compress(chunks(P_VC), *_compress_weights(cmp_pe_v, cmp_v_w1, cmp_v_w2, True), True)

    o_cmp_t, sel_t = _cmp_attention(p_arr, kc, vct, _overlap_matrix_t(n_cmp, n_slc), g3t)
    o_slc_t = _selected_attention(p_arr, vst, g3t, sel_t)
    o_win_t = _window_attention(p_arr, vwt, g3t)
    x2 = _out(x1, o_cmp_t, o_slc_t, o_win_t, p_arr, conv_w.astype(F32), w_nsa_out.astype(BF16),
              w_conv_out.astype(BF16), w_out.astype(BF16))
    return _ffn(x2, row(ffn2_norm), ffn2_w_gate, ffn2_w_up, ffn2_w_down, row(final_g), final_norm)


def kernel(x, ffn1_norm, ffn1_w_gate, ffn1_w_up, ffn1_w_down, mix_norm, w_in, cmp_pe_k, cmp_pe_v,
           cmp_k_w1, cmp_k_w2, cmp_v_w1, cmp_v_w2, conv_w, w_nsa_out, w_conv_out, w_out, ffn2_norm,
           ffn2_w_gate, ffn2_w_up, ffn2_w_down, final_norm):
    batch, _, _ = x.shape
    depth = ffn1_norm.shape[0]
    outs = []
    for b in range(batch):
        xb = x[b]
        for l in range(depth):
            xb = _layer(xb, ffn1_norm[l], ffn1_w_gate[l], ffn1_w_up[l], ffn1_w_down[l], mix_norm[l],
                        w_in[l], cmp_pe_k[l], cmp_pe_v[l], cmp_k_w1[l], cmp_k_w2[l], cmp_v_w1[l],
                        cmp_v_w2[l], conv_w[l], w_nsa_out[l], w_conv_out[l], w_out[l], ffn2_norm[l],
                        ffn2_w_gate[l], ffn2_w_up[l], ffn2_w_down[l], final_norm, l == depth - 1)
        outs.append(xb)
    return outs[0][None] if batch == 1 else jnp.stack(outs)
```

```python
import functools

import jax
import jax.numpy as jnp
import numpy as np
from jax import lax
from jax.experimental import pallas as pl
from jax.experimental.pallas import tpu as pltpu

D_MODEL = 1024
N_HEADS = 16
HEAD_DIM = 64
N_GROUPS = 2
HEADS_PER_GROUP = N_HEADS // N_GROUPS
N_PAIRS = N_HEADS // 2
PAIRS_PER_GROUP = N_PAIRS // N_GROUPS
CMP_BLOCK = 32
CMP_STRIDE = 16
CMP_HIDDEN = 256
SLC_BLOCK = 64
N_SELECT = 16
WINDOW = 512
CONV_WIDTH = 1024
CONV_K = 3
D_FF = 2816
EPS = 1e-6
NEG = -(2.0 ** 100)
N_FORCED = 3
LOG2E = float(np.log2(np.e))
LANES = 128
BF16_ROWS = 16

P_Q, P_CB, P_CC, P_CX, P_GA, P_GB = 0, 8, 16, 24, 32, 40
P_KC, P_VC, P_KS, P_VS, P_KW, P_VW, P_G3 = 48, 49, 50, 51, 52, 53, 54
P_BLOCKS = 56
P_COLS = P_BLOCKS * LANES
SEG_BLOCKS = D_MODEL // LANES

F32 = jnp.float32
BF16 = jnp.bfloat16

VMEM_LIMIT = 52 * 1024 * 1024


def _dot(a, b):
    return jnp.dot(a, b, preferred_element_type=F32)


def _dot_nt(a, b):
    return lax.dot_general(a, b, (((1,), (1,)), ((), ())), preferred_element_type=F32)


def _sigmoid(x):
    return 1.0 / (1.0 + jnp.exp(-x))


def _rms(x, g):
    return x * lax.rsqrt(jnp.mean(x * x, axis=-1, keepdims=True) + EPS) * g


def _params(sem):
    return pltpu.CompilerParams(dimension_semantics=sem, vmem_limit_bytes=VMEM_LIMIT)


FFN_TM = 512
FFN_CHUNK = 256


def _ffn_kernel(x_ref, g_ref, wg_ref, wu_ref, wd_ref, fg_ref, o_ref, *, final_norm):
    x = x_ref[...]
    h = _rms(x, g_ref[...]).astype(BF16)
    acc = None
    for c in range(D_FF // FFN_CHUNK):
        cols = slice(c * FFN_CHUNK, (c + 1) * FFN_CHUNK)
        gate = _dot(h, wg_ref[:, cols].astype(BF16))
        up = _dot(h, wu_ref[:, cols].astype(BF16))
        a = (gate * _sigmoid(gate) * up).astype(BF16)
        part = _dot(a, wd_ref[cols, :].astype(BF16))
        acc = part if acc is None else acc + part
    y = x + 0.5 * acc
    if final_norm:
        y = _rms(y, fg_ref[...])
    o_ref[...] = y


def _ffn(x, norm_g, wg, wu, wd, final_g, final_norm):
    s = x.shape[0]
    resident = lambda shape: pl.BlockSpec(shape, lambda i: (0, 0), pipeline_mode=pl.Buffered(1))
    return pl.pallas_call(
        functools.partial(_ffn_kernel, final_norm=final_norm),
        grid=(s // FFN_TM,),
        in_specs=[
            pl.BlockSpec((FFN_TM, D_MODEL), lambda i: (i, 0)),
            resident((1, D_MODEL)),
            resident((D_MODEL, D_FF)),
            resident((D_MODEL, D_FF)),
            resident((D_FF, D_MODEL)),
            resident((1, D_MODEL)),
        ],
        out_specs=pl.BlockSpec((FFN_TM, D_MODEL), lambda i: (i, 0)),
        out_shape=jax.ShapeDtypeStruct((s, D_MODEL), F32),
        compiler_params=_params(("parallel",)),
        name="ffn_final" if final_norm else "ffn",
    )(x, norm_g, wg, wu, wd, final_g)


PROJ_TM = 512
PREV_ROWS = 8
PROJ_CHUNK = 512


P_TRANSPOSED = (P_VS, P_VW, P_G3)


def _proj_kernel(x_ref, g_ref, w_ref, cw_ref, o_ref, *rest):
    t_refs, u_ref = rest[:-1], rest[-1]
    tm = x_ref.shape[0]
    chunk_blocks = PROJ_CHUNK // LANES
    seg_chunks = SEG_BLOCKS // chunk_blocks

    @pl.when(pl.program_id(0) == 0)
    def _():
        u_ref[0:PREV_ROWS, :] = jnp.zeros((PREV_ROWS, CONV_WIDTH), F32)

    h = _rms(x_ref[...], g_ref[...]).astype(BF16)

    def chunk(c):
        cols = slice(c * PROJ_CHUNK, (c + 1) * PROJ_CHUNK)
        y = _dot(h, w_ref[:, cols])
        if not P_CB <= c * chunk_blocks < P_CC:
            o_ref[:, cols] = y.astype(BF16)
        for t_ref, blk in zip(t_refs, P_TRANSPOSED):
            if blk // chunk_blocks == c:
                lo = (blk % chunk_blocks) * LANES
                t_ref[...] = y[:, lo:lo + LANES].T.astype(BF16)
        return y

    def segment(first_block):
        c0 = first_block // chunk_blocks
        return jnp.concatenate([chunk(c0 + k) for k in range(seg_chunks)], axis=1)

    u_ref[PREV_ROWS:, :] = segment(P_CC) * segment(P_CX)
    cw = cw_ref[...]
    conv = (cw[0:1, :] * u_ref[pl.ds(PREV_ROWS - 2, tm), :]
            + cw[1:2, :] * u_ref[pl.ds(PREV_ROWS - 1, tm), :]
            + cw[2:3, :] * u_ref[pl.ds(PREV_ROWS, tm), :])
    o_ref[:, P_CB * LANES:P_CC * LANES] = (segment(P_CB) * conv).astype(BF16)
    u_ref[0:PREV_ROWS, :] = u_ref[tm:tm + PREV_ROWS, :]

    for c in range(P_COLS // PROJ_CHUNK):
        if not P_CB <= c * chunk_blocks < P_GA:
            chunk(c)


def _proj(x, norm_g, w, conv_w):
    s = x.shape[0]
    resident = lambda shape: pl.BlockSpec(shape, lambda i: (0, 0), pipeline_mode=pl.Buffered(1))
    return pl.pallas_call(
        _proj_kernel,
        grid=(s // PROJ_TM,),
        in_specs=[
            pl.BlockSpec((PROJ_TM, D_MODEL), lambda i: (i, 0)),
            resident((1, D_MODEL)),
            resident((D_MODEL, P_COLS)),
            resident((CONV_K, CONV_WIDTH)),
        ],
        out_specs=[pl.BlockSpec((PROJ_TM, P_COLS), lambda i: (i, 0))]
        + [pl.BlockSpec((LANES, PROJ_TM), lambda i: (0, i)) for _ in P_TRANSPOSED],
        out_shape=[jax.ShapeDtypeStruct((s, P_COLS), BF16)]
        + [jax.ShapeDtypeStruct((LANES, s), BF16) for _ in P_TRANSPOSED],
        scratch_shapes=[pltpu.VMEM((PROJ_TM + PREV_ROWS, CONV_WIDTH), F32)],
        compiler_params=_params(("arbitrary",)),
        name="proj",
    )(x, norm_g, w, conv_w)


def _proj_weight(w_in):
    q, kc, vc, ks, vs, kw, vw, g3, cb, cc, cx, ga, gb = jnp.split(
        w_in, np.cumsum([1024, 128, 128, 128, 128, 128, 128, 48, 1024, 1024, 1024, 1024]).tolist(), axis=1)
    g3 = jnp.pad(g3, ((0, 0), (0, (P_BLOCKS - P_G3) * LANES - g3.shape[1])))
    w = jnp.concatenate([q * (HEAD_DIM ** -0.5 * LOG2E), cb, cc, cx, ga, gb, kc, vc, ks, vs, kw, vw, g3], axis=1)
    return w.astype(BF16)


def _gelu_tanh(x):
    return 0.5 * x * (1.0 + jnp.tanh(np.sqrt(2.0 / np.pi).astype(np.float32) * (x + 0.044715 * (x * x * x))))


def _compress_kernel(cf_ref, pe_ref, w1_ref, w2_ref, o_ref, *, transposed):
    cf = cf_ref[...].astype(F32)
    top = (cf + pe_ref[0]).astype(BF16)
    bot = (cf + pe_ref[1]).astype(BF16)
    a = _dot(top, w1_ref[0, 0])
    b = _dot(bot, w1_ref[0, 1])
    n = a.shape[0]
    pre = a + pltpu.roll(b, n - 1, 0)
    h = _gelu_tanh(pre).astype(BF16)
    if transposed:
        o_ref[0] = _dot_nt(w2_ref[...], h).astype(BF16)
    else:
        o_ref[0, 0] = _dot(h, w2_ref[0]).astype(BF16)
        o_ref[0, 1] = _dot(h, w2_ref[1]).astype(BF16)


def _compress(cf, pe, w1, w2, transposed):
    n = cf.shape[0]
    width = CMP_STRIDE * LANES
    if transposed:
        w2_spec = pl.BlockSpec((HEAD_DIM, CMP_HIDDEN), lambda g: (0, 0))
        out_spec = pl.BlockSpec((1, HEAD_DIM, n), lambda g: (g, 0, 0))
        out_shape = jax.ShapeDtypeStruct((N_GROUPS, HEAD_DIM, n), BF16)
    else:
        w2_spec = pl.BlockSpec((2, CMP_HIDDEN, LANES), lambda g: (0, 0, 0))
        out_spec = pl.BlockSpec((1, 2, n, LANES), lambda g: (g, 0, 0, 0))
        out_shape = jax.ShapeDtypeStruct((N_GROUPS, 2, n, LANES), BF16)
    return pl.pallas_call(
        functools.partial(_compress_kernel, transposed=transposed),
        grid=(N_GROUPS,),
        in_specs=[
            pl.BlockSpec((n, width), lambda g: (0, 0)),
            pl.BlockSpec((2, 1, width), lambda g: (0, 0, 0)),
            pl.BlockSpec((1, 2, width, CMP_HIDDEN), lambda g: (g, 0, 0, 0)),
            w2_spec,
        ],
        out_specs=out_spec,
        out_shape=out_shape,
        compiler_params=_params(("parallel",)),
        name="compress_v" if transposed else "compress_k",
    )(cf, pe, w1, w2)


def _compress_weights(pe, w1, w2, transposed):
    w1h = w1.reshape(2, CMP_STRIDE, HEAD_DIM, CMP_HIDDEN)
    zeros = jnp.zeros_like(w1h)
    w1g = jnp.stack([jnp.concatenate([w1h, zeros], axis=2), jnp.concatenate([zeros, w1h], axis=2)])
    w1g = w1g.reshape(N_GROUPS, 2, CMP_STRIDE * LANES, CMP_HIDDEN).astype(BF16)
    peh = pe.reshape(2, CMP_STRIDE, HEAD_DIM)
    pe2 = jnp.concatenate([peh, peh], axis=2).reshape(2, 1, CMP_STRIDE * LANES)
    if transposed:
        w2p = w2.T.astype(BF16)
    else:
        z2 = jnp.zeros_like(w2)
        w2p = jnp.stack([jnp.concatenate([w2, z2], axis=1), jnp.concatenate([z2, w2], axis=1)]).astype(BF16)
    return pe2, w1g, w2p


def _split_heads(kt):
    lo = lax.broadcasted_iota(jnp.int32, kt.shape, 1) < HEAD_DIM
    kr = pltpu.roll(kt, HEAD_DIM, 1)
    zero = jnp.zeros_like(kt)
    return [[jnp.where(lo, kt, zero), jnp.where(lo, zero, kr)],
            [jnp.where(lo, kr, zero), jnp.where(lo, zero, kt)]]


QSUB = 256
CMP_TQ = 256
CMP_KEY_CHUNK = 256
CMP_LOOKAHEAD = 4


def _cmp_kernel(q_ref, kc_ref, vct_ref, ovt_ref, g3t_ref, o_ref, sel_ref, *, n_slc):
    tq = q_ref.shape[0]
    n_cmp = kc_ref.shape[2]
    q0 = pl.program_id(0) * tq
    t = q0 + lax.broadcasted_iota(jnp.int32, (1, tq), 1)
    any_valid = t >= CMP_BLOCK - 1
    cur = t // SLC_BLOCK

    def body(n_keys, n_blk):
        c_end = lax.broadcasted_iota(jnp.int32, (n_keys, 1), 0) * CMP_STRIDE + (CMP_BLOCK - 1)
        bias = jnp.where(c_end <= t, 0.0, NEG)
        gates = _sigmoid(g3t_ref[...].astype(F32))

        ovt = ovt_ref[:n_blk, :n_keys]
        ones = jnp.ones((BF16_ROWS, n_keys), BF16)
        lhs = [jnp.concatenate([vct_ref[g, :, :n_keys], ones, ovt], axis=0) for g in range(N_GROUPS)]
        imp_row = HEAD_DIM + BF16_ROWS
        blk = lax.broadcasted_iota(jnp.int32, (n_blk, QSUB), 0)
        blk_f = blk.astype(F32)

        def select(cols, importance):
            forced = (blk == 0) | (blk == cur[:, cols]) | (blk == cur[:, cols] - 1)
            causal = blk * SLC_BLOCK <= t[:, cols]
            for g in range(N_GROUPS):
                work = jnp.where(causal & ~forced, importance[g], -1.0)
                for _ in range(N_SELECT - N_FORCED):
                    m = jnp.max(work, axis=0, keepdims=True)
                    first = jnp.min(jnp.where(work == m, blk_f, float(n_blk)), axis=0, keepdims=True)
                    work = jnp.where(blk_f == first, -jnp.inf, work)
                sel_ref[g * n_slc:g * n_slc + n_blk, cols] = jnp.where(forced | (work == -jnp.inf), 0.0, NEG)
                if n_blk < n_slc:
                    sel_ref[g * n_slc + n_blk:(g + 1) * n_slc, cols] = jnp.full((n_slc - n_blk, QSUB), NEG, F32)

        units = [(h, slice(qs * QSUB, (qs + 1) * QSUB)) for qs in range(tq // QSUB) for h in range(N_HEADS)]

        def scores(unit):
            h, cols = unit
            p, e = divmod(h, 2)
            g = p // PAIRS_PER_GROUP
            s = _dot_nt(kc_ref[g, e, :n_keys, :], q_ref[cols, p * LANES:(p + 1) * LANES]) + bias[:, cols]
            s = s.astype(BF16)
            return s, jnp.max(s, axis=0, keepdims=True)

        importance = [None] * N_GROUPS
        pending = [scores(u) for u in units[:CMP_LOOKAHEAD]]
        for n, (h, cols) in enumerate(units):
            g = h // HEADS_PER_GROUP
            if n + CMP_LOOKAHEAD < len(units):
                pending.append(scores(units[n + CMP_LOOKAHEAD]))
            s, m = pending.pop(0)
            pr = jnp.exp2(s - m)
            res = _dot(lhs[g], pr)
            inv_l = jnp.where(any_valid[:, cols], 1.0 / res[HEAD_DIM:HEAD_DIM + 1, :], 0.0)
            o_h = res[:HEAD_DIM, :] * (inv_l * gates[3 * h:3 * h + 1, cols])
            o_ref[h * HEAD_DIM:(h + 1) * HEAD_DIM, cols] = o_h.astype(BF16)
            imp_h = res[imp_row:, :] * inv_l
            importance[g] = imp_h if importance[g] is None else importance[g] + imp_h
            if h == N_HEADS - 1:
                select(cols, importance)
                importance = [None] * N_GROUPS

    chunk_tokens = CMP_KEY_CHUNK * CMP_STRIDE
    variant = (q0 + tq - 1) // chunk_tokens
    for v in range(n_cmp // CMP_KEY_CHUNK):
        pl.when(variant == v)(functools.partial(body, (v + 1) * CMP_KEY_CHUNK,
                                                (v + 1) * chunk_tokens // SLC_BLOCK))


def _cmp_attention(p_arr, kc, vct, overlap_t, g3t):
    s = p_arr.shape[0]
    n_cmp = kc.shape[2]
    n_slc = s // SLC_BLOCK
    return pl.pallas_call(
        functools.partial(_cmp_kernel, n_slc=n_slc),
        grid=(s // CMP_TQ,),
        in_specs=[
            pl.BlockSpec((CMP_TQ, D_MODEL), lambda i: (i, P_Q // SEG_BLOCKS)),
            pl.BlockSpec((N_GROUPS, 2, n_cmp, LANES), lambda i: (0, 0, 0, 0)),
            pl.BlockSpec((N_GROUPS, HEAD_DIM, n_cmp), lambda i: (0, 0, 0)),
            pl.BlockSpec((n_slc, n_cmp), lambda i: (0, 0)),
            pl.BlockSpec((LANES, CMP_TQ), lambda i: (0, i)),
        ],
        out_specs=[
            pl.BlockSpec((D_MODEL, CMP_TQ), lambda i: (0, i)),
            pl.BlockSpec((N_GROUPS * n_slc, CMP_TQ), lambda i: (0, i)),
        ],
        out_shape=[
            jax.ShapeDtypeStruct((D_MODEL, s), BF16),
            jax.ShapeDtypeStruct((N_GROUPS * n_slc, s), F32),
        ],
        compiler_params=_params(("parallel",)),
        name="cmp_attention",
    )(p_arr, kc, vct, overlap_t, g3t)


def _overlap_matrix_t(n_cmp, n_slc):
    c0 = np.arange(n_cmp)[None, :] * CMP_STRIDE
    s0 = np.arange(n_slc)[:, None] * SLC_BLOCK
    ov = (c0 < s0 + SLC_BLOCK) & (c0 + CMP_BLOCK > s0)
    return jnp.asarray(ov, dtype=BF16)


SLC_TQ = 1024
SLC_TK = 1024
KSUB = 512
SLC_LOOKAHEAD = 4
ACC_ROWS = HEAD_DIM + BF16_ROWS


def _selected_kernel(qi_ref, ki_ref, fl_ref, q_ref, k_ref, vt_ref, g3t_ref, sel_ref, o_ref, m_ref, acc_ref, *,
                     n_slc):
    step = pl.program_id(0)
    tq = q_ref.shape[0]
    tk = k_ref.shape[0]
    flags = fl_ref[step]

    @pl.when((flags & 1) != 0)
    def _():
        m_ref[...] = jnp.full_like(m_ref, NEG)
        acc_ref[...] = jnp.zeros_like(acc_ref)

    def attend(first_subs, diagonal):
        vt = vt_ref[...]
        ones = jnp.ones((ACC_ROWS - HEAD_DIM, KSUB), BF16)
        sub_blocks = KSUB // SLC_BLOCK
        k_half, v_aug, bias, units = [], [], [], []
        for ks, first_sub in enumerate(first_subs):
            c0 = first_sub * QSUB
            rows_k = slice(ks * KSUB, (ks + 1) * KSUB)
            t = qi_ref[step] * tq + c0 + lax.broadcasted_iota(jnp.int32, (1, tq - c0), 1)
            key = ki_ref[step] * tk + ks * KSUB + lax.broadcasted_iota(jnp.int32, (KSUB, 1), 0)
            causal_bias = jnp.where(key <= t, 0.0, NEG)
            bias_ks = []
            for g in range(N_GROUPS):
                first_block = pl.multiple_of(
                    g * n_slc + ki_ref[step] * (tk // SLC_BLOCK) + ks * sub_blocks, sub_blocks)
                rows = sel_ref[pl.ds(first_block, sub_blocks), c0:]
                per_key = jnp.broadcast_to(rows[:, None, :], (sub_blocks, SLC_BLOCK, tq - c0)).reshape(KSUB, tq - c0)
                bias_ks.append(jnp.minimum(per_key, causal_bias))
            bias.append(bias_ks)
            k_half.append(_split_heads(k_ref[rows_k, :]))
            v_aug.append([jnp.concatenate([vt[g * HEAD_DIM:(g + 1) * HEAD_DIM, rows_k], ones], axis=0)
                          for g in range(N_GROUPS)])
            units += [(ks, h, qs) for qs in range(first_sub, tq // QSUB) for h in range(N_HEADS)]

        def visible_keys(ks, qs):
            if not diagonal:
                return KSUB
            return min(KSUB, (qs + 1) * QSUB - (tq - tk) - ks * KSUB)

        def scores(unit):
            ks, h, qs = unit
            p, e = divmod(h, 2)
            g = p // PAIRS_PER_GROUP
            c0 = first_subs[ks] * QSUB
            n_keys = visible_keys(ks, qs)
            q = q_ref[qs * QSUB:(qs + 1) * QSUB, p * LANES:(p + 1) * LANES]
            s = (_dot_nt(k_half[ks][g][e][:n_keys], q)
                 + bias[ks][g][:n_keys, qs * QSUB - c0:(qs + 1) * QSUB - c0]).astype(BF16)
            return s, jnp.max(s, axis=0, keepdims=True)

        pending = [scores(u) for u in units[:SLC_LOOKAHEAD]]
        for n, (ks, h, qs) in enumerate(units):
            g = h // HEADS_PER_GROUP
            cols = slice(qs * QSUB, (qs + 1) * QSUB)
            if n + SLC_LOOKAHEAD < len(units):
                pending.append(scores(units[n + SLC_LOOKAHEAD]))
            s, s_max = pending.pop(0)
            m_prev = m_ref[h:h + 1, cols]
            m_new = jnp.maximum(m_prev, s_max.astype(F32))
            alpha = jnp.exp2(m_prev - m_new)
            pr = jnp.exp2(s - m_new.astype(BF16))
            m_ref[h:h + 1, cols] = m_new
            acc_ref[h, :, cols] = (acc_ref[h, :, cols] * alpha
                                   + _dot(v_aug[ks][g][:, :visible_keys(ks, qs)], pr))

    n_ksub = tk // KSUB
    is_last = (flags & 2) != 0

    @pl.when(jnp.logical_not(is_last))
    def _():
        attend([0] * n_ksub, diagonal=False)

    @pl.when(is_last)
    def _():
        attend([(tq - tk + ks * KSUB) // QSUB for ks in range(n_ksub)], diagonal=True)
        gates = _sigmoid(g3t_ref[...].astype(F32))
        for h in range(N_HEADS):
            acc = acc_ref[h]
            scale = gates[3 * h + 1:3 * h + 2, :] / acc[HEAD_DIM:HEAD_DIM + 1, :]
            o_ref[h * HEAD_DIM:(h + 1) * HEAD_DIM, :] = (acc[:HEAD_DIM, :] * scale).astype(BF16)


def _causal_tile_tables(s, tq, tk):
    qi, ki, fl = [], [], []
    for i in range(s // tq):
        last = (i * tq + tq - 1) // tk
        for k in range(last + 1):
            qi.append(i)
            ki.append(k)
            fl.append((1 if k == 0 else 0) | (2 if k == last else 0))
    return (jnp.asarray(qi, jnp.int32), jnp.asarray(ki, jnp.int32), jnp.asarray(fl, jnp.int32))


def _selected_attention(p_arr, vt, g3t, sel_t):
    s = p_arr.shape[0]
    n_slc = s // SLC_BLOCK
    tq, tk = SLC_TQ, SLC_TK
    assert tq % tk == 0 and tk % KSUB == 0 and KSUB % QSUB == 0, "the kernel's slice skipping needs nested tiles"
    qi, ki, fl = _causal_tile_tables(s, tq, tk)
    grid_spec = pltpu.PrefetchScalarGridSpec(
        num_scalar_prefetch=3,
        grid=(qi.shape[0],),
        in_specs=[
            pl.BlockSpec((tq, D_MODEL), lambda n, qi, ki, fl: (qi[n], P_Q // SEG_BLOCKS)),
            pl.BlockSpec((tk, LANES), lambda n, qi, ki, fl: (ki[n], P_KS)),
            pl.BlockSpec((LANES, tk), lambda n, qi, ki, fl: (0, ki[n])),
            pl.BlockSpec((LANES, tq), lambda n, qi, ki, fl: (0, qi[n])),
            pl.BlockSpec((N_GROUPS * n_slc, tq), lambda n, qi, ki, fl: (0, qi[n])),
        ],
        out_specs=pl.BlockSpec((D_MODEL, tq), lambda n, qi, ki, fl: (0, qi[n])),
        scratch_shapes=[
            pltpu.VMEM((N_HEADS, tq), F32),
            pltpu.VMEM((N_HEADS, ACC_ROWS, tq), F32),
        ],
    )
    return pl.pallas_call(
        functools.partial(_selected_kernel, n_slc=n_slc),
        grid_spec=grid_spec,
        out_shape=jax.ShapeDtypeStruct((D_MODEL, s), BF16),
        compiler_params=_params(("arbitrary",)),
        name="selected_attention",
    )(qi, ki, fl, p_arr, p_arr, vt, g3t, sel_t)


WINDOW_TQ = 1024
WINDOW_BLOCK = 256
WINDOW_LOOKAHEAD = 4


def _window_kernel(q_ref, *refs, n_blocks):
    k_refs, vt_refs = refs[:n_blocks], refs[n_blocks:2 * n_blocks]
    g3t_ref, o_ref = refs[2 * n_blocks:]
    tq = q_ref.shape[0]
    span = WINDOW + QSUB
    k_half = _split_heads(jnp.concatenate([r[...] for r in k_refs], axis=0))
    vt = jnp.concatenate([r[...] for r in vt_refs], axis=1)
    ones = jnp.ones((BF16_ROWS, vt.shape[1]), BF16)
    v_aug = [jnp.concatenate([vt[g * HEAD_DIM:(g + 1) * HEAD_DIM, :], ones], axis=0) for g in range(N_GROUPS)]
    gates = _sigmoid(g3t_ref[...].astype(F32))

    r = lax.broadcasted_iota(jnp.int32, (span, 1), 0)
    c = lax.broadcasted_iota(jnp.int32, (1, QSUB), 1)
    in_band = (r > c) & (r <= c + WINDOW)
    first_key = pl.program_id(0) * tq - WINDOW
    n_sub = tq // QSUB
    bias = [jnp.where(in_band & (first_key + qs * QSUB + r >= 0), 0.0, NEG) for qs in range(n_sub)]

    units = [(h, qs) for qs in range(n_sub) for h in range(N_HEADS)]

    def scores(unit):
        h, qs = unit
        p, e = divmod(h, 2)
        g = p // PAIRS_PER_GROUP
        q = q_ref[qs * QSUB:(qs + 1) * QSUB, p * LANES:(p + 1) * LANES]
        s = (_dot_nt(k_half[g][e][qs * QSUB:qs * QSUB + span, :], q) + bias[qs]).astype(BF16)
        return s, jnp.max(s, axis=0, keepdims=True)

    pending = [scores(u) for u in units[:WINDOW_LOOKAHEAD]]
    for n, (h, qs) in enumerate(units):
        g = h // HEADS_PER_GROUP
        cols = slice(qs * QSUB, (qs + 1) * QSUB)
        if n + WINDOW_LOOKAHEAD < len(units):
            pending.append(scores(units[n + WINDOW_LOOKAHEAD]))
        s, m = pending.pop(0)
        res = _dot(v_aug[g][:, qs * QSUB:qs * QSUB + span], jnp.exp2(s - m))
        scale = gates[3 * h + 2:3 * h + 3, cols] / res[HEAD_DIM:HEAD_DIM + 1, :]
        o_ref[h * HEAD_DIM:(h + 1) * HEAD_DIM, cols] = (res[:HEAD_DIM, :] * scale).astype(BF16)


def _window_attention(p_arr, vt, g3t):
    s = p_arr.shape[0]
    tq = WINDOW_TQ
    n_blocks = (WINDOW + tq) // WINDOW_BLOCK
    back = WINDOW // WINDOW_BLOCK
    per_tile = tq // WINDOW_BLOCK

    def block(j):
        return lambda i: jnp.maximum(i * per_tile - back + j, 0)

    k_specs = [pl.BlockSpec((WINDOW_BLOCK, LANES), lambda i, b=block(j): (b(i), P_KW)) for j in range(n_blocks)]
    v_specs = [pl.BlockSpec((LANES, WINDOW_BLOCK), lambda i, b=block(j): (0, b(i))) for j in range(n_blocks)]
    return pl.pallas_call(
        functools.partial(_window_kernel, n_blocks=n_blocks),
        grid=(s // tq,),
        in_specs=[pl.BlockSpec((tq, D_MODEL), lambda i: (i, P_Q // SEG_BLOCKS))] + k_specs + v_specs
        + [pl.BlockSpec((LANES, tq), lambda i: (0, i))],
        out_specs=pl.BlockSpec((D_MODEL, tq), lambda i: (0, i)),
        out_shape=jax.ShapeDtypeStruct((D_MODEL, s), BF16),
        compiler_params=_params(("parallel",)),
        name="window_attention",
    )(p_arr, *([p_arr] * n_blocks), *([vt] * n_blocks), g3t)


OUT_TM = 512


def _out_kernel(x_ref, oc_ref, os_ref, ow_ref, conv_ref, ga_ref, gb_ref, wa_ref, wb_ref, wo_ref, o_ref):
    y_b = _dot(conv_ref[...], wb_ref[...])

    o_nsa_t = oc_ref[...].astype(F32) + os_ref[...].astype(F32) + ow_ref[...].astype(F32)
    y_a = _dot(o_nsa_t.T.astype(BF16), wa_ref[...])

    merged = _sigmoid(ga_ref[...].astype(F32)) * y_a + _sigmoid(gb_ref[...].astype(F32)) * y_b
    o_ref[...] = x_ref[...] + _dot(merged.astype(BF16), wo_ref[...])


def _out(x1, o_cmp_t, o_slc_t, o_win_t, p_arr, w_nsa_out, w_conv_out, w_out):
    s = x1.shape[0]
    tm = OUT_TM
    row = lambda i: (i, 0)
    col = lambda i: (0, i)

    def seg(c):
        return pl.BlockSpec((tm, D_MODEL), lambda i: (i, c // SEG_BLOCKS))

    full = lambda shape: pl.BlockSpec(shape, lambda i: (0, 0))
    return pl.pallas_call(
        _out_kernel,
        grid=(s // tm,),
        in_specs=[
            pl.BlockSpec((tm, D_MODEL), row), pl.BlockSpec((D_MODEL, tm), col),
            pl.BlockSpec((D_MODEL, tm), col), pl.BlockSpec((D_MODEL, tm), col),
            seg(P_CB), seg(P_GA), seg(P_GB),
            full((D_MODEL, D_MODEL)), full((CONV_WIDTH, D_MODEL)), full((D_MODEL, D_MODEL)),
        ],
        out_specs=pl.BlockSpec((tm, D_MODEL), row),
        out_shape=jax.ShapeDtypeStruct((s, D_MODEL), F32),
        compiler_params=_params(("parallel",)),
        name="merge_out",
    )(x1, o_cmp_t, o_slc_t, o_win_t, p_arr, p_arr, p_arr, w_nsa_out, w_conv_out, w_out)


def _layer(x, ffn1_norm, ffn1_w_gate, ffn1_w_up, ffn1_w_down, mix_norm, w_in, cmp_pe_k, cmp_pe_v,
           cmp_k_w1, cmp_k_w2, cmp_v_w1, cmp_v_w2, conv_w, w_nsa_out, w_conv_out, w_out,
           ffn2_norm, ffn2_w_gate, ffn2_w_up, ffn2_w_down, final_g, final_norm):
    s = x.shape[0]
    n_cmp = s // CMP_STRIDE
    n_slc = s // SLC_BLOCK
    row = lambda v: v.reshape(1, -1).astype(F32)

    x1 = _ffn(x, row(ffn1_norm), ffn1_w_gate, ffn1_w_up, ffn1_w_down, row(final_g), False)
    p_arr, vst, vwt, g3t = _proj(x1, row(mix_norm), _proj_weight(w_in), conv_w.astype(F32))

    col = lambda c: p_arr[:, c * LANES:(c + 1) * LANES]
    chunks = lambda c: col(c).reshape(n_cmp, CMP_STRIDE * LANES)
    kc = _compress(chunks(P_KC), *_compress_weights(cmp_pe_k, cmp_k_w1, cmp_k_w2, False), False)
    vct = _compress(chunks(P_VC), *_compress_weights(cmp_pe_v, cmp_v_w1, cmp_v_w2, True), True)

    o_cmp_t, sel_t = _cmp_attention(p_arr, kc, vct, _overlap_matrix_t(n_cmp, n_slc), g3t)
    o_slc_t = _selected_attention(p_arr, vst, g3t, sel_t)
    o_win_t = _window_attention(p_arr, vwt, g3t)
    x2 = _out(x1, o_cmp_t, o_slc_t, o_win_t, p_arr, w_nsa_out.astype(BF16), w_conv_out.astype(BF16),
              w_out.astype(BF16))
    return _ffn(x2, row(ffn2_norm), ffn2_w_gate, ffn2_w_up, ffn2_w_down, row(final_g), final_norm)


def kernel(x, ffn1_norm, ffn1_w_gate, ffn1_w_up, ffn1_w_down, mix_norm, w_in, cmp_pe_k, cmp_pe_v,
           cmp_k_w1, cmp_k_w2, cmp_v_w1, cmp_v_w2, conv_w, w_nsa_out, w_conv_out, w_out, ffn2_norm,
           ffn2_w_gate, ffn2_w_up, ffn2_w_down, final_norm):
    batch, _, _ = x.shape
    depth = ffn1_norm.shape[0]
    outs = []
    for b in range(batch):
        xb = x[b]
        for l in range(depth):
            xb = _layer(xb, ffn1_norm[l], ffn1_w_gate[l], ffn1_w_up[l], ffn1_w_down[l], mix_norm[l],
                        w_in[l], cmp_pe_k[l], cmp_pe_v[l], cmp_k_w1[l], cmp_k_w2[l], cmp_v_w1[l],
                        cmp_v_w2[l], conv_w[l], w_nsa_out[l], w_conv_out[l], w_out[l], ffn2_norm[l],
                        ffn2_w_gate[l], ffn2_w_up[l], ffn2_w_down[l], final_norm, l == depth - 1)
        outs.append(xb)
    return outs[0][None] if batch == 1 else jnp.stack(outs)
```
